```python
import math
import jax, jax.numpy as jnp
from jax import lax
import numpy as np

D_MODEL = 2048
BATCH = 4
SEQ = 2048
DEPTH = 1
DEC_BATCH = 128
DEC_SEQ = 8
PAST_LEN = 16384
PAGE_SIZE = 128

D_PLE = 256
D_CONV = D_MODEL
CONV_A_W = 3
D_INNER = 2 * D_MODEL
SSM_HEAD_DIM = 64
SSM_HEADS = D_INNER // SSM_HEAD_DIM
SSM_GROUPS = 8
SSM_STATE = 128
SSM_CONV_W = 4
SSM_CHUNK = 128
D_XBC = D_INNER + 2 * SSM_GROUPS * SSM_STATE
N_EXPERTS = 32
TOP_K = 4
D_FF = D_MODEL
SWIGLU_LIMIT = 7.0
SWIGLU_ALPHA = 1.702
MOE_BLOCK = 128
EPS = 1e-6
D_IN_PROJ = 3 * D_CONV + D_INNER + D_XBC + SSM_HEADS + 2 * D_MODEL

kernel_name = 'hybrid_shortconv_ssd_moe_step'


def rmsnorm(x, g):
    xf = x.astype(jnp.float32)
    y = xf * lax.rsqrt(jnp.mean(xf * xf, axis=-1, keepdims=True) + EPS)
    return (y * g.astype(jnp.float32)).astype(x.dtype)


def causal_dwconv(prev, u, w, b=None):
    width = w.shape[0]
    t = u.shape[1]
    full = jnp.concatenate([prev.astype(u.dtype), u], axis=1)
    out = sum(full[:, k:k + t] * w[k] for k in range(width))
    if b is not None:
        out = out + b
    return out, full[:, full.shape[1] - (width - 1):]


def segsum(a):
    t = a.shape[-1]
    rep = jnp.broadcast_to(a[..., :, None], a.shape + (t,))
    strict = jnp.tril(jnp.ones((t, t), dtype=bool), -1)
    cs = jnp.cumsum(jnp.where(strict, rep, 0.0), axis=-2)
    return jnp.where(jnp.tril(jnp.ones((t, t), dtype=bool)), cs, -jnp.inf)


def ssd_scan(x, dt, a, bmat, cmat, h0):
    f32 = jnp.float32
    bsz, t = x.shape[0], x.shape[1]
    l = min(SSM_CHUNK, t)
    pad = (-t) % l
    q = SSM_HEADS // SSM_GROUPS
    xdt = x.astype(f32) * dt[..., None]
    adt = dt * a
    bm = bmat.astype(f32)
    cm = cmat.astype(f32)
    if pad:
        padt = lambda z: jnp.pad(z, [(0, 0), (0, pad)] + [(0, 0)] * (z.ndim - 2))
        xdt, adt, bm, cm = padt(xdt), padt(adt), padt(bm), padt(cm)
    c = (t + pad) // l
    xdt = xdt.reshape(bsz, c, l, SSM_GROUPS, q, SSM_HEAD_DIM)
    adt = adt.reshape(bsz, c, l, SSM_GROUPS, q).transpose(0, 3, 4, 1, 2)
    bm = bm.reshape(bsz, c, l, SSM_GROUPS, SSM_STATE)
    cm = cm.reshape(bsz, c, l, SSM_GROUPS, SSM_STATE)
    a_cum = jnp.cumsum(adt, axis=-1)
    decay_in = jnp.exp(segsum(adt))
    cb = jnp.einsum('bclgn,bcsgn->bgcls', cm, bm)
    y_diag = jnp.einsum('bgcls,bgqcls,bcsgqp->bclgqp', cb, decay_in, xdt)
    decay_to_end = jnp.exp(a_cum[..., -1:] - a_cum)
    states = jnp.einsum('bcsgn,bgqcs,bcsgqp->bcgqpn', bm, decay_to_end, xdt)
    h0g = h0.astype(f32).reshape(bsz, SSM_GROUPS, q, SSM_HEAD_DIM, SSM_STATE)
    states = jnp.concatenate([h0g[:, None], states], axis=1)
    chunk_tot = jnp.pad(a_cum[..., -1], [(0, 0), (0, 0), (0, 0), (1, 0)])
    decay_chunk = jnp.exp(segsum(chunk_tot))
    states = jnp.einsum('bgqzc,bcgqpn->bzgqpn', decay_chunk, states)
    h_prev, h_last = states[:, :-1], states[:, -1]
    y_off = jnp.einsum('bclgn,bcgqpn,bgqcl->bclgqp', cm, h_prev, jnp.exp(a_cum))
    y = (y_diag + y_off).reshape(bsz, c * l, SSM_HEADS, SSM_HEAD_DIM)[:, :t]
    return y, h_last.reshape(bsz, SSM_HEADS, SSM_HEAD_DIM, SSM_STATE)


def shortconv_branch(ax, ab, ac, conv_prev, conv_w):
    v, conv_new = causal_dwconv(conv_prev, ac * ax, conv_w)
    return ab * v, conv_new


def mamba2_branch(z, xbc_raw, dt_raw, conv_prev, ssm_prev, conv_w, conv_b, dt_bias, a_log, d_skip, norm_g):
    f32 = jnp.float32
    bsz, t = z.shape[0], z.shape[1]
    xbc, conv_new = causal_dwconv(conv_prev, xbc_raw, conv_w, conv_b)
    xbc = jax.nn.silu(xbc)
    gn = SSM_GROUPS * SSM_STATE
    xs = xbc[..., :D_INNER].reshape(bsz, t, SSM_HEADS, SSM_HEAD_DIM)
    bm = xbc[..., D_INNER:D_INNER + gn].reshape(bsz, t, SSM_GROUPS, SSM_STATE)
    cm = xbc[..., D_INNER + gn:].reshape(bsz, t, SSM_GROUPS, SSM_STATE)
    dt = jax.nn.softplus(dt_raw.astype(f32) + dt_bias.astype(f32))
    a = -jnp.exp(a_log.astype(f32))
    y, ssm_new = ssd_scan(xs, dt, a, bm, cm, ssm_prev)
    y = y + d_skip.astype(f32)[:, None] * xs.astype(f32)
    y = y.reshape(bsz, t, D_INNER) * jax.nn.silu(z.astype(f32))
    yg = y.reshape(bsz, t, SSM_GROUPS, D_INNER // SSM_GROUPS)
    yg = yg * lax.rsqrt(jnp.mean(yg * yg, axis=-1, keepdims=True) + EPS)
    y = yg.reshape(bsz, t, D_INNER) * norm_g.astype(f32)
    return y.astype(z.dtype), conv_new, ssm_new.astype(ssm_prev.dtype)


def moe_ffn(xn, w_router, b_router, w_up, b_up, w_down, b_down):
    n_tok = xn.shape[0]
    logits = xn.astype(jnp.float32) @ w_router.astype(jnp.float32) + b_router.astype(jnp.float32)
    top_v, top_i = lax.top_k(logits, TOP_K)
    gates = jax.nn.softmax(top_v, axis=-1)
    n_assign = n_tok * TOP_K
    flat_e = top_i.reshape(-1)
    flat_t = jnp.repeat(jnp.arange(n_tok, dtype=jnp.int32), TOP_K)
    flat_g = gates.reshape(-1)
    order = jnp.argsort(flat_e, stable=True)
    se, st, sg = flat_e[order], flat_t[order], flat_g[order]
    counts = jnp.bincount(flat_e, length=N_EXPERTS)
    padded = (counts + MOE_BLOCK - 1) // MOE_BLOCK * MOE_BLOCK
    start = jnp.cumsum(counts) - counts
    pend = jnp.cumsum(padded)
    pstart = pend - padded
    dest = pstart[se] + (jnp.arange(n_assign, dtype=jnp.int32) - start[se])
    n_blocks = -(-n_assign // MOE_BLOCK) + N_EXPERTS
    rows = n_blocks * MOE_BLOCK
    tok_buf = jnp.full((rows,), n_tok, jnp.int32).at[dest].set(st)
    gate_buf = jnp.zeros((rows,), jnp.float32).at[dest].set(sg)
    block_e = jnp.clip(jnp.searchsorted(pend, jnp.arange(n_blocks) * MOE_BLOCK, side='right'), 0, N_EXPERTS - 1)
    x_pad = jnp.concatenate([xn, jnp.zeros((1, xn.shape[1]), xn.dtype)], axis=0)
    xb = x_pad[tok_buf].reshape(n_blocks, MOE_BLOCK, xn.shape[1])

    def expert_block(args):
        xblk, e = args
        h = xblk @ w_up[e] + b_up[e]
        g, u = h[:, :D_FF], h[:, D_FF:]
        g = jnp.minimum(g, SWIGLU_LIMIT)
        u = jnp.clip(u, -SWIGLU_LIMIT, SWIGLU_LIMIT)
        act = g * jax.nn.sigmoid(SWIGLU_ALPHA * g) * (u + 1.0)
        return act @ w_down[e] + b_down[e]

    yb = lax.map(expert_block, (xb, block_e)).reshape(rows, xn.shape[1])
    y = jax.ops.segment_sum(yb * gate_buf[:, None].astype(yb.dtype), tok_buf, num_segments=n_tok + 1)
    return y[:n_tok]


def layer(x, p, conv_a_prev, conv_m_prev, ssm_prev, g_mix, w_in, conv_a_w, ssm_conv_w, ssm_conv_b,
          dt_bias, a_log, d_skip, ssm_norm_g, w_branch_a, w_branch_b, w_out, g_ffn, w_router, b_router,
          w_up, b_up, w_down, b_down, g_ple, w_ple_gate, w_ple_up):
    bsz, t, d = x.shape
    n1 = rmsnorm(x, g_mix)
    proj = n1 @ w_in
    cuts = np.cumsum([D_CONV, D_CONV, D_CONV, D_INNER, D_XBC, SSM_HEADS, D_MODEL]).tolist()
    ax, ab, ac, mz, mxbc, mdt, gate_a, gate_b = jnp.split(proj, cuts, axis=-1)
    ya, conv_a_new = shortconv_branch(ax, ab, ac, conv_a_prev, conv_a_w)
    ym, conv_m_new, ssm_new = mamba2_branch(mz, mxbc, mdt, conv_m_prev, ssm_prev, ssm_conv_w, ssm_conv_b,
                                            dt_bias, a_log, d_skip, ssm_norm_g)
    merged = jax.nn.sigmoid(gate_a) * (ya @ w_branch_a) + jax.nn.sigmoid(gate_b) * (ym @ w_branch_b)
    h = x + merged @ w_out
    n2 = rmsnorm(h, g_ffn)
    h = h + moe_ffn(n2.reshape(bsz * t, d), w_router, b_router, w_up, b_up, w_down, b_down).reshape(bsz, t, d)
    ple_gate = jax.nn.sigmoid(rmsnorm(h, g_ple) @ w_ple_gate)
    h = h + ple_gate * (p.astype(h.dtype) @ w_ple_up)
    return h, conv_a_new, conv_m_new, ssm_new


def setup_inputs(seed: int = 0) -> dict:
    key = jax.random.key(seed)
    ks = jax.random.split(key, 40)
    f32 = jnp.float32
    L = DEPTH
    nrm = lambda k, shape, scale: jax.random.normal(k, shape, f32) * scale
    gain = lambda k, shape: 1.0 + 0.02 * jax.random.normal(k, shape, f32)
    dt0 = jnp.exp(jax.random.uniform(ks[10], (L, SSM_HEADS), f32, math.log(1e-3), math.log(1e-1)))
    return {
        'x_prompt': nrm(ks[0], (BATCH, SEQ, D_MODEL), 1.0),
        'x_sample': nrm(ks[1], (DEC_BATCH, DEC_SEQ, D_MODEL), 1.0),
        'p_prompt': nrm(ks[2], (L, BATCH, SEQ, D_PLE), 1.0),
        'p_sample': nrm(ks[3], (L, DEC_BATCH, DEC_SEQ, D_PLE), 1.0),
        'state_conv_a': nrm(ks[4], (L, DEC_BATCH, CONV_A_W - 1, D_CONV), 1.0),
        'state_conv_ssm': nrm(ks[5], (L, DEC_BATCH, SSM_CONV_W - 1, D_XBC), 1.0),
        'state_ssm': nrm(ks[6], (L, DEC_BATCH, SSM_HEADS, SSM_HEAD_DIM, SSM_STATE), 0.1),
        'g_mix': gain(ks[7], (L, D_MODEL)),
        'w_in': nrm(ks[8], (L, D_MODEL, D_IN_PROJ), D_MODEL ** -0.5),
        'conv_a_w': nrm(ks[9], (L, CONV_A_W, D_CONV), CONV_A_W ** -0.5),
        'ssm_conv_w': nrm(ks[11], (L, SSM_CONV_W, D_XBC), SSM_CONV_W ** -0.5),
        'ssm_conv_b': nrm(ks[12], (L, D_XBC), 0.01),
        'dt_bias': dt0 + jnp.log(-jnp.expm1(-dt0)),
        'a_log': jnp.log(jax.random.uniform(ks[13], (L, SSM_HEADS), f32, 1.0, 16.0)),
        'd_skip': gain(ks[14], (L, SSM_HEADS)),
        'ssm_norm_g': gain(ks[15], (L, D_INNER)),
        'w_branch_a': nrm(ks[16], (L, D_CONV, D_MODEL), D_CONV ** -0.5),
        'w_branch_b': nrm(ks[17], (L, D_INNER, D_MODEL), D_INNER ** -0.5),
        'w_out': nrm(ks[18], (L, D_MODEL, D_MODEL), D_MODEL ** -0.5),
        'g_ffn': gain(ks[19], (L, D_MODEL)),
        'w_router': nrm(ks[20], (L, D_MODEL, N_EXPERTS), D_MODEL ** -0.5),
        'b_router': nrm(ks[21], (L, N_EXPERTS), 0.01),
        'w_up': nrm(ks[22], (L, N_EXPERTS, D_MODEL, 2 * D_FF), D_MODEL ** -0.5),
        'b_up': nrm(ks[23], (L, N_EXPERTS, 2 * D_FF), 0.01),
        'w_down': nrm(ks[24], (L, N_EXPERTS, D_FF, D_MODEL), D_FF ** -0.5),
        'b_down': nrm(ks[25], (L, N_EXPERTS, D_MODEL), 0.01),
        'g_ple': gain(ks[26], (L, D_MODEL)),
        'w_ple_gate': nrm(ks[27], (L, D_MODEL, D_MODEL), D_MODEL ** -0.5),
        'w_ple_up': nrm(ks[28], (L, D_PLE, D_MODEL), D_PLE ** -0.5),
        'g_final': gain(ks[29], (D_MODEL,)),
    }


def reference(x_prompt, x_sample, p_prompt, p_sample, state_conv_a, state_conv_ssm, state_ssm,
              g_mix, w_in, conv_a_w, ssm_conv_w, ssm_conv_b, dt_bias, a_log, d_skip, ssm_norm_g,
              w_branch_a, w_branch_b, w_out, g_ffn, w_router, b_router, w_up, b_up, w_down, b_down,
              g_ple, w_ple_gate, w_ple_up, g_final):
    hp, hs = x_prompt, x_sample
    ca_p, cm_p, s_p, ca_s, cm_s, s_s = [], [], [], [], [], []
    for i in range(DEPTH):
        lw = (g_mix[i], w_in[i], conv_a_w[i], ssm_conv_w[i], ssm_conv_b[i], dt_bias[i], a_log[i],
              d_skip[i], ssm_norm_g[i], w_branch_a[i], w_branch_b[i], w_out[i], g_ffn[i], w_router[i],
              b_router[i], w_up[i], b_up[i], w_down[i], b_down[i], g_ple[i], w_ple_gate[i], w_ple_up[i])
        b0 = hp.shape[0]
        hp, a_new, m_new, ssm_new = layer(
            hp, p_prompt[i],
            jnp.zeros((b0, CONV_A_W - 1, D_CONV), hp.dtype),
            jnp.zeros((b0, SSM_CONV_W - 1, D_XBC), hp.dtype),
            jnp.zeros((b0, SSM_HEADS, SSM_HEAD_DIM, SSM_STATE), state_ssm.dtype),
            *lw)
        ca_p.append(a_new); cm_p.append(m_new); s_p.append(ssm_new)
        hs, a_new, m_new, ssm_new = layer(hs, p_sample[i], state_conv_a[i], state_conv_ssm[i], state_ssm[i], *lw)
        ca_s.append(a_new); cm_s.append(m_new); s_s.append(ssm_new)
    y_prompt = rmsnorm(hp, g_final)
    y_sample = rmsnorm(hs, g_final)
    return (y_prompt, y_sample, jnp.stack(ca_p), jnp.stack(cm_p), jnp.stack(s_p),
            jnp.stack(ca_s), jnp.stack(cm_s), jnp.stack(s_s))
```

```python
import functools

import jax
import jax.numpy as jnp
from jax import lax
from jax.experimental import pallas as pl
from jax.experimental.pallas import tpu as pltpu

F32 = jnp.float32
BF16 = jnp.bfloat16
I32 = jnp.int32
HI = lax.Precision.HIGHEST

D_MODEL = 2048
BATCH = 4
SEQ = 2048
DEC_BATCH = 128
DEC_SEQ = 8
D_PLE = 256
D_CONV = D_MODEL
CONV_A_W = 3
D_INNER = 2 * D_MODEL
SSM_HEAD_DIM = 64
SSM_HEADS = D_INNER // SSM_HEAD_DIM
SSM_GROUPS = 8
SSM_STATE = 128
SSM_CONV_W = 4
SSM_CHUNK = 128
D_XBC = D_INNER + 2 * SSM_GROUPS * SSM_STATE
N_EXPERTS = 32
TOP_K = 4
D_FF = D_MODEL
SWIGLU_LIMIT = 7.0
SWIGLU_ALPHA = 1.702
EPS = 1e-6

NP_TOK = BATCH * SEQ
NS_TOK = DEC_BATCH * DEC_SEQ
N_TOK = NP_TOK + NS_TOK
N_ASSIGN = N_TOK * TOP_K

LANES = 128
SUBLANES = 8
ROW_TILE = D_MODEL // LANES

COL_AX = 0
COL_AB = D_CONV
COL_AC = 2 * D_CONV
COL_Z = 3 * D_CONV
COL_XBC = COL_Z + D_INNER
COL_DT = COL_XBC + D_XBC
COL_GATE = COL_DT + SSM_HEADS
N_MAIN = COL_DT

MOE_SUB = 256
MOE_TS = 512
MOE_NSB = MOE_TS // MOE_SUB
MOE_ROWS = N_ASSIGN + N_EXPERTS * MOE_SUB
MOE_SMAX = N_ASSIGN // MOE_TS + N_EXPERTS
MOE_TN = 512
MOE_J1 = D_FF // MOE_TN
MOE_J2 = D_MODEL // MOE_TN

MIB = 1024 * 1024


def _cparams(n_grid, vmem_mib):
    return pltpu.CompilerParams(dimension_semantics=("arbitrary",) * n_grid,
                                vmem_limit_bytes=vmem_mib * MIB)


def _sigmoid(x):
    return 1.0 / (1.0 + jnp.exp(-x))


def _softplus(x):
    return jnp.maximum(x, 0.0) + jnp.log1p(jnp.exp(-jnp.abs(x)))


def _rms_mm_body(x_ref, g_ref, w_ref, o_ref, n_ref):
    @pl.when(pl.program_id(1) == 0)
    def _():
        x = x_ref[...]
        ms = jnp.mean(x * x, axis=-1, keepdims=True)
        n_ref[...] = (x * lax.rsqrt(ms + EPS) * g_ref[...]).astype(BF16)

    o_ref[...] = jnp.dot(n_ref[...], w_ref[...].astype(BF16), preferred_element_type=F32)


def _rms_matmul(x, g, w, col0_blocks, ncols, tm, tn, name):
    n, k = x.shape
    return pl.pallas_call(
        _rms_mm_body,
        grid=(n // tm, ncols // tn),
        in_specs=[pl.BlockSpec((tm, k), lambda i, j: (i, 0)),
                  pl.BlockSpec((1, k), lambda i, j: (0, 0)),
                  pl.BlockSpec((k, tn), lambda i, j: (0, col0_blocks + j))],
        out_specs=pl.BlockSpec((tm, tn), lambda i, j: (i, j)),
        out_shape=jax.ShapeDtypeStruct((n, ncols), F32),
        scratch_shapes=[pltpu.VMEM((tm, k), BF16)],
        compiler_params=_cparams(2, 48),
        name=name,
    )(x, g, w)


def _causal_taps(u, w_ref, width, rowmod, prev):
    rows = u.shape[0]
    acc = u * w_ref[width - 1:width, :]
    for s in range(1, width):
        shifted = pltpu.roll(u, s, 0)
        if prev is None:
            fill = jnp.zeros_like(u)
        else:
            fill = pltpu.roll(prev, rows - SUBLANES + s, 0)
        acc = acc + jnp.where(rowmod >= s, shifted, fill) * w_ref[width - 1 - s:width - s, :]
    return acc


def _row_in_seq(shape, period):
    r = lax.broadcasted_iota(I32, shape, 0)
    return r if period >= shape[0] else r & (period - 1)


def _conv_a_body(period, has_prev, *refs):
    if has_prev:
        ax_ref, ab_ref, ac_ref, w_ref, prev_ref, ya_ref, u_ref = refs
        prev = prev_ref[...]
    else:
        ax_ref, ab_ref, ac_ref, w_ref, ya_ref, u_ref = refs
        prev = None
    u = ac_ref[...] * ax_ref[...]
    v = _causal_taps(u, w_ref, CONV_A_W, _row_in_seq(u.shape, period), prev)
    ya_ref[...] = (ab_ref[...] * v).astype(BF16)
    if has_prev:
        u_ref[...] = u
    else:
        u_ref[...] = u[u.shape[0] - SUBLANES:, :]


def _conv_m_body(period, has_prev, *refs):
    if has_prev:
        x_ref, w_ref, b_ref, prev_ref, o_ref = refs
        prev = prev_ref[...]
    else:
        x_ref, w_ref, b_ref, o_ref = refs
        prev = None
    u = x_ref[...]
    v = _causal_taps(u, w_ref, SSM_CONV_W, _row_in_seq(u.shape, period), prev) + b_ref[...]
    o_ref[...] = v * _sigmoid(v)


CONV_TC = 256


def _conv_a_prompt(proj, conv_w):
    nb = D_CONV // CONV_TC
    blk = lambda off: pl.BlockSpec((SEQ, CONV_TC), lambda b, c, off=off: (b, off // CONV_TC + c))
    return pl.pallas_call(
        functools.partial(_conv_a_body, SEQ, False),
        grid=(BATCH, nb),
        in_specs=[blk(COL_AX), blk(COL_AB), blk(COL_AC),
                  pl.BlockSpec((CONV_A_W, CONV_TC), lambda b, c: (0, c))],
        out_specs=[pl.BlockSpec((SEQ, CONV_TC), lambda b, c: (b, c)),
                   pl.BlockSpec((None, SUBLANES, CONV_TC), lambda b, c: (b, 0, c))],
        out_shape=[jax.ShapeDtypeStruct((NP_TOK, D_CONV), BF16),
                   jax.ShapeDtypeStruct((BATCH, SUBLANES, D_CONV), F32)],
        compiler_params=_cparams(2, 48),
        name="conv_a_prompt",
    )(proj, proj, proj, conv_w)


def _conv_a_sample(proj, conv_w, prev_pad):
    nb = D_CONV // CONV_TC
    rb = NP_TOK // NS_TOK
    blk = lambda off: pl.BlockSpec((NS_TOK, CONV_TC), lambda c, off=off: (rb, off // CONV_TC + c))
    own = pl.BlockSpec((NS_TOK, CONV_TC), lambda c: (0, c))
    return pl.pallas_call(
        functools.partial(_conv_a_body, DEC_SEQ, True),
        grid=(nb,),
        in_specs=[blk(COL_AX), blk(COL_AB), blk(COL_AC),
                  pl.BlockSpec((CONV_A_W, CONV_TC), lambda c: (0, c)), own],
        out_specs=[own, own],
        out_shape=[jax.ShapeDtypeStruct((NS_TOK, D_CONV), BF16),
                   jax.ShapeDtypeStruct((NS_TOK, D_CONV), F32)],
        compiler_params=_cparams(1, 48),
        name="conv_a_sample",
    )(proj, proj, proj, conv_w, prev_pad)


def _conv_m_prompt(proj, conv_w, conv_b):
    nb = D_XBC // CONV_TC
    return pl.pallas_call(
        functools.partial(_conv_m_body, SEQ, False),
        grid=(BATCH, nb),
        in_specs=[pl.BlockSpec((SEQ, CONV_TC), lambda b, c: (b, COL_XBC // CONV_TC + c)),
                  pl.BlockSpec((SSM_CONV_W, CONV_TC), lambda b, c: (0, c)),
                  pl.BlockSpec((1, CONV_TC), lambda b, c: (0, c))],
        out_specs=pl.BlockSpec((SEQ, CONV_TC), lambda b, c: (b, c)),
        out_shape=jax.ShapeDtypeStruct((NP_TOK, D_XBC), F32),
        compiler_params=_cparams(2, 48),
        name="conv_m_prompt",
    )(proj, conv_w, conv_b)


def _conv_m_sample(proj, conv_w, conv_b, prev_pad):
    nb = D_XBC // CONV_TC
    rb = NP_TOK // NS_TOK
    own = pl.BlockSpec((NS_TOK, CONV_TC), lambda c: (0, c))
    return pl.pallas_call(
        functools.partial(_conv_m_body, DEC_SEQ, True),
        grid=(nb,),
        in_specs=[pl.BlockSpec((NS_TOK, CONV_TC), lambda c: (rb, COL_XBC // CONV_TC + c)),
                  pl.BlockSpec((SSM_CONV_W, CONV_TC), lambda c: (0, c)),
                  pl.BlockSpec((1, CONV_TC), lambda c: (0, c)), own],
        out_specs=own,
        out_shape=jax.ShapeDtypeStruct((NS_TOK, D_XBC), F32),
        compiler_params=_cparams(1, 48),
        name="conv_m_sample",
    )(proj, conv_w, conv_b, prev_pad)


def _ssd_body(valid, has_h0, *refs):
    if has_h0:
        (xbc_in, zlo_ref, zhi_ref, dtr_in, dtb_ref, dtbc_ref, alog_ref, alogc_ref, dsk_ref, ng_ref,
         h0_ref, ym_ref, hl_ref, ht_ref, y_ref, xp_ref, dp_ref) = refs
    else:
        (xbc_in, zlo_ref, zhi_ref, dtr_in, dtb_ref, dtbc_ref, alog_ref, alogc_ref, dsk_ref, ng_ref,
         ym_ref, hl_ref, ht_ref, y_ref) = refs
    L = SSM_CHUNK
    H = SSM_HEADS
    c = pl.program_id(1)

    @pl.when(c == 0)
    def _():
        if has_h0:
            ht_ref[...] = h0_ref[...].T
        else:
            ht_ref[...] = jnp.zeros_like(ht_ref)

    if valid < L:
        @pl.when(jnp.logical_and(pl.program_id(0) == 0, c == 0))
        def _():
            xp_ref[...] = jnp.zeros_like(xp_ref)
            dp_ref[...] = jnp.zeros_like(dp_ref)

        xp_ref[0:valid, :] = xbc_in[...]
        dp_ref[0:valid, :] = dtr_in[...]
        xbc_ref, dtr = xp_ref, dp_ref[...]
    else:
        xbc_ref, dtr = xbc_in, dtr_in[...]

    dt = _softplus(dtr + dtb_ref[...])
    eye = (lax.broadcasted_iota(I32, (H, H), 0) == lax.broadcasted_iota(I32, (H, H), 1)).astype(F32)
    dtr_t = lax.dot_general(eye, dtr, (((1,), (1,)), ((), ())), precision=HI,
                            preferred_element_type=F32)
    dt_t = _softplus(dtr_t + dtbc_ref[...])
    if valid < L:
        dt = jnp.where(lax.broadcasted_iota(I32, (L, H), 0) < valid, dt, 0.0)
        dt_t = jnp.where(lax.broadcasted_iota(I32, (H, L), 1) < valid, dt_t, 0.0)
    adt = dt * (-jnp.exp(alog_ref[...]))
    adt_t = dt_t * (-jnp.exp(alogc_ref[...]))
    li = lax.broadcasted_iota(I32, (L, L), 0)
    si = lax.broadcasted_iota(I32, (L, L), 1)
    causal = li >= si
    a_cum = jnp.dot(causal.astype(F32), adt, precision=HI, preferred_element_type=F32)
    a_cum_t = jnp.dot(adt_t, (li <= si).astype(F32), precision=HI, preferred_element_type=F32)
    ea = jnp.exp(a_cum)
    a_tot_t = a_cum_t[:, L - 1:L]
    dte_t = jnp.exp(a_tot_t - a_cum_t)
    ea_tot_t = jnp.exp(a_tot_t)

    lane = lax.broadcasted_iota(I32, (L, LANES), 1)
    lo = lane < SSM_HEAD_DIM
    lane1 = lax.broadcasted_iota(I32, (1, LANES), 1) < SSM_HEAD_DIM
    gn = SSM_GROUPS * SSM_STATE
    for g in range(SSM_GROUPS):
        bg = xbc_ref[:, D_INNER + g * SSM_STATE:D_INNER + (g + 1) * SSM_STATE]
        cg = xbc_ref[:, D_INNER + gn + g * SSM_STATE:D_INNER + gn + (g + 1) * SSM_STATE]
        bg_t = bg.T
        cb = jnp.dot(cg.astype(BF16), bg_t.astype(BF16), preferred_element_type=F32)
        for jj in range(SSM_HEADS // SSM_GROUPS // 2):
            j = g * (SSM_HEADS // SSM_GROUPS // 2) + jj
            cols = slice(j * LANES, (j + 1) * LANES)
            lhs1, lhs2 = [], []
            for h in (2 * j, 2 * j + 1):
                diff = a_cum[:, h:h + 1] - a_cum_t[h:h + 1, :]
                decay = jnp.exp(jnp.where(causal, diff, -1e30))
                lhs1.append((cb * decay).astype(BF16))
            for h in (2 * j, 2 * j + 1):
                lhs1.append((cg * ea[:, h:h + 1]).astype(BF16))
                lhs2.append((bg_t * dte_t[h:h + 1, :]).astype(BF16))
            dt_pair = jnp.where(lo, dt[:, 2 * j:2 * j + 1], dt[:, 2 * j + 1:2 * j + 2])
            xdt = xbc_ref[:, cols] * dt_pair
            zero = jnp.zeros_like(xdt)
            x_top = jnp.where(lo, xdt, zero).astype(BF16)
            x_bot = jnp.where(lo, zero, xdt).astype(BF16)
            ht = ht_ref[:, cols]
            h_top = jnp.where(lo, ht, zero).astype(BF16)
            h_bot = jnp.where(lo, zero, ht).astype(BF16)
            y_ref[:, cols] = jnp.dot(jnp.concatenate(lhs1, axis=1),
                                     jnp.concatenate([x_top, x_bot, h_top, h_bot], axis=0),
                                     preferred_element_type=F32)
            upd = jnp.dot(jnp.concatenate(lhs2, axis=1), jnp.concatenate([x_top, x_bot], axis=0),
                          preferred_element_type=F32)
            keep = jnp.where(lane1, ea_tot_t[2 * j:2 * j + 1, :], ea_tot_t[2 * j + 1:2 * j + 2, :])
            ht_ref[:, cols] = ht * keep + upd

    gw = D_INNER // SSM_GROUPS
    for g in range(SSM_GROUPS):
        cols = slice(g * gw, (g + 1) * gw)
        zr = zlo_ref if g < SSM_GROUPS // 2 else zhi_ref
        zc = slice((g % (SSM_GROUPS // 2)) * gw, (g % (SSM_GROUPS // 2) + 1) * gw)
        z = zr[:, zc]
        y = y_ref[0:valid, cols] + dsk_ref[:, cols] * xbc_ref[0:valid, cols]
        y = y * (z * _sigmoid(z))
        y = y * lax.rsqrt(jnp.mean(y * y, axis=-1, keepdims=True) + EPS)
        ym_ref[:, cols] = (y * ng_ref[:, cols]).astype(ym_ref.dtype)

    @pl.when(c == pl.num_programs(1) - 1)
    def _():
        hl_ref[...] = ht_ref[...].T


def _ssd_call(valid, xbc, proj, dt_raw, dtb, dtbc, alog, alogc, dsk, ng, h0, *, n_seq, n_chunk, row0, name):
    has_h0 = h0 is not None
    hp = SSM_HEADS * SSM_HEAD_DIM
    rb0 = row0 // valid
    zb = D_INNER // 2
    in_specs = [pl.BlockSpec((valid, D_XBC), lambda b, c: (b * n_chunk + c, 0)),
                pl.BlockSpec((valid, zb), lambda b, c: (rb0 + b * n_chunk + c, COL_Z // zb)),
                pl.BlockSpec((valid, zb), lambda b, c: (rb0 + b * n_chunk + c, COL_Z // zb + 1)),
                pl.BlockSpec((valid, SSM_HEADS), lambda b, c: (rb0 + b * n_chunk + c, 0)),
                pl.BlockSpec((1, SSM_HEADS), lambda b, c: (0, 0)),
                pl.BlockSpec((SSM_HEADS, 1), lambda b, c: (0, 0)),
                pl.BlockSpec((1, SSM_HEADS), lambda b, c: (0, 0)),
                pl.BlockSpec((SSM_HEADS, 1), lambda b, c: (0, 0)),
                pl.BlockSpec((1, D_INNER), lambda b, c: (0, 0)),
                pl.BlockSpec((1, D_INNER), lambda b, c: (0, 0))]
    args = [xbc, proj, proj, dt_raw, dtb, dtbc, alog, alogc, dsk, ng]
    scratch = [pltpu.VMEM((SSM_STATE, hp), F32), pltpu.VMEM((SSM_CHUNK, hp), F32)]
    if has_h0:
        in_specs.append(pl.BlockSpec((None, hp, SSM_STATE), lambda b, c: (b, 0, 0)))
        args.append(h0)
        scratch += [pltpu.VMEM((SSM_CHUNK, D_XBC), F32), pltpu.VMEM((SSM_CHUNK, SSM_HEADS), F32)]
    ym_dtype = BF16 if valid % 16 == 0 else F32
    return pl.pallas_call(
        functools.partial(_ssd_body, valid, has_h0),
        grid=(n_seq, n_chunk),
        in_specs=in_specs,
        out_specs=[pl.BlockSpec((valid, D_INNER), lambda b, c: (b * n_chunk + c, 0)),
                   pl.BlockSpec((None, hp, SSM_STATE), lambda b, c: (b, 0, 0))],
        out_shape=[jax.ShapeDtypeStruct((n_seq * n_chunk * valid, D_INNER), ym_dtype),
                   jax.ShapeDtypeStruct((n_seq, hp, SSM_STATE), F32)],
        scratch_shapes=scratch,
        compiler_params=_cparams(2, 48),
        name=name,
    )(*args)


def _merge_body(ya_ref, ym_ref, wa_ref, wb_ref, ga_ref, gb_ref, o_ref):
    a = jnp.dot(ya_ref[...], wa_ref[...].astype(BF16), preferred_element_type=F32)
    b = jnp.dot(ym_ref[...], wb_ref[...].astype(BF16), preferred_element_type=F32)
    o_ref[...] = (_sigmoid(ga_ref[...]) * a + _sigmoid(gb_ref[...]) * b).astype(BF16)


def _merge(ya, ym, wa, wb, gates):
    tm, tn = 1024, 256
    return pl.pallas_call(
        _merge_body,
        grid=(N_TOK // tm, D_MODEL // tn),
        in_specs=[pl.BlockSpec((tm, D_CONV), lambda i, j: (i, 0)),
                  pl.BlockSpec((tm, D_INNER), lambda i, j: (i, 0)),
                  pl.BlockSpec((D_CONV, tn), lambda i, j: (0, j)),
                  pl.BlockSpec((D_INNER, tn), lambda i, j: (0, j)),
                  pl.BlockSpec((tm, tn), lambda i, j: (i, j)),
                  pl.BlockSpec((tm, tn), lambda i, j: (i, D_MODEL // tn + j))],
        out_specs=pl.BlockSpec((tm, tn), lambda i, j: (i, j)),
        out_shape=jax.ShapeDtypeStruct((N_TOK, D_MODEL), BF16),
        compiler_params=_cparams(2, 52),
        name="merge",
    )(ya, ym, wa, wb, gates, gates)


def _resid_body(m_ref, w_ref, x_ref, o_ref):
    o_ref[...] = x_ref[...] + jnp.dot(m_ref[...], w_ref[...].astype(BF16), preferred_element_type=F32)


def _resid_matmul(m, w, x):
    tm, tn = 1024, 512
    return pl.pallas_call(
        _resid_body,
        grid=(N_TOK // tm, D_MODEL // tn),
        in_specs=[pl.BlockSpec((tm, D_MODEL), lambda i, j: (i, 0)),
                  pl.BlockSpec((D_MODEL, tn), lambda i, j: (0, j)),
                  pl.BlockSpec((tm, tn), lambda i, j: (i, j))],
        out_specs=pl.BlockSpec((tm, tn), lambda i, j: (i, j)),
        out_shape=jax.ShapeDtypeStruct((N_TOK, D_MODEL), F32),
        compiler_params=_cparams(2, 48),
        name="out_proj",
    )(m, w, x)


ROUTER_TM = 512


def _router_body(h_ref, g_ref, wr_ref, br_ref, n2_ref, ei_ref, gt_ref, rk_ref, cnt_ref, carry_ref):
    tm = ROUTER_TM

    @pl.when(pl.program_id(0) == 0)
    def _():
        carry_ref[...] = jnp.zeros_like(carry_ref)

    h = h_ref[...]
    n2 = h * lax.rsqrt(jnp.mean(h * h, axis=-1, keepdims=True) + EPS) * g_ref[...]
    for cidx in range(ROW_TILE):
        n2_ref[pl.ds(cidx, tm, stride=ROW_TILE), :] = n2[:, cidx * LANES:(cidx + 1) * LANES]
    logits = jnp.dot(n2, wr_ref[...], precision=HI, preferred_element_type=F32) + br_ref[...]

    lane = lax.broadcasted_iota(I32, (tm, LANES), 1).astype(F32)
    work = logits
    vals, idxs = [], []
    for _ in range(TOP_K):
        m = jnp.max(work, axis=-1, keepdims=True)
        idx = jnp.min(jnp.where(work == m, lane, float(LANES)), axis=-1, keepdims=True)
        vals.append(m)
        idxs.append(idx)
        work = jnp.where(lane == idx, -jnp.inf, work)
    ex = [jnp.exp(v - vals[0]) for v in vals]
    den = ex[0] + ex[1] + ex[2] + ex[3]
    sel = jnp.zeros((tm, LANES), F32)
    for idx in idxs:
        sel = sel + jnp.where(lane == idx, 1.0, 0.0)
    before = (lax.broadcasted_iota(I32, (tm, tm), 0) > lax.broadcasted_iota(I32, (tm, tm), 1))
    rank_e = jnp.dot(before.astype(BF16), sel.astype(BF16), preferred_element_type=F32) + carry_ref[0:1, :]
    carry_ref[...] = carry_ref[...] + jnp.sum(sel, axis=0, keepdims=True)
    ei = jnp.zeros((tm, LANES), F32)
    gt = jnp.zeros((tm, LANES), F32)
    rk = jnp.zeros((tm, LANES), F32)
    for k in range(TOP_K):
        rank_k = jnp.sum(jnp.where(lane == idxs[k], rank_e, 0.0), axis=-1, keepdims=True)
        here = lane == float(k)
        ei = jnp.where(here, idxs[k], ei)
        gt = jnp.where(here, ex[k] / den, gt)
        rk = jnp.where(here, rank_k, rk)
    ei_ref[...] = ei.astype(I32)
    gt_ref[...] = gt
    rk_ref[...] = rk.astype(I32)
    cnt_ref[...] = carry_ref[...]


def _router(h, g, wr_pad, br_pad):
    tm = ROUTER_TM
    tok = pl.BlockSpec((tm, LANES), lambda i: (i, 0))
    return pl.pallas_call(
        _router_body,
        grid=(N_TOK // tm,),
        in_specs=[pl.BlockSpec((tm, D_MODEL), lambda i: (i, 0)),
                  pl.BlockSpec((1, D_MODEL), lambda i: (0, 0)),
                  pl.BlockSpec((D_MODEL, LANES), lambda i: (0, 0)),
                  pl.BlockSpec((1, LANES), lambda i: (0, 0))],
        out_specs=[pl.BlockSpec((tm * ROW_TILE, LANES), lambda i: (i, 0)), tok, tok, tok,
                   pl.BlockSpec((SUBLANES, LANES), lambda i: (0, 0))],
        out_shape=[jax.ShapeDtypeStruct((N_TOK * ROW_TILE, LANES), F32),
                   jax.ShapeDtypeStruct((N_TOK, LANES), I32),
                   jax.ShapeDtypeStruct((N_TOK, LANES), F32),
                   jax.ShapeDtypeStruct((N_TOK, LANES), I32),
                   jax.ShapeDtypeStruct((SUBLANES, LANES), F32)],
        scratch_shapes=[pltpu.VMEM((SUBLANES, LANES), F32)],
        compiler_params=_cparams(1, 48),
        name="router",
    )(h, g, wr_pad, br_pad)


DISPATCH_TOK = 256


def _row_slab(ref, row):
    start = row * ROW_TILE
    if not isinstance(start, int):
        start = pl.multiple_of(start, ROW_TILE)
    return ref.at[pl.ds(start, ROW_TILE)]


def _row_copy(src, src_row, dst, dst_row, sem):
    return pltpu.make_async_copy(_row_slab(src, src_row), _row_slab(dst, dst_row), sem)


def _dispatch_body(dest_ref, n2_hbm, zero_hbm, xs_hbm, sem):
    del zero_hbm
    base = pl.program_id(0) * DISPATCH_TOK

    def issue(i, carry):
        t = base + i
        for k in range(TOP_K):
            _row_copy(n2_hbm, t, xs_hbm, dest_ref[t * TOP_K + k], sem).start()
        return carry

    lax.fori_loop(0, DISPATCH_TOK, issue, 0)

    def drain(i, carry):
        for k in range(TOP_K):
            _row_copy(n2_hbm, 0, xs_hbm, 0, sem).wait()
        return carry

    lax.fori_loop(0, DISPATCH_TOK, drain, 0)


def _dispatch(dest_flat, n2_rows):
    zeros = jnp.zeros((MOE_ROWS * ROW_TILE, LANES), F32)
    return pl.pallas_call(
        _dispatch_body,
        grid_spec=pltpu.PrefetchScalarGridSpec(
            num_scalar_prefetch=1,
            grid=(N_TOK // DISPATCH_TOK,),
            in_specs=[pl.BlockSpec(memory_space=pl.ANY), pl.BlockSpec(memory_space=pl.ANY)],
            out_specs=pl.BlockSpec(memory_space=pl.ANY),
            scratch_shapes=[pltpu.SemaphoreType.DMA(())]),
        out_shape=jax.ShapeDtypeStruct((MOE_ROWS * ROW_TILE, LANES), F32),
        input_output_aliases={2: 0},
        compiler_params=_cparams(1, 16),
        name="moe_dispatch",
    )(dest_flat, n2_rows, zeros)


def _expert_body(e_ref, r_ref, n_ref, xs_hbm, wg_ref, wu_ref, bg_ref, bu_ref, wd_ref, bd_ref, yb_hbm,
                 stage_ref, xb_ref, act_ref, ynat_ref, sem_in, sem_out):
    del e_ref
    s = pl.program_id(0)
    j = pl.program_id(1)
    nsub = n_ref[s]
    row0 = r_ref[s]
    sub_rows = MOE_SUB * ROW_TILE

    def in_copy(k):
        return pltpu.make_async_copy(
            xs_hbm.at[pl.ds(pl.multiple_of((row0 + k * MOE_SUB) * ROW_TILE, sub_rows), sub_rows)],
            stage_ref.at[pl.ds(k * sub_rows, sub_rows)], sem_in)

    def out_copy(k):
        return pltpu.make_async_copy(
            stage_ref.at[pl.ds(k * sub_rows, sub_rows)],
            yb_hbm.at[pl.ds(pl.multiple_of((row0 + k * MOE_SUB) * ROW_TILE, sub_rows), sub_rows)], sem_out)

    @pl.when(j == 0)
    def _():
        for k in range(MOE_NSB):
            @pl.when(k < nsub)
            def _():
                in_copy(k).start()
        for k in range(MOE_NSB):
            @pl.when(k < nsub)
            def _():
                in_copy(k).wait()
        for k in range(MOE_NSB):
            @pl.when(k < nsub)
            def _():
                for cidx in range(ROW_TILE):
                    xb_ref[k * MOE_SUB:(k + 1) * MOE_SUB, cidx * LANES:(cidx + 1) * LANES] = (
                        stage_ref[pl.ds(k * sub_rows + cidx, MOE_SUB, stride=ROW_TILE), :].astype(BF16))

    @pl.when(j < MOE_J1)
    def _():
        wg = wg_ref[...].astype(BF16)
        wu = wu_ref[...].astype(BF16)
        for k in range(MOE_NSB):
            @pl.when(k < nsub)
            def _():
                x = xb_ref[k * MOE_SUB:(k + 1) * MOE_SUB, :]
                gate = jnp.dot(x, wg, preferred_element_type=F32) + bg_ref[...]
                up = jnp.dot(x, wu, preferred_element_type=F32) + bu_ref[...]
                gate = jnp.minimum(gate, SWIGLU_LIMIT)
                up = jnp.clip(up, -SWIGLU_LIMIT, SWIGLU_LIMIT)
                a = gate * _sigmoid(SWIGLU_ALPHA * gate) * (up + 1.0)
                act_ref[jnp.minimum(j, MOE_J1 - 1), k * MOE_SUB:(k + 1) * MOE_SUB, :] = a.astype(BF16)

    @pl.when(j >= MOE_J1)
    def _():
        wd = wd_ref[...].astype(BF16)
        jd = jnp.maximum(j - MOE_J1, 0)
        for k in range(MOE_NSB):
            @pl.when(k < nsub)
            def _():
                y = bd_ref[...] + jnp.dot(act_ref[0, k * MOE_SUB:(k + 1) * MOE_SUB, :], wd[0:MOE_TN, :],
                                          preferred_element_type=F32)
                for q in range(1, MOE_J1):
                    y = y + jnp.dot(act_ref[q, k * MOE_SUB:(k + 1) * MOE_SUB, :],
                                    wd[q * MOE_TN:(q + 1) * MOE_TN, :], preferred_element_type=F32)
                ynat_ref[jd, k * MOE_SUB:(k + 1) * MOE_SUB, :] = y

    @pl.when(j == MOE_J1 + MOE_J2 - 1)
    def _():
        per = MOE_TN // LANES
        for k in range(MOE_NSB):
            @pl.when(k < nsub)
            def _():
                for cidx in range(ROW_TILE):
                    stage_ref[pl.ds(k * sub_rows + cidx, MOE_SUB, stride=ROW_TILE), :] = (
                        ynat_ref[cidx // per, k * MOE_SUB:(k + 1) * MOE_SUB,
                                 (cidx % per) * LANES:(cidx % per + 1) * LANES])
                out_copy(k).start()
        for k in range(MOE_NSB):
            @pl.when(k < nsub)
            def _():
                out_copy(k).wait()


def _experts(sup_e, sup_row, sup_n, xs_rows, w_up, b_up, w_down, b_down):
    def up_j(s, j, e, r, n):
        return jnp.where(n[s] > 0, jnp.minimum(j, MOE_J1 - 1), MOE_J1 - 1)

    def dn_j(s, j, e, r, n):
        return jnp.where(n[s] > 0, jnp.maximum(j - MOE_J1, 0), MOE_J2 - 1)

    b_up3 = b_up.reshape(N_EXPERTS, 1, 2 * D_FF)
    b_down3 = b_down.reshape(N_EXPERTS, 1, D_MODEL)
    return pl.pallas_call(
        _expert_body,
        grid_spec=pltpu.PrefetchScalarGridSpec(
            num_scalar_prefetch=3,
            grid=(MOE_SMAX, MOE_J1 + MOE_J2),
            in_specs=[
                pl.BlockSpec(memory_space=pl.ANY),
                pl.BlockSpec((None, D_MODEL, MOE_TN), lambda s, j, e, r, n: (e[s], 0, up_j(s, j, e, r, n))),
                pl.BlockSpec((None, D_MODEL, MOE_TN),
                             lambda s, j, e, r, n: (e[s], 0, MOE_J1 + up_j(s, j, e, r, n))),
                pl.BlockSpec((None, 1, MOE_TN), lambda s, j, e, r, n: (e[s], 0, up_j(s, j, e, r, n))),
                pl.BlockSpec((None, 1, MOE_TN), lambda s, j, e, r, n: (e[s], 0, MOE_J1 + up_j(s, j, e, r, n))),
                pl.BlockSpec((None, D_FF, MOE_TN), lambda s, j, e, r, n: (e[s], 0, dn_j(s, j, e, r, n))),
                pl.BlockSpec((None, 1, MOE_TN), lambda s, j, e, r, n: (e[s], 0, dn_j(s, j, e, r, n))),
            ],
            out_specs=pl.BlockSpec(memory_space=pl.ANY),
            scratch_shapes=[pltpu.VMEM((MOE_TS * ROW_TILE, LANES), F32),
                            pltpu.VMEM((MOE_TS, D_MODEL), BF16),
                            pltpu.VMEM((MOE_J1, MOE_TS, MOE_TN), BF16),
                            pltpu.VMEM((MOE_J2, MOE_TS, MOE_TN), F32),
                            pltpu.SemaphoreType.DMA(()),
                            pltpu.SemaphoreType.DMA(())]),
        out_shape=jax.ShapeDtypeStruct((MOE_ROWS * ROW_TILE, LANES), F32),
        input_output_aliases={3: 0},
        compiler_params=_cparams(2, 52),
        name="moe_experts",
    )(sup_e, sup_row, sup_n, xs_rows, w_up, w_up, b_up3, b_up3, w_down, b_down3)


COMB_TM = 256
COMB_TN = 512
COMB_J = D_MODEL // COMB_TN


def _combine_body(dest_ref, yb_hbm, h_ref, gt_ref, gp_ref, wg_ref, p_ref, wu_ref, gf_ref, o_ref,
                  gath_ref, h2_ref, h3_ref, n3_ref, sem):
    tm = COMB_TM
    i = pl.program_id(0)
    j = pl.program_id(1)
    per = COMB_TN // LANES

    @pl.when(j == 0)
    def _():
        base = i * tm

        def issue(r, carry):
            for k in range(TOP_K):
                _row_copy(yb_hbm, dest_ref[(base + r) * TOP_K + k], gath_ref, k * tm + r, sem).start()
            return carry

        lax.fori_loop(0, tm, issue, 0)

        def drain(r, carry):
            for k in range(TOP_K):
                _row_copy(yb_hbm, 0, gath_ref, 0, sem).wait()
            return carry

        lax.fori_loop(0, tm, drain, 0)

        gt = gt_ref[...]
        ssq = jnp.zeros((tm, 1), F32)
        for cidx in range(ROW_TILE):
            acc = h_ref[:, cidx * LANES:(cidx + 1) * LANES]
            for k in range(TOP_K):
                acc = acc + gt[:, k:k + 1] * gath_ref[pl.ds(k * tm * ROW_TILE + cidx, tm, stride=ROW_TILE), :]
            h2_ref[cidx // per, :, (cidx % per) * LANES:(cidx % per + 1) * LANES] = acc
            ssq = ssq + jnp.sum(acc * acc, axis=-1, keepdims=True)
        scale = lax.rsqrt(ssq / D_MODEL + EPS)
        for q in range(COMB_J):
            n3_ref[:, q * COMB_TN:(q + 1) * COMB_TN] = (
                h2_ref[q] * scale * gp_ref[:, q * COMB_TN:(q + 1) * COMB_TN]).astype(BF16)

    gate = _sigmoid(jnp.dot(n3_ref[...], wg_ref[...].astype(BF16), preferred_element_type=F32))
    up = jnp.dot(p_ref[...].astype(BF16), wu_ref[...].astype(BF16), preferred_element_type=F32)
    h3_ref[j] = h2_ref[j] + gate * up

    @pl.when(j == COMB_J - 1)
    def _():
        ssq = jnp.zeros((tm, 1), F32)
        for q in range(COMB_J):
            v = h3_ref[q]
            ssq = ssq + jnp.sum(v * v, axis=-1, keepdims=True)
        scale = lax.rsqrt(ssq / D_MODEL + EPS)
        for q in range(COMB_J):
            o_ref[:, q * COMB_TN:(q + 1) * COMB_TN] = (
                h3_ref[q] * scale * gf_ref[:, q * COMB_TN:(q + 1) * COMB_TN])


def _combine(dest_flat, yb_rows, h, gates, g_ple, w_ple_gate, p_all, w_ple_up, g_final):
    tm = COMB_TM
    return pl.pallas_call(
        _combine_body,
        grid_spec=pltpu.PrefetchScalarGridSpec(
            num_scalar_prefetch=1,
            grid=(N_TOK // tm, COMB_J),
            in_specs=[
                pl.BlockSpec(memory_space=pl.ANY),
                pl.BlockSpec((tm, D_MODEL), lambda i, j, d: (i, 0)),
                pl.BlockSpec((tm, LANES), lambda i, j, d: (i, 0)),
                pl.BlockSpec((1, D_MODEL), lambda i, j, d: (0, 0)),
                pl.BlockSpec((D_MODEL, COMB_TN), lambda i, j, d: (0, j)),
                pl.BlockSpec((tm, D_PLE), lambda i, j, d: (i, 0)),
                pl.BlockSpec((D_PLE, COMB_TN), lambda i, j, d: (0, j)),
                pl.BlockSpec((1, D_MODEL), lambda i, j, d: (0, 0)),
            ],
            out_specs=pl.BlockSpec((tm, D_MODEL), lambda i, j, d: (i, 0)),
            scratch_shapes=[pltpu.VMEM((TOP_K * tm * ROW_TILE, LANES), F32),
                            pltpu.VMEM((COMB_J, tm, COMB_TN), F32),
                            pltpu.VMEM((COMB_J, tm, COMB_TN), F32),
                            pltpu.VMEM((tm, D_MODEL), BF16),
                            pltpu.SemaphoreType.DMA(())]),
        out_shape=jax.ShapeDtypeStruct((N_TOK, D_MODEL), F32),
        compiler_params=_cparams(2, 48),
        name="combine_ple_norm",
    )(dest_flat, yb_rows, h, gates, g_ple, w_ple_gate, p_all, w_ple_up, g_final)


def _routing_tables(counts, eidx, rank):
    nsub_e = (counts + MOE_SUB - 1) // MOE_SUB
    prow_e = (jnp.cumsum(nsub_e) - nsub_e) * MOE_SUB
    dest = (prow_e[eidx] + rank).reshape(-1).astype(I32)
    nsup_e = (nsub_e + MOE_NSB - 1) // MOE_NSB
    csup = jnp.cumsum(nsup_e)
    total = csup[-1]
    s = jnp.arange(MOE_SMAX, dtype=I32)
    e_raw = jnp.minimum(jnp.searchsorted(csup, s, side="right").astype(I32), N_EXPERTS - 1)
    valid = s < total
    e_last = e_raw[jnp.maximum(total - 1, 0)]
    e_s = jnp.where(valid, e_raw, e_last)
    local = s - (csup[e_s] - nsup_e[e_s])
    row_s = jnp.where(valid, prow_e[e_s] + local * MOE_TS, 0)
    n_s = jnp.where(valid, jnp.clip(nsub_e[e_s] - local * MOE_NSB, 0, MOE_NSB), 0)
    return dest, e_s.astype(I32), row_s.astype(I32), n_s.astype(I32)


def _pad_prev(state, width):
    b, _, c = state.shape
    return jnp.pad(state, ((0, 0), (SUBLANES - (width - 1), 0), (0, 0))).reshape(b * SUBLANES, c)


def kernel(x_prompt, x_sample, p_prompt, p_sample, state_conv_a, state_conv_ssm, state_ssm, g_mix, w_in, conv_a_w, ssm_conv_w, ssm_conv_b, dt_bias, a_log, d_skip, ssm_norm_g, w_branch_a, w_branch_b, w_out, g_ffn, w_router, b_router, w_up, b_up, w_down, b_down, g_ple, w_ple_gate, w_ple_up, g_final):
    x_all = jnp.concatenate([x_prompt.reshape(NP_TOK, D_MODEL), x_sample.reshape(NS_TOK, D_MODEL)], axis=0)
    p_all = jnp.concatenate([p_prompt[0].reshape(NP_TOK, D_PLE), p_sample[0].reshape(NS_TOK, D_PLE)], axis=0)
    w_in0 = w_in[0]
    g_mix2 = g_mix[0].reshape(1, D_MODEL)

    proj = _rms_matmul(x_all, g_mix2, w_in0, 0, N_MAIN, 1024, 512, "in_proj")
    dt_raw = _rms_matmul(x_all, g_mix2, w_in0[:, COL_DT:COL_GATE], 0, SSM_HEADS, 1024, SSM_HEADS, "in_proj_dt")
    gates = _rms_matmul(x_all, g_mix2, w_in0[:, COL_GATE:], 0, 2 * D_MODEL, 1024, 512, "in_proj_gates")

    ya_p, ua_tail = _conv_a_prompt(proj, conv_a_w[0])
    ya_s, ua_s = _conv_a_sample(proj, conv_a_w[0], _pad_prev(state_conv_a[0], CONV_A_W))
    conv_a_prompt = ua_tail[:, SUBLANES - (CONV_A_W - 1):, :]
    conv_a_sample = ua_s.reshape(DEC_BATCH, DEC_SEQ, D_CONV)[:, DEC_SEQ - (CONV_A_W - 1):, :]

    conv_b2 = ssm_conv_b[0].reshape(1, D_XBC)
    xbc_p = _conv_m_prompt(proj, ssm_conv_w[0], conv_b2)
    xbc_s = _conv_m_sample(proj, ssm_conv_w[0], conv_b2, _pad_prev(state_conv_ssm[0], SSM_CONV_W))
    xbc_raw_p = proj[:NP_TOK, COL_XBC:COL_XBC + D_XBC].reshape(BATCH, SEQ, D_XBC)
    xbc_raw_s = proj[NP_TOK:, COL_XBC:COL_XBC + D_XBC].reshape(DEC_BATCH, DEC_SEQ, D_XBC)
    conv_m_prompt = xbc_raw_p[:, SEQ - (SSM_CONV_W - 1):, :]
    conv_m_sample = xbc_raw_s[:, DEC_SEQ - (SSM_CONV_W - 1):, :]

    ssd_w = (dt_bias[0].reshape(1, SSM_HEADS), dt_bias[0].reshape(SSM_HEADS, 1),
             a_log[0].reshape(1, SSM_HEADS), a_log[0].reshape(SSM_HEADS, 1),
             jnp.repeat(d_skip[0], SSM_HEAD_DIM).reshape(1, D_INNER), ssm_norm_g[0].reshape(1, D_INNER))
    ym_p, ssm_p = _ssd_call(SSM_CHUNK, xbc_p, proj, dt_raw, *ssd_w, None,
                            n_seq=BATCH, n_chunk=SEQ // SSM_CHUNK, row0=0, name="ssd_prompt")
    h0 = state_ssm[0].reshape(DEC_BATCH, SSM_HEADS * SSM_HEAD_DIM, SSM_STATE)
    ym_s, ssm_s = _ssd_call(DEC_SEQ, xbc_s, proj, dt_raw, *ssd_w, h0,
                            n_seq=DEC_BATCH, n_chunk=1, row0=NP_TOK, name="ssd_sample")
    ssm_shape = (SSM_HEADS, SSM_HEAD_DIM, SSM_STATE)

    ya = jnp.concatenate([ya_p, ya_s], axis=0)
    ym = jnp.concatenate([ym_p, ym_s.astype(BF16)], axis=0)
    merged = _merge(ya, ym, w_branch_a[0], w_branch_b[0], gates)
    h = _resid_matmul(merged, w_out[0], x_all)

    wr_pad = jnp.pad(w_router[0], ((0, 0), (0, LANES - N_EXPERTS)))
    br_pad = jnp.pad(b_router[0].reshape(1, N_EXPERTS), ((0, 0), (0, LANES - N_EXPERTS)), constant_values=-1e30)
    n2_rows, eidx, gate_w, rank, cnt = _router(h, g_ffn[0].reshape(1, D_MODEL), wr_pad, br_pad)
    counts = cnt[0, :N_EXPERTS].astype(I32)
    dest, sup_e, sup_row, sup_n = _routing_tables(counts, eidx[:, :TOP_K], rank[:, :TOP_K])
    xs_rows = _dispatch(dest, n2_rows)
    yb_rows = _experts(sup_e, sup_row, sup_n, xs_rows, w_up[0], b_up[0], w_down[0], b_down[0])

    y_all = _combine(dest, yb_rows, h, gate_w, g_ple[0].reshape(1, D_MODEL), w_ple_gate[0], p_all,
                     w_ple_up[0], g_final.reshape(1, D_MODEL))
    y_prompt = y_all[:NP_TOK].reshape(BATCH, SEQ, D_MODEL)
    y_sample = y_all[NP_TOK:].reshape(DEC_BATCH, DEC_SEQ, D_MODEL)
    return (y_prompt, y_sample,
            conv_a_prompt[None], conv_m_prompt[None], ssm_p.reshape((1, BATCH) + ssm_shape),
            conv_a_sample[None], conv_m_sample[None], ssm_s.reshape((1, DEC_BATCH) + ssm_shape))
```

```python
import functools

import jax
import jax.numpy as jnp
from jax import lax
from jax.experimental import pallas as pl
from jax.experimental.pallas import tpu as pltpu

F32 = jnp.float32
BF16 = jnp.bfloat16
I32 = jnp.int32
HI = lax.Precision.HIGHEST

D_MODEL = 2048
BATCH = 4
SEQ = 2048
DEC_BATCH = 128
DEC_SEQ = 8
D_PLE = 256
D_CONV = D_MODEL
CONV_A_W = 3
D_INNER = 2 * D_MODEL
SSM_HEAD_DIM = 64
SSM_HEADS = D_INNER // SSM_HEAD_DIM
SSM_GROUPS = 8
SSM_STATE = 128
SSM_CONV_W = 4
SSM_CHUNK = 128
D_XBC = D_INNER + 2 * SSM_GROUPS * SSM_STATE
N_EXPERTS = 32
TOP_K = 4
D_FF = D_MODEL
SWIGLU_LIMIT = 7.0
SWIGLU_ALPHA = 1.702
EPS = 1e-6

NP_TOK = BATCH * SEQ
NS_TOK = DEC_BATCH * DEC_SEQ
N_TOK = NP_TOK + NS_TOK
N_ASSIGN = N_TOK * TOP_K

LANES = 128
SUBLANES = 8
ROW_TILE = D_MODEL // LANES

COL_AX = 0
COL_AB = D_CONV
COL_AC = 2 * D_CONV
COL_Z = 3 * D_CONV
COL_XBC = COL_Z + D_INNER
COL_DT = COL_XBC + D_XBC
COL_GATE = COL_DT + SSM_HEADS
N_MAIN = COL_DT

MOE_SUB = 256
MOE_TS = 1024
MOE_NSB = MOE_TS // MOE_SUB
MOE_ROWS = N_ASSIGN + N_EXPERTS * MOE_SUB
MOE_SMAX = N_ASSIGN // MOE_TS + N_EXPERTS
MOE_TN = 512
MOE_J1 = D_FF // MOE_TN
MOE_J2 = D_MODEL // MOE_TN

MIB = 1024 * 1024


def _cparams(n_grid, vmem_mib):
    return pltpu.CompilerParams(dimension_semantics=("arbitrary",) * n_grid,
                                vmem_limit_bytes=vmem_mib * MIB)


def _sigmoid(x):
    return 1.0 / (1.0 + jnp.exp(-x))


def _softplus(x):
    return jnp.maximum(x, 0.0) + jnp.log1p(jnp.exp(-jnp.abs(x)))


def _rms_mm_body(x_ref, g_ref, w_ref, o_ref, n_ref):
    @pl.when(pl.program_id(1) == 0)
    def _():
        x = x_ref[...]
        ms = jnp.mean(x * x, axis=-1, keepdims=True)
        n_ref[...] = (x * lax.rsqrt(ms + EPS) * g_ref[...]).astype(BF16)

    o_ref[...] = jnp.dot(n_ref[...], w_ref[...].astype(BF16), preferred_element_type=F32)


def _rms_matmul(x, g, w, col0_blocks, ncols, tm, tn, name):
    n, k = x.shape
    return pl.pallas_call(
        _rms_mm_body,
        grid=(n // tm, ncols // tn),
        in_specs=[pl.BlockSpec((tm, k), lambda i, j: (i, 0)),
                  pl.BlockSpec((1, k), lambda i, j: (0, 0)),
                  pl.BlockSpec((k, tn), lambda i, j: (0, col0_blocks + j))],
        out_specs=pl.BlockSpec((tm, tn), lambda i, j: (i, j)),
        out_shape=jax.ShapeDtypeStruct((n, ncols), F32),
        scratch_shapes=[pltpu.VMEM((tm, k), BF16)],
        compiler_params=_cparams(2, 48),
        name=name,
    )(x, g, w)


def _causal_taps(u, w_ref, width, rowmod, prev):
    rows = u.shape[0]
    acc = u * w_ref[width - 1:width, :]
    for s in range(1, width):
        shifted = pltpu.roll(u, s, 0)
        if prev is None:
            fill = jnp.zeros_like(u)
        else:
            fill = pltpu.roll(prev, rows - SUBLANES + s, 0)
        acc = acc + jnp.where(rowmod >= s, shifted, fill) * w_ref[width - 1 - s:width - s, :]
    return acc


def _row_in_seq(shape, period):
    r = lax.broadcasted_iota(I32, shape, 0)
    return r if period >= shape[0] else r & (period - 1)


def _conv_a_body(period, has_prev, *refs):
    if has_prev:
        ax_ref, ab_ref, ac_ref, w_ref, prev_ref, ya_ref, u_ref = refs
        prev = prev_ref[...]
    else:
        ax_ref, ab_ref, ac_ref, w_ref, ya_ref, u_ref = refs
        prev = None
    u = ac_ref[...] * ax_ref[...]
    v = _causal_taps(u, w_ref, CONV_A_W, _row_in_seq(u.shape, period), prev)
    ya_ref[...] = (ab_ref[...] * v).astype(BF16)
    if has_prev:
        u_ref[...] = u
    else:
        u_ref[...] = u[u.shape[0] - SUBLANES:, :]


def _conv_m_body(period, has_prev, *refs):
    if has_prev:
        x_ref, w_ref, b_ref, prev_ref, o_ref = refs
        prev = prev_ref[...]
    else:
        x_ref, w_ref, b_ref, o_ref = refs
        prev = None
    u = x_ref[...]
    v = _causal_taps(u, w_ref, SSM_CONV_W, _row_in_seq(u.shape, period), prev) + b_ref[...]
    o_ref[...] = v * _sigmoid(v)


CONV_TC = 256


def _conv_a_prompt(proj, conv_w):
    nb = D_CONV // CONV_TC
    blk = lambda off: pl.BlockSpec((SEQ, CONV_TC), lambda b, c, off=off: (b, off // CONV_TC + c))
    return pl.pallas_call(
        functools.partial(_conv_a_body, SEQ, False),
        grid=(BATCH, nb),
        in_specs=[blk(COL_AX), blk(COL_AB), blk(COL_AC),
                  pl.BlockSpec((CONV_A_W, CONV_TC), lambda b, c: (0, c))],
        out_specs=[pl.BlockSpec((SEQ, CONV_TC), lambda b, c: (b, c)),
                   pl.BlockSpec((None, SUBLANES, CONV_TC), lambda b, c: (b, 0, c))],
        out_shape=[jax.ShapeDtypeStruct((NP_TOK, D_CONV), BF16),
                   jax.ShapeDtypeStruct((BATCH, SUBLANES, D_CONV), F32)],
        compiler_params=_cparams(2, 48),
        name="conv_a_prompt",
    )(proj, proj, proj, conv_w)


def _conv_a_sample(proj, conv_w, prev_pad):
    nb = D_CONV // CONV_TC
    rb = NP_TOK // NS_TOK
    blk = lambda off: pl.BlockSpec((NS_TOK, CONV_TC), lambda c, off=off: (rb, off // CONV_TC + c))
    own = pl.BlockSpec((NS_TOK, CONV_TC), lambda c: (0, c))
    return pl.pallas_call(
        functools.partial(_conv_a_body, DEC_SEQ, True),
        grid=(nb,),
        in_specs=[blk(COL_AX), blk(COL_AB), blk(COL_AC),
                  pl.BlockSpec((CONV_A_W, CONV_TC), lambda c: (0, c)), own],
        out_specs=[own, own],
        out_shape=[jax.ShapeDtypeStruct((NS_TOK, D_CONV), BF16),
                   jax.ShapeDtypeStruct((NS_TOK, D_CONV), F32)],
        compiler_params=_cparams(1, 48),
        name="conv_a_sample",
    )(proj, proj, proj, conv_w, prev_pad)


def _conv_m_prompt(proj, conv_w, conv_b):
    nb = D_XBC // CONV_TC
    return pl.pallas_call(
        functools.partial(_conv_m_body, SEQ, False),
        grid=(BATCH, nb),
        in_specs=[pl.BlockSpec((SEQ, CONV_TC), lambda b, c: (b, COL_XBC // CONV_TC + c)),
                  pl.BlockSpec((SSM_CONV_W, CONV_TC), lambda b, c: (0, c)),
                  pl.BlockSpec((1, CONV_TC), lambda b, c: (0, c))],
        out_specs=pl.BlockSpec((SEQ, CONV_TC), lambda b, c: (b, c)),
        out_shape=jax.ShapeDtypeStruct((NP_TOK, D_XBC), F32),
        compiler_params=_cparams(2, 48),
        name="conv_m_prompt",
    )(proj, conv_w, conv_b)


def _conv_m_sample(proj, conv_w, conv_b, prev_pad):
    nb = D_XBC // CONV_TC
    rb = NP_TOK // NS_TOK
    own = pl.BlockSpec((NS_TOK, CONV_TC), lambda c: (0, c))
    return pl.pallas_call(
        functools.partial(_conv_m_body, DEC_SEQ, True),
        grid=(nb,),
        in_specs=[pl.BlockSpec((NS_TOK, CONV_TC), lambda c: (rb, COL_XBC // CONV_TC + c)),
                  pl.BlockSpec((SSM_CONV_W, CONV_TC), lambda c: (0, c)),
                  pl.BlockSpec((1, CONV_TC), lambda c: (0, c)), own],
        out_specs=own,
        out_shape=jax.ShapeDtypeStruct((NS_TOK, D_XBC), F32),
        compiler_params=_cparams(1, 48),
        name="conv_m_sample",
    )(proj, conv_w, conv_b, prev_pad)


def _ssd_body(valid, has_h0, *refs):
    if has_h0:
        (xbc_in, zlo_ref, zhi_ref, dtr_in, dtb_ref, dtbc_ref, alog_ref, alogc_ref, dsk_ref, ng_ref,
         h0_ref, ym_ref, hl_ref, ht_ref, y_ref, xp_ref, dp_ref) = refs
    else:
        (xbc_in, zlo_ref, zhi_ref, dtr_in, dtb_ref, dtbc_ref, alog_ref, alogc_ref, dsk_ref, ng_ref,
         ym_ref, hl_ref, ht_ref, y_ref) = refs
    L = SSM_CHUNK
    H = SSM_HEADS
    c = pl.program_id(1)

    @pl.when(c == 0)
    def _():
        if has_h0:
            ht_ref[...] = h0_ref[...].T
        else:
            ht_ref[...] = jnp.zeros_like(ht_ref)

    if valid < L:
        @pl.when(jnp.logical_and(pl.program_id(0) == 0, c == 0))
        def _():
            xp_ref[...] = jnp.zeros_like(xp_ref)
            dp_ref[...] = jnp.zeros_like(dp_ref)

        xp_ref[0:valid, :] = xbc_in[...]
        dp_ref[0:valid, :] = dtr_in[...]
        xbc_ref, dtr = xp_ref, dp_ref[...]
    else:
        xbc_ref, dtr = xbc_in, dtr_in[...]

    dt = _softplus(dtr + dtb_ref[...])
    eye = (lax.broadcasted_iota(I32, (H, H), 0) == lax.broadcasted_iota(I32, (H, H), 1)).astype(F32)
    dtr_t = lax.dot_general(eye, dtr, (((1,), (1,)), ((), ())), precision=HI,
                            preferred_element_type=F32)
    dt_t = _softplus(dtr_t + dtbc_ref[...])
    if valid < L:
        dt = jnp.where(lax.broadcasted_iota(I32, (L, H), 0) < valid, dt, 0.0)
        dt_t = jnp.where(lax.broadcasted_iota(I32, (H, L), 1) < valid, dt_t, 0.0)
    adt = dt * (-jnp.exp(alog_ref[...]))
    adt_t = dt_t * (-jnp.exp(alogc_ref[...]))
    li = lax.broadcasted_iota(I32, (L, L), 0)
    si = lax.broadcasted_iota(I32, (L, L), 1)
    causal = li >= si
    a_cum = jnp.dot(causal.astype(F32), adt, precision=HI, preferred_element_type=F32)
    a_cum_t = jnp.dot(adt_t, (li <= si).astype(F32), precision=HI, preferred_element_type=F32)
    ea = jnp.exp(a_cum)
    a_tot_t = a_cum_t[:, L - 1:L]
    dte_t = jnp.exp(a_tot_t - a_cum_t)
    ea_tot_t = jnp.exp(a_tot_t)

    lane = lax.broadcasted_iota(I32, (L, LANES), 1)
    lo = lane < SSM_HEAD_DIM
    lane1 = lax.broadcasted_iota(I32, (1, LANES), 1) < SSM_HEAD_DIM
    gn = SSM_GROUPS * SSM_STATE
    for g in range(SSM_GROUPS):
        bg = xbc_ref[:, D_INNER + g * SSM_STATE:D_INNER + (g + 1) * SSM_STATE]
        cg = xbc_ref[:, D_INNER + gn + g * SSM_STATE:D_INNER + gn + (g + 1) * SSM_STATE]
        bg_t = bg.T
        cb = jnp.dot(cg.astype(BF16), bg_t.astype(BF16), preferred_element_type=F32)
        for jj in range(SSM_HEADS // SSM_GROUPS // 2):
            j = g * (SSM_HEADS // SSM_GROUPS // 2) + jj
            cols = slice(j * LANES, (j + 1) * LANES)
            lhs1, lhs2 = [], []
            for h in (2 * j, 2 * j + 1):
                diff = a_cum[:, h:h + 1] - a_cum_t[h:h + 1, :]
                decay = jnp.exp(jnp.where(causal, diff, -1e30))
                lhs1.append((cb * decay).astype(BF16))
            for h in (2 * j, 2 * j + 1):
                lhs1.append((cg * ea[:, h:h + 1]).astype(BF16))
                lhs2.append((bg_t * dte_t[h:h + 1, :]).astype(BF16))
            dt_pair = jnp.where(lo, dt[:, 2 * j:2 * j + 1], dt[:, 2 * j + 1:2 * j + 2])
            xdt = xbc_ref[:, cols] * dt_pair
            zero = jnp.zeros_like(xdt)
            x_top = jnp.where(lo, xdt, zero).astype(BF16)
            x_bot = jnp.where(lo, zero, xdt).astype(BF16)
            ht = ht_ref[:, cols]
            h_top = jnp.where(lo, ht, zero).astype(BF16)
            h_bot = jnp.where(lo, zero, ht).astype(BF16)
            y_ref[:, cols] = jnp.dot(jnp.concatenate(lhs1, axis=1),
                                     jnp.concatenate([x_top, x_bot, h_top, h_bot], axis=0),
                                     preferred_element_type=F32)
            upd = jnp.dot(jnp.concatenate(lhs2, axis=1), jnp.concatenate([x_top, x_bot], axis=0),
                          preferred_element_type=F32)
            keep = jnp.where(lane1, ea_tot_t[2 * j:2 * j + 1, :], ea_tot_t[2 * j + 1:2 * j + 2, :])
            ht_ref[:, cols] = ht * keep + upd

    gw = D_INNER // SSM_GROUPS
    for g in range(SSM_GROUPS):
        cols = slice(g * gw, (g + 1) * gw)
        zr = zlo_ref if g < SSM_GROUPS // 2 else zhi_ref
        zc = slice((g % (SSM_GROUPS // 2)) * gw, (g % (SSM_GROUPS // 2) + 1) * gw)
        z = zr[:, zc]
        y = y_ref[0:valid, cols] + dsk_ref[:, cols] * xbc_ref[0:valid, cols]
        y = y * (z * _sigmoid(z))
        y = y * lax.rsqrt(jnp.mean(y * y, axis=-1, keepdims=True) + EPS)
        ym_ref[:, cols] = (y * ng_ref[:, cols]).astype(ym_ref.dtype)

    @pl.when(c == pl.num_programs(1) - 1)
    def _():
        hl_ref[...] = ht_ref[...].T


def _ssd_call(valid, xbc, proj, dt_raw, dtb, dtbc, alog, alogc, dsk, ng, h0, *, n_seq, n_chunk, row0, name):
    has_h0 = h0 is not None
    hp = SSM_HEADS * SSM_HEAD_DIM
    rb0 = row0 // valid
    zb = D_INNER // 2
    in_specs = [pl.BlockSpec((valid, D_XBC), lambda b, c: (b * n_chunk + c, 0)),
                pl.BlockSpec((valid, zb), lambda b, c: (rb0 + b * n_chunk + c, COL_Z // zb)),
                pl.BlockSpec((valid, zb), lambda b, c: (rb0 + b * n_chunk + c, COL_Z // zb + 1)),
                pl.BlockSpec((valid, SSM_HEADS), lambda b, c: (rb0 + b * n_chunk + c, 0)),
                pl.BlockSpec((1, SSM_HEADS), lambda b, c: (0, 0)),
                pl.BlockSpec((SSM_HEADS, 1), lambda b, c: (0, 0)),
                pl.BlockSpec((1, SSM_HEADS), lambda b, c: (0, 0)),
                pl.BlockSpec((SSM_HEADS, 1), lambda b, c: (0, 0)),
                pl.BlockSpec((1, D_INNER), lambda b, c: (0, 0)),
                pl.BlockSpec((1, D_INNER), lambda b, c: (0, 0))]
    args = [xbc, proj, proj, dt_raw, dtb, dtbc, alog, alogc, dsk, ng]
    scratch = [pltpu.VMEM((SSM_STATE, hp), F32), pltpu.VMEM((SSM_CHUNK, hp), F32)]
    if has_h0:
        in_specs.append(pl.BlockSpec((None, hp, SSM_STATE), lambda b, c: (b, 0, 0)))
        args.append(h0)
        scratch += [pltpu.VMEM((SSM_CHUNK, D_XBC), F32), pltpu.VMEM((SSM_CHUNK, SSM_HEADS), F32)]
    ym_dtype = BF16 if valid % 16 == 0 else F32
    return pl.pallas_call(
        functools.partial(_ssd_body, valid, has_h0),
        grid=(n_seq, n_chunk),
        in_specs=in_specs,
        out_specs=[pl.BlockSpec((valid, D_INNER), lambda b, c: (b * n_chunk + c, 0)),
                   pl.BlockSpec((None, hp, SSM_STATE), lambda b, c: (b, 0, 0))],
        out_shape=[jax.ShapeDtypeStruct((n_seq * n_chunk * valid, D_INNER), ym_dtype),
                   jax.ShapeDtypeStruct((n_seq, hp, SSM_STATE), F32)],
        scratch_shapes=scratch,
        compiler_params=_cparams(2, 48),
        name=name,
    )(*args)


def _merge_body(ya_ref, ym_ref, wa_ref, wb_ref, ga_ref, gb_ref, o_ref):
    a = jnp.dot(ya_ref[...], wa_ref[...].astype(BF16), preferred_element_type=F32)
    b = jnp.dot(ym_ref[...], wb_ref[...].astype(BF16), preferred_element_type=F32)
    o_ref[...] = (_sigmoid(ga_ref[...]) * a + _sigmoid(gb_ref[...]) * b).astype(BF16)


def _merge(ya, ym, wa, wb, gates):
    tm, tn = 1024, 256
    return pl.pallas_call(
        _merge_body,
        grid=(N_TOK // tm, D_MODEL // tn),
        in_specs=[pl.BlockSpec((tm, D_CONV), lambda i, j: (i, 0)),
                  pl.BlockSpec((tm, D_INNER), lambda i, j: (i, 0)),
                  pl.BlockSpec((D_CONV, tn), lambda i, j: (0, j)),
                  pl.BlockSpec((D_INNER, tn), lambda i, j: (0, j)),
                  pl.BlockSpec((tm, tn), lambda i, j: (i, j)),
                  pl.BlockSpec((tm, tn), lambda i, j: (i, D_MODEL // tn + j))],
        out_specs=pl.BlockSpec((tm, tn), lambda i, j: (i, j)),
        out_shape=jax.ShapeDtypeStruct((N_TOK, D_MODEL), BF16),
        compiler_params=_cparams(2, 52),
        name="merge",
    )(ya, ym, wa, wb, gates, gates)


def _resid_body(m_ref, w_ref, x_ref, o_ref):
    o_ref[...] = x_ref[...] + jnp.dot(m_ref[...], w_ref[...].astype(BF16), preferred_element_type=F32)


def _resid_matmul(m, w, x):
    tm, tn = 1024, 512
    return pl.pallas_call(
        _resid_body,
        grid=(N_TOK // tm, D_MODEL // tn),
        in_specs=[pl.BlockSpec((tm, D_MODEL), lambda i, j: (i, 0)),
                  pl.BlockSpec((D_MODEL, tn), lambda i, j: (0, j)),
                  pl.BlockSpec((tm, tn), lambda i, j: (i, j))],
        out_specs=pl.BlockSpec((tm, tn), lambda i, j: (i, j)),
        out_shape=jax.ShapeDtypeStruct((N_TOK, D_MODEL), F32),
        compiler_params=_cparams(2, 48),
        name="out_proj",
    )(m, w, x)


ROUTER_TM = 512


def _router_body(h_ref, g_ref, wr_ref, br_ref, ei_ref, gt_ref, rk_ref, cnt_ref, carry_ref):
    tm = ROUTER_TM

    @pl.when(pl.program_id(0) == 0)
    def _():
        carry_ref[...] = jnp.zeros_like(carry_ref)

    h = h_ref[...]
    n2 = h * lax.rsqrt(jnp.mean(h * h, axis=-1, keepdims=True) + EPS) * g_ref[...]
    logits = jnp.dot(n2, wr_ref[...], precision=HI, preferred_element_type=F32) + br_ref[...]

    lane = lax.broadcasted_iota(I32, (tm, LANES), 1).astype(F32)
    work = logits
    vals, idxs = [], []
    for _ in range(TOP_K):
        m = jnp.max(work, axis=-1, keepdims=True)
        idx = jnp.min(jnp.where(work == m, lane, float(LANES)), axis=-1, keepdims=True)
        vals.append(m)
        idxs.append(idx)
        work = jnp.where(lane == idx, -jnp.inf, work)
    ex = [jnp.exp(v - vals[0]) for v in vals]
    den = ex[0] + ex[1] + ex[2] + ex[3]
    sel = jnp.zeros((tm, LANES), F32)
    for idx in idxs:
        sel = sel + jnp.where(lane == idx, 1.0, 0.0)
    before = (lax.broadcasted_iota(I32, (tm, tm), 0) > lax.broadcasted_iota(I32, (tm, tm), 1))
    rank_e = jnp.dot(before.astype(BF16), sel.astype(BF16), preferred_element_type=F32) + carry_ref[0:1, :]
    carry_ref[...] = carry_ref[...] + jnp.sum(sel, axis=0, keepdims=True)
    ei = jnp.zeros((tm, LANES), F32)
    gt = jnp.zeros((tm, LANES), F32)
    rk = jnp.zeros((tm, LANES), F32)
    for k in range(TOP_K):
        rank_k = jnp.sum(jnp.where(lane == idxs[k], rank_e, 0.0), axis=-1, keepdims=True)
        here = lane == float(k)
        ei = jnp.where(here, idxs[k], ei)
        gt = jnp.where(here, ex[k] / den, gt)
        rk = jnp.where(here, rank_k, rk)
    ei_ref[...] = ei.astype(I32)
    gt_ref[...] = gt
    rk_ref[...] = rk.astype(I32)
    cnt_ref[...] = carry_ref[...]


def _router(h, g, wr_pad, br_pad):
    tm = ROUTER_TM
    tok = pl.BlockSpec((tm, LANES), lambda i: (i, 0))
    return pl.pallas_call(
        _router_body,
        grid=(N_TOK // tm,),
        in_specs=[pl.BlockSpec((tm, D_MODEL), lambda i: (i, 0)),
                  pl.BlockSpec((1, D_MODEL), lambda i: (0, 0)),
                  pl.BlockSpec((D_MODEL, LANES), lambda i: (0, 0)),
                  pl.BlockSpec((1, LANES), lambda i: (0, 0))],
        out_specs=[tok, tok, tok, pl.BlockSpec((SUBLANES, LANES), lambda i: (0, 0))],
        out_shape=[jax.ShapeDtypeStruct((N_TOK, LANES), I32),
                   jax.ShapeDtypeStruct((N_TOK, LANES), F32),
                   jax.ShapeDtypeStruct((N_TOK, LANES), I32),
                   jax.ShapeDtypeStruct((SUBLANES, LANES), F32)],
        scratch_shapes=[pltpu.VMEM((SUBLANES, LANES), F32)],
        compiler_params=_cparams(1, 48),
        name="router",
    )(h, g, wr_pad, br_pad)


DISPATCH_TOK = 512


def _row_slab(ref, row):
    start = row * ROW_TILE
    if not isinstance(start, int):
        start = pl.multiple_of(start, ROW_TILE)
    return ref.at[pl.ds(start, ROW_TILE)]


def _row_copy(src, src_row, dst, dst_row, sem):
    return pltpu.make_async_copy(_row_slab(src, src_row), _row_slab(dst, dst_row), sem)


def _dispatch_body(dest_ref, h_ref, g_ref, zero_hbm, xs_hbm, rows_ref, sem):
    del zero_hbm
    tm = DISPATCH_TOK
    base = pl.program_id(0) * tm
    h = h_ref[...]
    n2 = h * lax.rsqrt(jnp.mean(h * h, axis=-1, keepdims=True) + EPS) * g_ref[...]
    for cidx in range(ROW_TILE):
        rows_ref[pl.ds(cidx, tm, stride=ROW_TILE), :] = n2[:, cidx * LANES:(cidx + 1) * LANES]

    def issue(i, carry):
        for k in range(TOP_K):
            _row_copy(rows_ref, i, xs_hbm, dest_ref[(base + i) * TOP_K + k], sem).start()
        return carry

    lax.fori_loop(0, tm, issue, 0)

    def drain(i, carry):
        for k in range(TOP_K):
            _row_copy(rows_ref, 0, xs_hbm, 0, sem).wait()
        return carry

    lax.fori_loop(0, tm, drain, 0)


def _dispatch(dest_flat, h, g):
    tm = DISPATCH_TOK
    zeros = jnp.zeros((MOE_ROWS * ROW_TILE, LANES), F32)
    return pl.pallas_call(
        _dispatch_body,
        grid_spec=pltpu.PrefetchScalarGridSpec(
            num_scalar_prefetch=1,
            grid=(N_TOK // tm,),
            in_specs=[pl.BlockSpec((tm, D_MODEL), lambda i, d: (i, 0)),
                      pl.BlockSpec((1, D_MODEL), lambda i, d: (0, 0)),
                      pl.BlockSpec(memory_space=pl.ANY)],
            out_specs=pl.BlockSpec(memory_space=pl.ANY),
            scratch_shapes=[pltpu.VMEM((tm * ROW_TILE, LANES), F32), pltpu.SemaphoreType.DMA(())]),
        out_shape=jax.ShapeDtypeStruct((MOE_ROWS * ROW_TILE, LANES), F32),
        input_output_aliases={3: 0},
        compiler_params=_cparams(1, 32),
        name="moe_dispatch",
    )(dest_flat, h, g, zeros)


def _expert_body(e_ref, r_ref, n_ref, xs_hbm, wg_ref, wu_ref, bg_ref, bu_ref, wd_ref, bd_ref, yb_hbm,
                 stage_ref, xb_ref, act_ref, sem_in, sem_out):
    del e_ref
    s = pl.program_id(0)
    j = pl.program_id(1)
    nsub = n_ref[s]
    row0 = r_ref[s]
    sub_rows = MOE_SUB * ROW_TILE

    def in_copy(k):
        return pltpu.make_async_copy(
            xs_hbm.at[pl.ds(pl.multiple_of((row0 + k * MOE_SUB) * ROW_TILE, sub_rows), sub_rows)],
            stage_ref.at[pl.ds(k * sub_rows, sub_rows)], sem_in)

    def out_copy(k):
        return pltpu.make_async_copy(
            stage_ref.at[pl.ds(k * sub_rows, sub_rows)],
            yb_hbm.at[pl.ds(pl.multiple_of((row0 + k * MOE_SUB) * ROW_TILE, sub_rows), sub_rows)], sem_out)

    @pl.when(j == 0)
    def _():
        for k in range(MOE_NSB):
            @pl.when(k < nsub)
            def _():
                in_copy(k).start()
        for k in range(MOE_NSB):
            @pl.when(k < nsub)
            def _():
                in_copy(k).wait()
        for k in range(MOE_NSB):
            @pl.when(k < nsub)
            def _():
                for cidx in range(ROW_TILE):
                    xb_ref[k * MOE_SUB:(k + 1) * MOE_SUB, cidx * LANES:(cidx + 1) * LANES] = (
                        stage_ref[pl.ds(k * sub_rows + cidx, MOE_SUB, stride=ROW_TILE), :].astype(BF16))

    @pl.when(j < MOE_J1)
    def _():
        wg = wg_ref[...].astype(BF16)
        wu = wu_ref[...].astype(BF16)
        for k in range(MOE_NSB):
            @pl.when(k < nsub)
            def _():
                x = xb_ref[k * MOE_SUB:(k + 1) * MOE_SUB, :]
                gate = jnp.dot(x, wg, preferred_element_type=F32) + bg_ref[...]
                up = jnp.dot(x, wu, preferred_element_type=F32) + bu_ref[...]
                gate = jnp.minimum(gate, SWIGLU_LIMIT)
                up = jnp.clip(up, -SWIGLU_LIMIT, SWIGLU_LIMIT)
                a = gate * _sigmoid(SWIGLU_ALPHA * gate) * (up + 1.0)
                act_ref[jnp.minimum(j, MOE_J1 - 1), k * MOE_SUB:(k + 1) * MOE_SUB, :] = a.astype(BF16)

    per = MOE_TN // LANES
    for jd in range(MOE_J2):
        @pl.when(j == MOE_J1 + jd)
        def _():
            wd = wd_ref[...].astype(BF16)
            for k in range(MOE_NSB):
                @pl.when(k < nsub)
                def _():
                    y = bd_ref[...] + jnp.dot(act_ref[0, k * MOE_SUB:(k + 1) * MOE_SUB, :], wd[0:MOE_TN, :],
                                              preferred_element_type=F32)
                    for q in range(1, MOE_J1):
                        y = y + jnp.dot(act_ref[q, k * MOE_SUB:(k + 1) * MOE_SUB, :],
                                        wd[q * MOE_TN:(q + 1) * MOE_TN, :], preferred_element_type=F32)
                    for cc in range(per):
                        stage_ref[pl.ds(k * sub_rows + jd * per + cc, MOE_SUB, stride=ROW_TILE), :] = (
                            y[:, cc * LANES:(cc + 1) * LANES])

    @pl.when(j == MOE_J1 + MOE_J2 - 1)
    def _():
        for k in range(MOE_NSB):
            @pl.when(k < nsub)
            def _():
                out_copy(k).start()
        for k in range(MOE_NSB):
            @pl.when(k < nsub)
            def _():
                out_copy(k).wait()


def _experts(sup_e, sup_row, sup_n, xs_rows, w_up, b_up, w_down, b_down):
    def up_j(s, j, e, r, n):
        return jnp.where(n[s] > 0, jnp.minimum(j, MOE_J1 - 1), MOE_J1 - 1)

    def dn_j(s, j, e, r, n):
        return jnp.where(n[s] > 0, jnp.maximum(j - MOE_J1, 0), MOE_J2 - 1)

    b_up3 = b_up.reshape(N_EXPERTS, 1, 2 * D_FF)
    b_down3 = b_down.reshape(N_EXPERTS, 1, D_MODEL)
    return pl.pallas_call(
        _expert_body,
        grid_spec=pltpu.PrefetchScalarGridSpec(
            num_scalar_prefetch=3,
            grid=(MOE_SMAX, MOE_J1 + MOE_J2),
            in_specs=[
                pl.BlockSpec(memory_space=pl.ANY),
                pl.BlockSpec((None, D_MODEL, MOE_TN), lambda s, j, e, r, n: (e[s], 0, up_j(s, j, e, r, n))),
                pl.BlockSpec((None, D_MODEL, MOE_TN),
                             lambda s, j, e, r, n: (e[s], 0, MOE_J1 + up_j(s, j, e, r, n))),
                pl.BlockSpec((None, 1, MOE_TN), lambda s, j, e, r, n: (e[s], 0, up_j(s, j, e, r, n))),
                pl.BlockSpec((None, 1, MOE_TN), lambda s, j, e, r, n: (e[s], 0, MOE_J1 + up_j(s, j, e, r, n))),
                pl.BlockSpec((None, D_FF, MOE_TN), lambda s, j, e, r, n: (e[s], 0, dn_j(s, j, e, r, n))),
                pl.BlockSpec((None, 1, MOE_TN), lambda s, j, e, r, n: (e[s], 0, dn_j(s, j, e, r, n))),
            ],
            out_specs=pl.BlockSpec(memory_space=pl.ANY),
            scratch_shapes=[pltpu.VMEM((MOE_TS * ROW_TILE, LANES), F32),
                            pltpu.VMEM((MOE_TS, D_MODEL), BF16),
                            pltpu.VMEM((MOE_J1, MOE_TS, MOE_TN), BF16),
                            pltpu.SemaphoreType.DMA(()),
                            pltpu.SemaphoreType.DMA(())]),
        out_shape=jax.ShapeDtypeStruct((MOE_ROWS * ROW_TILE, LANES), F32),
        input_output_aliases={3: 0},
        compiler_params=_cparams(2, 56),
        name="moe_experts",
    )(sup_e, sup_row, sup_n, xs_rows, w_up, w_up, b_up3, b_up3, w_down, b_down3)


COMB_TM = 256
COMB_TN = 512
COMB_J = D_MODEL // COMB_TN


def _combine_body(dest_ref, yb_hbm, h_ref, gt_ref, gp_ref, wg_ref, p_ref, wu_ref, gf_ref, o_ref,
                  gath_ref, h2_ref, h3_ref, n3_ref, sem):
    tm = COMB_TM
    i = pl.program_id(0)
    j = pl.program_id(1)
    per = COMB_TN // LANES

    @pl.when(j == 0)
    def _():
        base = i * tm

        def issue(r, carry):
            for k in range(TOP_K):
                _row_copy(yb_hbm, dest_ref[(base + r) * TOP_K + k], gath_ref, k * tm + r, sem).start()
            return carry

        lax.fori_loop(0, tm, issue, 0)

        def drain(r, carry):
            for k in range(TOP_K):
                _row_copy(yb_hbm, 0, gath_ref, 0, sem).wait()
            return carry

        lax.fori_loop(0, tm, drain, 0)

        gt = gt_ref[...]
        ssq = jnp.zeros((tm, 1), F32)
        for cidx in range(ROW_TILE):
            acc = h_ref[:, cidx * LANES:(cidx + 1) * LANES]
            for k in range(TOP_K):
                acc = acc + gt[:, k:k + 1] * gath_ref[pl.ds(k * tm * ROW_TILE + cidx, tm, stride=ROW_TILE), :]
            h2_ref[cidx // per, :, (cidx % per) * LANES:(cidx % per + 1) * LANES] = acc
            ssq = ssq + jnp.sum(acc * acc, axis=-1, keepdims=True)
        scale = lax.rsqrt(ssq / D_MODEL + EPS)
        for q in range(COMB_J):
            n3_ref[:, q * COMB_TN:(q + 1) * COMB_TN] = (
                h2_ref[q] * scale * gp_ref[:, q * COMB_TN:(q + 1) * COMB_TN]).astype(BF16)

    gate = _sigmoid(jnp.dot(n3_ref[...], wg_ref[...].astype(BF16), preferred_element_type=F32))
    up = jnp.dot(p_ref[...].astype(BF16), wu_ref[...].astype(BF16), preferred_element_type=F32)
    h3_ref[j] = h2_ref[j] + gate * up

    @pl.when(j == COMB_J - 1)
    def _():
        ssq = jnp.zeros((tm, 1), F32)
        for q in range(COMB_J):
            v = h3_ref[q]
            ssq = ssq + jnp.sum(v * v, axis=-1, keepdims=True)
        scale = lax.rsqrt(ssq / D_MODEL + EPS)
        for q in range(COMB_J):
            o_ref[:, q * COMB_TN:(q + 1) * COMB_TN] = (
                h3_ref[q] * scale * gf_ref[:, q * COMB_TN:(q + 1) * COMB_TN])


def _combine(dest_flat, yb_rows, h, gates, g_ple, w_ple_gate, p_all, w_ple_up, g_final):
    tm = COMB_TM
    return pl.pallas_call(
        _combine_body,
        grid_spec=pltpu.PrefetchScalarGridSpec(
            num_scalar_prefetch=1,
            grid=(N_TOK // tm, COMB_J),
            in_specs=[
                pl.BlockSpec(memory_space=pl.ANY),
                pl.BlockSpec((tm, D_MODEL), lambda i, j, d: (i, 0)),
                pl.BlockSpec((tm, LANES), lambda i, j, d: (i, 0)),
                pl.BlockSpec((1, D_MODEL), lambda i, j, d: (0, 0)),
                pl.BlockSpec((D_MODEL, COMB_TN), lambda i, j, d: (0, j)),
                pl.BlockSpec((tm, D_PLE), lambda i, j, d: (i, 0)),
                pl.BlockSpec((D_PLE, COMB_TN), lambda i, j, d: (0, j)),
                pl.BlockSpec((1, D_MODEL), lambda i, j, d: (0, 0)),
            ],
            out_specs=pl.BlockSpec((tm, D_MODEL), lambda i, j, d: (i, 0)),
            scratch_shapes=[pltpu.VMEM((TOP_K * tm * ROW_TILE, LANES), F32),
                            pltpu.VMEM((COMB_J, tm, COMB_TN), F32),
                            pltpu.VMEM((COMB_J, tm, COMB_TN), F32),
                            pltpu.VMEM((tm, D_MODEL), BF16),
                            pltpu.SemaphoreType.DMA(())]),
        out_shape=jax.ShapeDtypeStruct((N_TOK, D_MODEL), F32),
        compiler_params=_cparams(2, 48),
        name="combine_ple_norm",
    )(dest_flat, yb_rows, h, gates, g_ple, w_ple_gate, p_all, w_ple_up, g_final)


def _routing_tables(counts, eidx, rank):
    nsub_e = (counts + MOE_SUB - 1) // MOE_SUB
    prow_e = (jnp.cumsum(nsub_e) - nsub_e) * MOE_SUB
    dest = (prow_e[eidx] + rank).reshape(-1).astype(I32)
    nsup_e = (nsub_e + MOE_NSB - 1) // MOE_NSB
    csup = jnp.cumsum(nsup_e)
    total = csup[-1]
    s = jnp.arange(MOE_SMAX, dtype=I32)
    e_raw = jnp.minimum(jnp.searchsorted(csup, s, side="right").astype(I32), N_EXPERTS - 1)
    valid = s < total
    e_last = e_raw[jnp.maximum(total - 1, 0)]
    e_s = jnp.where(valid, e_raw, e_last)
    local = s - (csup[e_s] - nsup_e[e_s])
    row_s = jnp.where(valid, prow_e[e_s] + local * MOE_TS, 0)
    n_s = jnp.where(valid, jnp.clip(nsub_e[e_s] - local * MOE_NSB, 0, MOE_NSB), 0)
    return dest, e_s.astype(I32), row_s.astype(I32), n_s.astype(I32)


def _pad_prev(state, width):
    b, _, c = state.shape
    return jnp.pad(state, ((0, 0), (SUBLANES - (width - 1), 0), (0, 0))).reshape(b * SUBLANES, c)


def kernel(x_prompt, x_sample, p_prompt, p_sample, state_conv_a, state_conv_ssm, state_ssm, g_mix, w_in, conv_a_w, ssm_conv_w, ssm_conv_b, dt_bias, a_log, d_skip, ssm_norm_g, w_branch_a, w_branch_b, w_out, g_ffn, w_router, b_router, w_up, b_up, w_down, b_down, g_ple, w_ple_gate, w_ple_up, g_final):
    x_all = jnp.concatenate([x_prompt.reshape(NP_TOK, D_MODEL), x_sample.reshape(NS_TOK, D_MODEL)], axis=0)
    p_all = jnp.concatenate([p_prompt[0].reshape(NP_TOK, D_PLE), p_sample[0].reshape(NS_TOK, D_PLE)], axis=0)
    w_in0 = w_in[0]
    g_mix2 = g_mix[0].reshape(1, D_MODEL)

    proj = _rms_matmul(x_all, g_mix2, w_in0, 0, N_MAIN, 1024, 512, "in_proj")
    dt_raw = _rms_matmul(x_all, g_mix2, w_in0[:, COL_DT:COL_GATE], 0, SSM_HEADS, 1024, SSM_HEADS, "in_proj_dt")
    gates = _rms_matmul(x_all, g_mix2, w_in0[:, COL_GATE:], 0, 2 * D_MODEL, 1024, 512, "in_proj_gates")

    ya_p, ua_tail = _conv_a_prompt(proj, conv_a_w[0])
    ya_s, ua_s = _conv_a_sample(proj, conv_a_w[0], _pad_prev(state_conv_a[0], CONV_A_W))
    conv_a_prompt = ua_tail[:, SUBLANES - (CONV_A_W - 1):, :]
    conv_a_sample = ua_s.reshape(DEC_BATCH, DEC_SEQ, D_CONV)[:, DEC_SEQ - (CONV_A_W - 1):, :]

    conv_b2 = ssm_conv_b[0].reshape(1, D_XBC)
    xbc_p = _conv_m_prompt(proj, ssm_conv_w[0], conv_b2)
    xbc_s = _conv_m_sample(proj, ssm_conv_w[0], conv_b2, _pad_prev(state_conv_ssm[0], SSM_CONV_W))
    proj3 = proj.reshape(N_TOK // DEC_SEQ, DEC_SEQ, N_MAIN)
    t0 = DEC_SEQ - (SSM_CONV_W - 1)
    conv_m_prompt = proj3[SEQ // DEC_SEQ - 1:NP_TOK // DEC_SEQ:SEQ // DEC_SEQ, t0:, COL_XBC:COL_XBC + D_XBC]
    conv_m_sample = proj3[NP_TOK // DEC_SEQ:, t0:, COL_XBC:COL_XBC + D_XBC]

    ssd_w = (dt_bias[0].reshape(1, SSM_HEADS), dt_bias[0].reshape(SSM_HEADS, 1),
             a_log[0].reshape(1, SSM_HEADS), a_log[0].reshape(SSM_HEADS, 1),
             jnp.repeat(d_skip[0], SSM_HEAD_DIM).reshape(1, D_INNER), ssm_norm_g[0].reshape(1, D_INNER))
    ym_p, ssm_p = _ssd_call(SSM_CHUNK, xbc_p, proj, dt_raw, *ssd_w, None,
                            n_seq=BATCH, n_chunk=SEQ // SSM_CHUNK, row0=0, name="ssd_prompt")
    h0 = state_ssm[0].reshape(DEC_BATCH, SSM_HEADS * SSM_HEAD_DIM, SSM_STATE)
    ym_s, ssm_s = _ssd_call(DEC_SEQ, xbc_s, proj, dt_raw, *ssd_w, h0,
                            n_seq=DEC_BATCH, n_chunk=1, row0=NP_TOK, name="ssd_sample")
    ssm_shape = (SSM_HEADS, SSM_HEAD_DIM, SSM_STATE)

    ya = jnp.concatenate([ya_p, ya_s], axis=0)
    ym = jnp.concatenate([ym_p, ym_s.astype(BF16)], axis=0)
    merged = _merge(ya, ym, w_branch_a[0], w_branch_b[0], gates)
    h = _resid_matmul(merged, w_out[0], x_all)

    wr_pad = jnp.pad(w_router[0], ((0, 0), (0, LANES - N_EXPERTS)))
    br_pad = jnp.pad(b_router[0].reshape(1, N_EXPERTS), ((0, 0), (0, LANES - N_EXPERTS)), constant_values=-1e30)
    g_ffn2 = g_ffn[0].reshape(1, D_MODEL)
    eidx, gate_w, rank, cnt = _router(h, g_ffn2, wr_pad, br_pad)
    counts = cnt[0, :N_EXPERTS].astype(I32)
    dest, sup_e, sup_row, sup_n = _routing_tables(counts, eidx[:, :TOP_K], rank[:, :TOP_K])
    xs_rows = _dispatch(dest, h, g_ffn2)
    yb_rows = _experts(sup_e, sup_row, sup_n, xs_rows, w_up[0], b_up[0], w_down[0], b_down[0])

    y_all = _combine(dest, yb_rows, h, gate_w, g_ple[0].reshape(1, D_MODEL), w_ple_gate[0], p_all,
                     w_ple_up[0], g_final.reshape(1, D_MODEL))
    y_prompt = y_all[:NP_TOK].reshape(BATCH, SEQ, D_MODEL)
    y_sample = y_all[NP_TOK:].reshape(DEC_BATCH, DEC_SEQ, D_MODEL)
    return (y_prompt, y_sample,
            conv_a_prompt[None], conv_m_prompt[None], ssm_p.reshape((1, BATCH) + ssm_shape),
            conv_a_sample[None], conv_m_sample[None], ssm_s.reshape((1, DEC_BATCH) + ssm_shape))
```

```python
import functools

import jax
import jax.numpy as jnp
from jax import lax
from jax.experimental import pallas as pl
from jax.experimental.pallas import tpu as pltpu

F32 = jnp.float32
BF16 = jnp.bfloat16
I32 = jnp.int32
HI = lax.Precision.HIGHEST

D_MODEL = 2048
BATCH = 4
SEQ = 2048
DEC_BATCH = 128
DEC_SEQ = 8
D_PLE = 256
D_CONV = D_MODEL
CONV_A_W = 3
D_INNER = 2 * D_MODEL
SSM_HEAD_DIM = 64
SSM_HEADS = D_INNER // SSM_HEAD_DIM
SSM_GROUPS = 8
SSM_STATE = 128
SSM_CONV_W = 4
SSM_CHUNK = 128
D_XBC = D_INNER + 2 * SSM_GROUPS * SSM_STATE
N_EXPERTS = 32
TOP_K = 4
D_FF = D_MODEL
SWIGLU_LIMIT = 7.0
SWIGLU_ALPHA = 1.702
EPS = 1e-6

NP_TOK = BATCH * SEQ
NS_TOK = DEC_BATCH * DEC_SEQ
N_TOK = NP_TOK + NS_TOK
N_ASSIGN = N_TOK * TOP_K

LANES = 128
SUBLANES = 8
ROW_TILE = D_MODEL // LANES

COL_AX = 0
COL_AB = D_CONV
COL_AC = 2 * D_CONV
COL_Z = 3 * D_CONV
COL_XBC = COL_Z + D_INNER
COL_DT = COL_XBC + D_XBC
COL_GATE = COL_DT + SSM_HEADS
N_MAIN = COL_DT

MOE_SUB = 256
MOE_TS = 1280
MOE_NSB = MOE_TS // MOE_SUB
MOE_ROWS = N_ASSIGN + N_EXPERTS * MOE_SUB
MOE_SMAX = N_ASSIGN // MOE_TS + N_EXPERTS
MOE_TN = 512
MOE_J1 = D_FF // MOE_TN
MOE_J2 = D_MODEL // MOE_TN

MIB = 1024 * 1024


def _cparams(n_grid, vmem_mib):
    return pltpu.CompilerParams(dimension_semantics=("arbitrary",) * n_grid,
                                vmem_limit_bytes=vmem_mib * MIB)


def _sigmoid(x):
    return 1.0 / (1.0 + jnp.exp(-x))


def _softplus(x):
    return jnp.maximum(x, 0.0) + jnp.log1p(jnp.exp(-jnp.abs(x)))


def _rms_mm_body(x_ref, g_ref, w_ref, o_ref, n_ref):
    @pl.when(pl.program_id(1) == 0)
    def _():
        x = x_ref[...]
        ms = jnp.mean(x * x, axis=-1, keepdims=True)
        n_ref[...] = (x * lax.rsqrt(ms + EPS) * g_ref[...]).astype(BF16)

    o_ref[...] = jnp.dot(n_ref[...], w_ref[...].astype(BF16), preferred_element_type=F32)


def _rms_matmul(x, g, w, col0_blocks, ncols, tm, tn, name):
    n, k = x.shape
    return pl.pallas_call(
        _rms_mm_body,
        grid=(n // tm, ncols // tn),
        in_specs=[pl.BlockSpec((tm, k), lambda i, j: (i, 0)),
                  pl.BlockSpec((1, k), lambda i, j: (0, 0)),
                  pl.BlockSpec((k, tn), lambda i, j: (0, col0_blocks + j))],
        out_specs=pl.BlockSpec((tm, tn), lambda i, j: (i, j)),
        out_shape=jax.ShapeDtypeStruct((n, ncols), F32),
        scratch_shapes=[pltpu.VMEM((tm, k), BF16)],
        compiler_params=_cparams(2, 48),
        name=name,
    )(x, g, w)


def _causal_taps(u, w_ref, width, rowmod, prev):
    rows = u.shape[0]
    acc = u * w_ref[width - 1:width, :]
    for s in range(1, width):
        shifted = pltpu.roll(u, s, 0)
        if prev is None:
            fill = jnp.zeros_like(u)
        else:
            fill = pltpu.roll(prev, rows - SUBLANES + s, 0)
        acc = acc + jnp.where(rowmod >= s, shifted, fill) * w_ref[width - 1 - s:width - s, :]
    return acc


def _row_in_seq(shape, period):
    r = lax.broadcasted_iota(I32, shape, 0)
    return r if period >= shape[0] else r & (period - 1)


def _conv_a_body(period, has_prev, *refs):
    if has_prev:
        ax_ref, ab_ref, ac_ref, w_ref, prev_ref, ya_ref, u_ref = refs
        prev = prev_ref[...]
    else:
        ax_ref, ab_ref, ac_ref, w_ref, ya_ref, u_ref = refs
        prev = None
    u = ac_ref[...] * ax_ref[...]
    v = _causal_taps(u, w_ref, CONV_A_W, _row_in_seq(u.shape, period), prev)
    ya_ref[...] = (ab_ref[...] * v).astype(BF16)
    if has_prev:
        u_ref[...] = u
    else:
        u_ref[...] = u[u.shape[0] - SUBLANES:, :]


def _conv_m_body(period, has_prev, *refs):
    if has_prev:
        x_ref, w_ref, b_ref, prev_ref, o_ref = refs
        prev = prev_ref[...]
    else:
        x_ref, w_ref, b_ref, o_ref = refs
        prev = None
    u = x_ref[...]
    v = _causal_taps(u, w_ref, SSM_CONV_W, _row_in_seq(u.shape, period), prev) + b_ref[...]
    o_ref[...] = v * _sigmoid(v)


CONV_TC = 256


def _conv_a_prompt(proj, conv_w):
    nb = D_CONV // CONV_TC
    blk = lambda off: pl.BlockSpec((SEQ, CONV_TC), lambda b, c, off=off: (b, off // CONV_TC + c))
    return pl.pallas_call(
        functools.partial(_conv_a_body, SEQ, False),
        grid=(BATCH, nb),
        in_specs=[blk(COL_AX), blk(COL_AB), blk(COL_AC),
                  pl.BlockSpec((CONV_A_W, CONV_TC), lambda b, c: (0, c))],
        out_specs=[pl.BlockSpec((SEQ, CONV_TC), lambda b, c: (b, c)),
                   pl.BlockSpec((None, SUBLANES, CONV_TC), lambda b, c: (b, 0, c))],
        out_shape=[jax.ShapeDtypeStruct((NP_TOK, D_CONV), BF16),
                   jax.ShapeDtypeStruct((BATCH, SUBLANES, D_CONV), F32)],
        compiler_params=_cparams(2, 48),
        name="conv_a_prompt",
    )(proj, proj, proj, conv_w)


def _conv_a_sample(proj, conv_w, prev_pad):
    nb = D_CONV // CONV_TC
    rb = NP_TOK // NS_TOK
    blk = lambda off: pl.BlockSpec((NS_TOK, CONV_TC), lambda c, off=off: (rb, off // CONV_TC + c))
    own = pl.BlockSpec((NS_TOK, CONV_TC), lambda c: (0, c))
    return pl.pallas_call(
        functools.partial(_conv_a_body, DEC_SEQ, True),
        grid=(nb,),
        in_specs=[blk(COL_AX), blk(COL_AB), blk(COL_AC),
                  pl.BlockSpec((CONV_A_W, CONV_TC), lambda c: (0, c)), own],
        out_specs=[own, own],
        out_shape=[jax.ShapeDtypeStruct((NS_TOK, D_CONV), BF16),
                   jax.ShapeDtypeStruct((NS_TOK, D_CONV), F32)],
        compiler_params=_cparams(1, 48),
        name="conv_a_sample",
    )(proj, proj, proj, conv_w, prev_pad)


def _conv_m_prompt(proj, conv_w, conv_b):
    nb = D_XBC // CONV_TC
    return pl.pallas_call(
        functools.partial(_conv_m_body, SEQ, False),
        grid=(BATCH, nb),
        in_specs=[pl.BlockSpec((SEQ, CONV_TC), lambda b, c: (b, COL_XBC // CONV_TC + c)),
                  pl.BlockSpec((SSM_CONV_W, CONV_TC), lambda b, c: (0, c)),
                  pl.BlockSpec((1, CONV_TC), lambda b, c: (0, c))],
        out_specs=pl.BlockSpec((SEQ, CONV_TC), lambda b, c: (b, c)),
        out_shape=jax.ShapeDtypeStruct((NP_TOK, D_XBC), F32),
        compiler_params=_cparams(2, 48),
        name="conv_m_prompt",
    )(proj, conv_w, conv_b)


def _conv_m_sample(proj, conv_w, conv_b, prev_pad):
    nb = D_XBC // CONV_TC
    rb = NP_TOK // NS_TOK
    own = pl.BlockSpec((NS_TOK, CONV_TC), lambda c: (0, c))
    return pl.pallas_call(
        functools.partial(_conv_m_body, DEC_SEQ, True),
        grid=(nb,),
        in_specs=[pl.BlockSpec((NS_TOK, CONV_TC), lambda c: (rb, COL_XBC // CONV_TC + c)),
                  pl.BlockSpec((SSM_CONV_W, CONV_TC), lambda c: (0, c)),
                  pl.BlockSpec((1, CONV_TC), lambda c: (0, c)), own],
        out_specs=own,
        out_shape=jax.ShapeDtypeStruct((NS_TOK, D_XBC), F32),
        compiler_params=_cparams(1, 48),
        name="conv_m_sample",
    )(proj, conv_w, conv_b, prev_pad)


def _ssd_body(xbc_ref, zlo_ref, zhi_ref, dtr_ref, dtb_ref, dtbc_ref, alog_ref, alogc_ref, dsk_ref, ng_ref,
              ym_ref, hl_ref, ht_ref, y_ref):
    L = SSM_CHUNK
    H = SSM_HEADS
    c = pl.program_id(1)

    @pl.when(c == 0)
    def _():
        ht_ref[...] = jnp.zeros_like(ht_ref)

    dtr = dtr_ref[...]
    dt = _softplus(dtr + dtb_ref[...])
    eye = (lax.broadcasted_iota(I32, (H, H), 0) == lax.broadcasted_iota(I32, (H, H), 1)).astype(F32)
    dtr_t = lax.dot_general(eye, dtr, (((1,), (1,)), ((), ())), precision=HI,
                            preferred_element_type=F32)
    dt_t = _softplus(dtr_t + dtbc_ref[...])
    adt = dt * (-jnp.exp(alog_ref[...]))
    adt_t = dt_t * (-jnp.exp(alogc_ref[...]))
    li = lax.broadcasted_iota(I32, (L, L), 0)
    si = lax.broadcasted_iota(I32, (L, L), 1)
    causal = li >= si
    a_cum = jnp.dot(causal.astype(F32), adt, precision=HI, preferred_element_type=F32)
    a_cum_t = jnp.dot(adt_t, (li <= si).astype(F32), precision=HI, preferred_element_type=F32)
    ea = jnp.exp(a_cum)
    a_tot_t = a_cum_t[:, L - 1:L]
    dte_t = jnp.exp(a_tot_t - a_cum_t)
    ea_tot_t = jnp.exp(a_tot_t)

    lane = lax.broadcasted_iota(I32, (L, LANES), 1)
    lo = lane < SSM_HEAD_DIM
    lane1 = lax.broadcasted_iota(I32, (1, LANES), 1) < SSM_HEAD_DIM
    gn = SSM_GROUPS * SSM_STATE
    for g in range(SSM_GROUPS):
        bg = xbc_ref[:, D_INNER + g * SSM_STATE:D_INNER + (g + 1) * SSM_STATE]
        cg = xbc_ref[:, D_INNER + gn + g * SSM_STATE:D_INNER + gn + (g + 1) * SSM_STATE]
        bg_t = bg.T
        cb = jnp.dot(cg.astype(BF16), bg_t.astype(BF16), preferred_element_type=F32)
        for jj in range(SSM_HEADS // SSM_GROUPS // 2):
            j = g * (SSM_HEADS // SSM_GROUPS // 2) + jj
            cols = slice(j * LANES, (j + 1) * LANES)
            lhs1, lhs2 = [], []
            for h in (2 * j, 2 * j + 1):
                diff = a_cum[:, h:h + 1] - a_cum_t[h:h + 1, :]
                decay = jnp.exp(jnp.where(causal, diff, -1e30))
                lhs1.append((cb * decay).astype(BF16))
            for h in (2 * j, 2 * j + 1):
                lhs1.append((cg * ea[:, h:h + 1]).astype(BF16))
                lhs2.append((bg_t * dte_t[h:h + 1, :]).astype(BF16))
            dt_pair = jnp.where(lo, dt[:, 2 * j:2 * j + 1], dt[:, 2 * j + 1:2 * j + 2])
            xdt = xbc_ref[:, cols] * dt_pair
            zero = jnp.zeros_like(xdt)
            x_top = jnp.where(lo, xdt, zero).astype(BF16)
            x_bot = jnp.where(lo, zero, xdt).astype(BF16)
            ht = ht_ref[:, cols]
            h_top = jnp.where(lo, ht, zero).astype(BF16)
            h_bot = jnp.where(lo, zero, ht).astype(BF16)
            y_ref[:, cols] = jnp.dot(jnp.concatenate(lhs1, axis=1),
                                     jnp.concatenate([x_top, x_bot, h_top, h_bot], axis=0),
                                     preferred_element_type=F32)
            upd = jnp.dot(jnp.concatenate(lhs2, axis=1), jnp.concatenate([x_top, x_bot], axis=0),
                          preferred_element_type=F32)
            keep = jnp.where(lane1, ea_tot_t[2 * j:2 * j + 1, :], ea_tot_t[2 * j + 1:2 * j + 2, :])
            ht_ref[:, cols] = ht * keep + upd

    gw = D_INNER // SSM_GROUPS
    for g in range(SSM_GROUPS):
        cols = slice(g * gw, (g + 1) * gw)
        zr = zlo_ref if g < SSM_GROUPS // 2 else zhi_ref
        zc = slice((g % (SSM_GROUPS // 2)) * gw, (g % (SSM_GROUPS // 2) + 1) * gw)
        z = zr[:, zc]
        y = y_ref[:, cols] + dsk_ref[:, cols] * xbc_ref[:, cols]
        y = y * (z * _sigmoid(z))
        y = y * lax.rsqrt(jnp.mean(y * y, axis=-1, keepdims=True) + EPS)
        ym_ref[:, cols] = (y * ng_ref[:, cols]).astype(ym_ref.dtype)

    @pl.when(c == pl.num_programs(1) - 1)
    def _():
        hl_ref[...] = ht_ref[...].T


def _ssd_call(xbc, proj, dt_raw, dtb, dtbc, alog, alogc, dsk, ng):
    L = SSM_CHUNK
    n_chunk = SEQ // L
    hp = SSM_HEADS * SSM_HEAD_DIM
    zb = D_INNER // 2
    vec = lambda shape: pl.BlockSpec(shape, lambda b, c: (0, 0))
    return pl.pallas_call(
        _ssd_body,
        grid=(BATCH, n_chunk),
        in_specs=[pl.BlockSpec((L, D_XBC), lambda b, c: (b * n_chunk + c, 0)),
                  pl.BlockSpec((L, zb), lambda b, c: (b * n_chunk + c, COL_Z // zb)),
                  pl.BlockSpec((L, zb), lambda b, c: (b * n_chunk + c, COL_Z // zb + 1)),
                  pl.BlockSpec((L, SSM_HEADS), lambda b, c: (b * n_chunk + c, 0)),
                  vec((1, SSM_HEADS)), vec((SSM_HEADS, 1)), vec((1, SSM_HEADS)), vec((SSM_HEADS, 1)),
                  vec((1, D_INNER)), vec((1, D_INNER))],
        out_specs=[pl.BlockSpec((L, D_INNER), lambda b, c: (b * n_chunk + c, 0)),
                   pl.BlockSpec((None, hp, SSM_STATE), lambda b, c: (b, 0, 0))],
        out_shape=[jax.ShapeDtypeStruct((NP_TOK, D_INNER), BF16),
                   jax.ShapeDtypeStruct((BATCH, hp, SSM_STATE), F32)],
        scratch_shapes=[pltpu.VMEM((SSM_STATE, hp), F32), pltpu.VMEM((L, hp), F32)],
        compiler_params=_cparams(2, 48),
        name="ssd_prompt",
    )(xbc, proj, proj, dt_raw, dtb, dtbc, alog, alogc, dsk, ng)


def _ssd_step_body(xbc_in, zlo_ref, zhi_ref, dtr_in, dtb_ref, dtbc_ref, alog_ref, alogc_ref, dsk_ref, ng_ref,
                   h0_ref, ym_ref, hl_ref, xp_ref, dp_ref):
    T = DEC_SEQ
    L = SSM_CHUNK
    H = SSM_HEADS
    P = SSM_HEAD_DIM

    @pl.when(pl.program_id(0) == 0)
    def _():
        xp_ref[...] = jnp.zeros_like(xp_ref)
        dp_ref[...] = jnp.zeros_like(dp_ref)

    xp_ref[0:T, :] = xbc_in[...]
    dp_ref[0:T, :] = dtr_in[...]

    dt8 = _softplus(dtr_in[...] + dtb_ref[...])
    a_cum8 = dt8 * (-jnp.exp(alog_ref[...]))
    row8 = lax.broadcasted_iota(I32, (T, H), 0)
    k = 1
    while k < T:
        a_cum8 = a_cum8 + jnp.where(row8 >= k, pltpu.roll(a_cum8, k, 0), 0.0)
        k *= 2
    ea8 = jnp.exp(a_cum8)
    dte8 = jnp.exp(a_cum8[T - 1:T, :] - a_cum8)
    eye = (lax.broadcasted_iota(I32, (H, H), 0) == lax.broadcasted_iota(I32, (H, H), 1)).astype(F32)
    dtr_t = lax.dot_general(eye, dp_ref[...], (((1,), (1,)), ((), ())), precision=HI,
                            preferred_element_type=F32)
    dt_t = jnp.where(lax.broadcasted_iota(I32, (H, L), 1) < T, _softplus(dtr_t + dtbc_ref[...]), 0.0)
    adt_t = dt_t * (-jnp.exp(alogc_ref[...]))
    upper = (lax.broadcasted_iota(I32, (L, L), 0) <= lax.broadcasted_iota(I32, (L, L), 1)).astype(F32)
    a_cum_t = jnp.dot(adt_t, upper, precision=HI, preferred_element_type=F32)
    ea_tot_t = jnp.exp(a_cum_t[:, L - 1:L])

    causal8 = lax.broadcasted_iota(I32, (T, L), 0) >= lax.broadcasted_iota(I32, (T, L), 1)
    lo8 = lax.broadcasted_iota(I32, (T, LANES), 1) < P
    zeros_t = jnp.zeros((T, LANES), F32)
    zeros_pad = jnp.zeros((L - T, LANES), F32)
    gn = SSM_GROUPS * SSM_STATE
    gw = D_INNER // SSM_GROUPS
    pairs = SSM_HEADS // SSM_GROUPS // 2
    for g in range(SSM_GROUPS):
        bgp = xp_ref[:, D_INNER + g * SSM_STATE:D_INNER + (g + 1) * SSM_STATE].astype(BF16)
        cg8 = xbc_in[:, D_INNER + gn + g * SSM_STATE:D_INNER + gn + (g + 1) * SSM_STATE]
        cg16 = jnp.concatenate([cg8, zeros_t], axis=0).astype(BF16)
        cb = lax.dot_general(cg16, bgp, (((1,), (1,)), ((), ())), preferred_element_type=F32)[0:T]
        ys = []
        for jj in range(pairs):
            j = g * pairs + jj
            ha, hb = 2 * j, 2 * j + 1
            cols = slice(j * LANES, (j + 1) * LANES)

            def decay(h):
                diff = a_cum8[:, h:h + 1] - a_cum_t[h:h + 1, :]
                return jnp.exp(jnp.where(causal8, diff, -1e30))

            m = jnp.concatenate([cb * decay(ha), cb * decay(hb)], axis=0).astype(BF16)
            cd = jnp.concatenate([cg8 * ea8[:, ha:ha + 1], cg8 * ea8[:, hb:hb + 1]], axis=0).astype(BF16)
            xdt8 = xbc_in[:, cols] * jnp.where(lo8, dt8[:, ha:ha + 1], dt8[:, hb:hb + 1])
            xd8 = xdt8 * jnp.where(lo8, dte8[:, ha:ha + 1], dte8[:, hb:hb + 1])
            xdt_pad = jnp.concatenate([xdt8, zeros_pad], axis=0).astype(BF16)
            xd_pad = jnp.concatenate([xd8, zeros_pad], axis=0).astype(BF16)
            hp = h0_ref[ha:hb + 1].reshape(2 * P, SSM_STATE)
            out = (jnp.dot(m, xdt_pad, preferred_element_type=F32)
                   + lax.dot_general(cd, hp.astype(BF16), (((1,), (1,)), ((), ())),
                                     preferred_element_type=F32))
            ys.append(jnp.where(lo8, out[0:T], out[T:2 * T]))
            upd = lax.dot_general(xd_pad, bgp, (((0,), (0,)), ((), ())), preferred_element_type=F32)
            keep = jnp.concatenate([jnp.broadcast_to(ea_tot_t[ha:ha + 1, :], (P, SSM_STATE)),
                                    jnp.broadcast_to(ea_tot_t[hb:hb + 1, :], (P, SSM_STATE))], axis=0)
            hl_ref[ha:hb + 1] = (hp * keep + upd).reshape(2, P, SSM_STATE)

        cols = slice(g * gw, (g + 1) * gw)
        zr = zlo_ref if g < SSM_GROUPS // 2 else zhi_ref
        zc = slice((g % (SSM_GROUPS // 2)) * gw, (g % (SSM_GROUPS // 2) + 1) * gw)
        z = zr[:, zc]
        y = jnp.concatenate(ys, axis=1) + dsk_ref[:, cols] * xbc_in[:, cols]
        y = y * (z * _sigmoid(z))
        y = y * lax.rsqrt(jnp.mean(y * y, axis=-1, keepdims=True) + EPS)
        ym_ref[:, cols] = y * ng_ref[:, cols]


def _ssd_step_call(xbc, proj, dt_raw, dtb, dtbc, alog, alogc, dsk, ng, state):
    T = DEC_SEQ
    rb0 = NP_TOK // T
    zb = D_INNER // 2
    st = pl.BlockSpec((None, None, SSM_HEADS, SSM_HEAD_DIM, SSM_STATE), lambda b: (0, b, 0, 0, 0))
    vec = lambda shape: pl.BlockSpec(shape, lambda b: (0, 0))
    return pl.pallas_call(
        _ssd_step_body,
        grid=(DEC_BATCH,),
        in_specs=[pl.BlockSpec((T, D_XBC), lambda b: (b, 0)),
                  pl.BlockSpec((T, zb), lambda b: (rb0 + b, COL_Z // zb)),
                  pl.BlockSpec((T, zb), lambda b: (rb0 + b, COL_Z // zb + 1)),
                  pl.BlockSpec((T, SSM_HEADS), lambda b: (rb0 + b, 0)),
                  vec((1, SSM_HEADS)), vec((SSM_HEADS, 1)), vec((1, SSM_HEADS)), vec((SSM_HEADS, 1)),
                  vec((1, D_INNER)), vec((1, D_INNER)), st],
        out_specs=[pl.BlockSpec((T, D_INNER), lambda b: (b, 0)), st],
        out_shape=[jax.ShapeDtypeStruct((NS_TOK, D_INNER), F32),
                   jax.ShapeDtypeStruct(state.shape, F32)],
        scratch_shapes=[pltpu.VMEM((SSM_CHUNK, D_XBC), F32), pltpu.VMEM((SSM_CHUNK, SSM_HEADS), F32)],
        compiler_params=_cparams(1, 32),
        name="ssd_sample",
    )(xbc, proj, proj, dt_raw, dtb, dtbc, alog, alogc, dsk, ng, state)


def _merge_body(ya_ref, ym_ref, wa_ref, wb_ref, ga_ref, gb_ref, o_ref):
    a = jnp.dot(ya_ref[...], wa_ref[...].astype(BF16), preferred_element_type=F32)
    b = jnp.dot(ym_ref[...], wb_ref[...].astype(BF16), preferred_element_type=F32)
    o_ref[...] = (_sigmoid(ga_ref[...]) * a + _sigmoid(gb_ref[...]) * b).astype(BF16)


def _merge(ya, ym, wa, wb, gates):
    tm, tn = 1024, 256
    return pl.pallas_call(
        _merge_body,
        grid=(N_TOK // tm, D_MODEL // tn),
        in_specs=[pl.BlockSpec((tm, D_CONV), lambda i, j: (i, 0)),
                  pl.BlockSpec((tm, D_INNER), lambda i, j: (i, 0)),
                  pl.BlockSpec((D_CONV, tn), lambda i, j: (0, j)),
                  pl.BlockSpec((D_INNER, tn), lambda i, j: (0, j)),
                  pl.BlockSpec((tm, tn), lambda i, j: (i, j)),
                  pl.BlockSpec((tm, tn), lambda i, j: (i, D_MODEL // tn + j))],
        out_specs=pl.BlockSpec((tm, tn), lambda i, j: (i, j)),
        out_shape=jax.ShapeDtypeStruct((N_TOK, D_MODEL), BF16),
        compiler_params=_cparams(2, 52),
        name="merge",
    )(ya, ym, wa, wb, gates, gates)


def _resid_body(m_ref, w_ref, x_ref, o_ref):
    o_ref[...] = x_ref[...] + jnp.dot(m_ref[...], w_ref[...].astype(BF16), preferred_element_type=F32)


def _resid_matmul(m, w, x):
    tm, tn = 1024, 512
    return pl.pallas_call(
        _resid_body,
        grid=(N_TOK // tm, D_MODEL // tn),
        in_specs=[pl.BlockSpec((tm, D_MODEL), lambda i, j: (i, 0)),
                  pl.BlockSpec((D_MODEL, tn), lambda i, j: (0, j)),
                  pl.BlockSpec((tm, tn), lambda i, j: (i, j))],
        out_specs=pl.BlockSpec((tm, tn), lambda i, j: (i, j)),
        out_shape=jax.ShapeDtypeStruct((N_TOK, D_MODEL), F32),
        compiler_params=_cparams(2, 48),
        name="out_proj",
    )(m, w, x)


ROUTER_TM = 512


def _router_body(h_ref, g_ref, wr_ref, br_ref, ei_ref, gt_ref, rk_ref, cnt_ref, carry_ref):
    tm = ROUTER_TM

    @pl.when(pl.program_id(0) == 0)
    def _():
        carry_ref[...] = jnp.zeros_like(carry_ref)

    h = h_ref[...]
    n2 = h * lax.rsqrt(jnp.mean(h * h, axis=-1, keepdims=True) + EPS) * g_ref[...]
    logits = jnp.dot(n2, wr_ref[...], precision=HI, preferred_element_type=F32) + br_ref[...]

    lane = lax.broadcasted_iota(I32, (tm, LANES), 1).astype(F32)
    work = logits
    vals, idxs = [], []
    for _ in range(TOP_K):
        m = jnp.max(work, axis=-1, keepdims=True)
        idx = jnp.min(jnp.where(work == m, lane, float(LANES)), axis=-1, keepdims=True)
        vals.append(m)
        idxs.append(idx)
        work = jnp.where(lane == idx, -jnp.inf, work)
    ex = [jnp.exp(v - vals[0]) for v in vals]
    den = ex[0] + ex[1] + ex[2] + ex[3]
    sel = jnp.zeros((tm, LANES), F32)
    for idx in idxs:
        sel = sel + jnp.where(lane == idx, 1.0, 0.0)
    before = (lax.broadcasted_iota(I32, (tm, tm), 0) > lax.broadcasted_iota(I32, (tm, tm), 1))
    rank_e = jnp.dot(before.astype(BF16), sel.astype(BF16), preferred_element_type=F32) + carry_ref[0:1, :]
    carry_ref[...] = carry_ref[...] + jnp.sum(sel, axis=0, keepdims=True)
    ei = jnp.zeros((tm, LANES), F32)
    gt = jnp.zeros((tm, LANES), F32)
    rk = jnp.zeros((tm, LANES), F32)
    for k in range(TOP_K):
        rank_k = jnp.sum(jnp.where(lane == idxs[k], rank_e, 0.0), axis=-1, keepdims=True)
        here = lane == float(k)
        ei = jnp.where(here, idxs[k], ei)
        gt = jnp.where(here, ex[k] / den, gt)
        rk = jnp.where(here, rank_k, rk)
    ei_ref[...] = ei.astype(I32)
    gt_ref[...] = gt
    rk_ref[...] = rk.astype(I32)
    cnt_ref[...] = carry_ref[...]


def _router(h, g, wr_pad, br_pad):
    tm = ROUTER_TM
    tok = pl.BlockSpec((tm, LANES), lambda i: (i, 0))
    return pl.pallas_call(
        _router_body,
        grid=(N_TOK // tm,),
        in_specs=[pl.BlockSpec((tm, D_MODEL), lambda i: (i, 0)),
                  pl.BlockSpec((1, D_MODEL), lambda i: (0, 0)),
                  pl.BlockSpec((D_MODEL, LANES), lambda i: (0, 0)),
                  pl.BlockSpec((1, LANES), lambda i: (0, 0))],
        out_specs=[tok, tok, tok, pl.BlockSpec((SUBLANES, LANES), lambda i: (0, 0))],
        out_shape=[jax.ShapeDtypeStruct((N_TOK, LANES), I32),
                   jax.ShapeDtypeStruct((N_TOK, LANES), F32),
                   jax.ShapeDtypeStruct((N_TOK, LANES), I32),
                   jax.ShapeDtypeStruct((SUBLANES, LANES), F32)],
        scratch_shapes=[pltpu.VMEM((SUBLANES, LANES), F32)],
        compiler_params=_cparams(1, 48),
        name="router",
    )(h, g, wr_pad, br_pad)


DISPATCH_TOK = 512


def _row_slab(ref, row):
    start = row * ROW_TILE
    if not isinstance(start, int):
        start = pl.multiple_of(start, ROW_TILE)
    return ref.at[pl.ds(start, ROW_TILE)]


def _row_copy(src, src_row, dst, dst_row, sem):
    return pltpu.make_async_copy(_row_slab(src, src_row), _row_slab(dst, dst_row), sem)


def _dispatch_body(dest_ref, h_ref, g_ref, zero_hbm, xs_hbm, rows_ref, sem):
    del zero_hbm
    tm = DISPATCH_TOK
    base = pl.program_id(0) * tm
    h = h_ref[...]
    n2 = h * lax.rsqrt(jnp.mean(h * h, axis=-1, keepdims=True) + EPS) * g_ref[...]
    for cidx in range(ROW_TILE):
        rows_ref[pl.ds(cidx, tm, stride=ROW_TILE), :] = n2[:, cidx * LANES:(cidx + 1) * LANES]

    def issue(i, carry):
        for k in range(TOP_K):
            _row_copy(rows_ref, i, xs_hbm, dest_ref[(base + i) * TOP_K + k], sem).start()
        return carry

    lax.fori_loop(0, tm, issue, 0)

    def drain(i, carry):
        for k in range(TOP_K):
            _row_copy(rows_ref, 0, xs_hbm, 0, sem).wait()
        return carry

    lax.fori_loop(0, tm, drain, 0)


def _dispatch(dest_flat, h, g):
    tm = DISPATCH_TOK
    zeros = jnp.zeros((MOE_ROWS * ROW_TILE, LANES), F32)
    return pl.pallas_call(
        _dispatch_body,
        grid_spec=pltpu.PrefetchScalarGridSpec(
            num_scalar_prefetch=1,
            grid=(N_TOK // tm,),
            in_specs=[pl.BlockSpec((tm, D_MODEL), lambda i, d: (i, 0)),
                      pl.BlockSpec((1, D_MODEL), lambda i, d: (0, 0)),
                      pl.BlockSpec(memory_space=pl.ANY)],
            out_specs=pl.BlockSpec(memory_space=pl.ANY),
            scratch_shapes=[pltpu.VMEM((tm * ROW_TILE, LANES), F32), pltpu.SemaphoreType.DMA(())]),
        out_shape=jax.ShapeDtypeStruct((MOE_ROWS * ROW_TILE, LANES), F32),
        input_output_aliases={3: 0},
        compiler_params=_cparams(1, 32),
        name="moe_dispatch",
    )(dest_flat, h, g, zeros)


def _expert_body(e_ref, r_ref, n_ref, xs_hbm, wg_ref, wu_ref, bg_ref, bu_ref, wd_ref, bd_ref, yb_hbm,
                 stage_ref, xb_ref, act_ref, sem_in, sem_out):
    del e_ref
    s = pl.program_id(0)
    j = pl.program_id(1)
    nsub = n_ref[s]
    row0 = r_ref[s]
    sub_rows = MOE_SUB * ROW_TILE

    def in_copy(k):
        return pltpu.make_async_copy(
            xs_hbm.at[pl.ds(pl.multiple_of((row0 + k * MOE_SUB) * ROW_TILE, sub_rows), sub_rows)],
            stage_ref.at[pl.ds(k * sub_rows, sub_rows)], sem_in)

    def out_copy(k):
        return pltpu.make_async_copy(
            stage_ref.at[pl.ds(k * sub_rows, sub_rows)],
            yb_hbm.at[pl.ds(pl.multiple_of((row0 + k * MOE_SUB) * ROW_TILE, sub_rows), sub_rows)], sem_out)

    @pl.when(j == 0)
    def _():
        for k in range(MOE_NSB):
            @pl.when(k < nsub)
            def _():
                in_copy(k).start()
        for k in range(MOE_NSB):
            @pl.when(k < nsub)
            def _():
                in_copy(k).wait()
        for k in range(MOE_NSB):
            @pl.when(k < nsub)
            def _():
                for cidx in range(ROW_TILE):
                    xb_ref[k * MOE_SUB:(k + 1) * MOE_SUB, cidx * LANES:(cidx + 1) * LANES] = (
                        stage_ref[pl.ds(k * sub_rows + cidx, MOE_SUB, stride=ROW_TILE), :].astype(BF16))

    @pl.when(j < MOE_J1)
    def _():
        wg = wg_ref[...].astype(BF16)
        wu = wu_ref[...].astype(BF16)
        for k in range(MOE_NSB):
            @pl.when(k < nsub)
            def _():
                x = xb_ref[k * MOE_SUB:(k + 1) * MOE_SUB, :]
                gate = jnp.dot(x, wg, preferred_element_type=F32) + bg_ref[...]
                up = jnp.dot(x, wu, preferred_element_type=F32) + bu_ref[...]
                gate = jnp.minimum(gate, SWIGLU_LIMIT)
                up = jnp.clip(up, -SWIGLU_LIMIT, SWIGLU_LIMIT)
                a = gate * _sigmoid(SWIGLU_ALPHA * gate) * (up + 1.0)
                act_ref[jnp.minimum(j, MOE_J1 - 1), k * MOE_SUB:(k + 1) * MOE_SUB, :] = a.astype(BF16)

    per = MOE_TN // LANES
    for jd in range(MOE_J2):
        @pl.when(j == MOE_J1 + jd)
        def _():
            wd = wd_ref[...].astype(BF16)
            for k in range(MOE_NSB):
                @pl.when(k < nsub)
                def _():
                    y = bd_ref[...] + jnp.dot(act_ref[0, k * MOE_SUB:(k + 1) * MOE_SUB, :], wd[0:MOE_TN, :],
                                              preferred_element_type=F32)
                    for q in range(1, MOE_J1):
                        y = y + jnp.dot(act_ref[q, k * MOE_SUB:(k + 1) * MOE_SUB, :],
                                        wd[q * MOE_TN:(q + 1) * MOE_TN, :], preferred_element_type=F32)
                    for cc in range(per):
                        stage_ref[pl.ds(k * sub_rows + jd * per + cc, MOE_SUB, stride=ROW_TILE), :] = (
                            y[:, cc * LANES:(cc + 1) * LANES])

    @pl.when(j == MOE_J1 + MOE_J2 - 1)
    def _():
        for k in range(MOE_NSB):
            @pl.when(k < nsub)
            def _():
                out_copy(k).start()
        for k in range(MOE_NSB):
            @pl.when(k < nsub)
            def _():
                out_copy(k).wait()


def _experts(sup_e, sup_row, sup_n, xs_rows, w_up, b_up, w_down, b_down):
    def up_j(s, j, e, r, n):
        return jnp.where(n[s] > 0, jnp.minimum(j, MOE_J1 - 1), MOE_J1 - 1)

    def dn_j(s, j, e, r, n):
        return jnp.where(n[s] > 0, jnp.maximum(j - MOE_J1, 0), MOE_J2 - 1)

    b_up3 = b_up.reshape(N_EXPERTS, 1, 2 * D_FF)
    b_down3 = b_down.reshape(N_EXPERTS, 1, D_MODEL)
    return pl.pallas_call(
        _expert_body,
        grid_spec=pltpu.PrefetchScalarGridSpec(
            num_scalar_prefetch=3,
            grid=(MOE_SMAX, MOE_J1 + MOE_J2),
            in_specs=[
                pl.BlockSpec(memory_space=pl.ANY),
                pl.BlockSpec((None, D_MODEL, MOE_TN), lambda s, j, e, r, n: (e[s], 0, up_j(s, j, e, r, n))),
                pl.BlockSpec((None, D_MODEL, MOE_TN),
                             lambda s, j, e, r, n: (e[s], 0, MOE_J1 + up_j(s, j, e, r, n))),
                pl.BlockSpec((None, 1, MOE_TN), lambda s, j, e, r, n: (e[s], 0, up_j(s, j, e, r, n))),
                pl.BlockSpec((None, 1, MOE_TN), lambda s, j, e, r, n: (e[s], 0, MOE_J1 + up_j(s, j, e, r, n))),
                pl.BlockSpec((None, D_FF, MOE_TN), lambda s, j, e, r, n: (e[s], 0, dn_j(s, j, e, r, n))),
                pl.BlockSpec((None, 1, MOE_TN), lambda s, j, e, r, n: (e[s], 0, dn_j(s, j, e, r, n))),
            ],
            out_specs=pl.BlockSpec(memory_space=pl.ANY),
            scratch_shapes=[pltpu.VMEM((MOE_TS * ROW_TILE, LANES), F32),
                            pltpu.VMEM((MOE_TS, D_MODEL), BF16),
                            pltpu.VMEM((MOE_J1, MOE_TS, MOE_TN), BF16),
                            pltpu.SemaphoreType.DMA(()),
                            pltpu.SemaphoreType.DMA(())]),
        out_shape=jax.ShapeDtypeStruct((MOE_ROWS * ROW_TILE, LANES), F32),
        input_output_aliases={3: 0},
        compiler_params=_cparams(2, 58),
        name="moe_experts",
    )(sup_e, sup_row, sup_n, xs_rows, w_up, w_up, b_up3, b_up3, w_down, b_down3)


COMB_TM = 512
COMB_TN = 512
COMB_J = D_MODEL // COMB_TN


def _combine_body(dest_ref, yb_hbm, h_ref, gt_ref, gp_ref, wg_ref, p_ref, wu_ref, gf_ref, o_ref,
                  gath_ref, n3_ref, sem):
    tm = COMB_TM
    i = pl.program_id(0)
    j = pl.program_id(1)

    @pl.when(j == 0)
    def _():
        base = i * tm

        def issue(r, carry):
            for k in range(TOP_K):
                _row_copy(yb_hbm, dest_ref[(base + r) * TOP_K + k], gath_ref, k * tm + r, sem).start()
            return carry

        lax.fori_loop(0, tm, issue, 0)

        def drain(r, carry):
            for k in range(TOP_K):
                _row_copy(yb_hbm, 0, gath_ref, 0, sem).wait()
            return carry

        lax.fori_loop(0, tm, drain, 0)

        gt = gt_ref[...]
        ssq = jnp.zeros((tm, 1), F32)
        for cidx in range(ROW_TILE):
            cols = slice(cidx * LANES, (cidx + 1) * LANES)
            acc = h_ref[:, cols]
            for k in range(TOP_K):
                acc = acc + gt[:, k:k + 1] * gath_ref[pl.ds(k * tm * ROW_TILE + cidx, tm, stride=ROW_TILE), :]
            o_ref[:, cols] = acc
            ssq = ssq + jnp.sum(acc * acc, axis=-1, keepdims=True)
        scale = lax.rsqrt(ssq / D_MODEL + EPS)
        for q in range(COMB_J):
            cols = slice(q * COMB_TN, (q + 1) * COMB_TN)
            n3_ref[:, cols] = (o_ref[:, cols] * scale * gp_ref[:, cols]).astype(BF16)

    for q in range(COMB_J):
        @pl.when(j == q)
        def _():
            cols = slice(q * COMB_TN, (q + 1) * COMB_TN)
            gate = _sigmoid(jnp.dot(n3_ref[...], wg_ref[...].astype(BF16), preferred_element_type=F32))
            up = jnp.dot(p_ref[...].astype(BF16), wu_ref[...].astype(BF16), preferred_element_type=F32)
            o_ref[:, cols] = o_ref[:, cols] + gate * up

    @pl.when(j == COMB_J - 1)
    def _():
        ssq = jnp.zeros((tm, 1), F32)
        for q in range(COMB_J):
            v = o_ref[:, q * COMB_TN:(q + 1) * COMB_TN]
            ssq = ssq + jnp.sum(v * v, axis=-1, keepdims=True)
        scale = lax.rsqrt(ssq / D_MODEL + EPS)
        for q in range(COMB_J):
            cols = slice(q * COMB_TN, (q + 1) * COMB_TN)
            o_ref[:, cols] = o_ref[:, cols] * scale * gf_ref[:, cols]


def _combine(dest_flat, yb_rows, h, gates, g_ple, w_ple_gate, p_all, w_ple_up, g_final):
    tm = COMB_TM
    return pl.pallas_call(
        _combine_body,
        grid_spec=pltpu.PrefetchScalarGridSpec(
            num_scalar_prefetch=1,
            grid=(N_TOK // tm, COMB_J),
            in_specs=[
                pl.BlockSpec(memory_space=pl.ANY),
                pl.BlockSpec((tm, D_MODEL), lambda i, j, d: (i, 0)),
                pl.BlockSpec((tm, LANES), lambda i, j, d: (i, 0)),
                pl.BlockSpec((1, D_MODEL), lambda i, j, d: (0, 0)),
                pl.BlockSpec((D_MODEL, COMB_TN), lambda i, j, d: (0, j)),
                pl.BlockSpec((tm, D_PLE), lambda i, j, d: (i, 0)),
                pl.BlockSpec((D_PLE, COMB_TN), lambda i, j, d: (0, j)),
                pl.BlockSpec((1, D_MODEL), lambda i, j, d: (0, 0)),
            ],
            out_specs=pl.BlockSpec((tm, D_MODEL), lambda i, j, d: (i, 0)),
            scratch_shapes=[pltpu.VMEM((TOP_K * tm * ROW_TILE, LANES), F32),
                            pltpu.VMEM((tm, D_MODEL), BF16),
                            pltpu.SemaphoreType.DMA(())]),
        out_shape=jax.ShapeDtypeStruct((N_TOK, D_MODEL), F32),
        compiler_params=_cparams(2, 56),
        name="combine_ple_norm",
    )(dest_flat, yb_rows, h, gates, g_ple, w_ple_gate, p_all, w_ple_up, g_final)


def _routing_tables(counts, eidx, rank):
    nsub_e = (counts + MOE_SUB - 1) // MOE_SUB
    prow_e = (jnp.cumsum(nsub_e) - nsub_e) * MOE_SUB
    dest = (prow_e[eidx] + rank).reshape(-1).astype(I32)
    nsup_e = (nsub_e + MOE_NSB - 1) // MOE_NSB
    csup = jnp.cumsum(nsup_e)
    total = csup[-1]
    s = jnp.arange(MOE_SMAX, dtype=I32)
    e_raw = jnp.minimum(jnp.searchsorted(csup, s, side="right").astype(I32), N_EXPERTS - 1)
    valid = s < total
    e_last = e_raw[jnp.maximum(total - 1, 0)]
    e_s = jnp.where(valid, e_raw, e_last)
    local = s - (csup[e_s] - nsup_e[e_s])
    row_s = jnp.where(valid, prow_e[e_s] + local * MOE_TS, 0)
    n_s = jnp.where(valid, jnp.clip(nsub_e[e_s] - local * MOE_NSB, 0, MOE_NSB), 0)
    return dest, e_s.astype(I32), row_s.astype(I32), n_s.astype(I32)


def _pad_prev(state, width):
    b, _, c = state.shape
    return jnp.pad(state, ((0, 0), (SUBLANES - (width - 1), 0), (0, 0))).reshape(b * SUBLANES, c)


def kernel(x_prompt, x_sample, p_prompt, p_sample, state_conv_a, state_conv_ssm, state_ssm, g_mix, w_in, conv_a_w, ssm_conv_w, ssm_conv_b, dt_bias, a_log, d_skip, ssm_norm_g, w_branch_a, w_branch_b, w_out, g_ffn, w_router, b_router, w_up, b_up, w_down, b_down, g_ple, w_ple_gate, w_ple_up, g_final):
    x_all = jnp.concatenate([x_prompt.reshape(NP_TOK, D_MODEL), x_sample.reshape(NS_TOK, D_MODEL)], axis=0)
    p_all = jnp.concatenate([p_prompt[0].reshape(NP_TOK, D_PLE), p_sample[0].reshape(NS_TOK, D_PLE)], axis=0)
    w_in0 = w_in[0]
    g_mix2 = g_mix[0].reshape(1, D_MODEL)

    proj = _rms_matmul(x_all, g_mix2, w_in0, 0, N_MAIN, 1024, 512, "in_proj")
    dt_raw = _rms_matmul(x_all, g_mix2, w_in0[:, COL_DT:COL_GATE], 0, SSM_HEADS, 1024, SSM_HEADS, "in_proj_dt")
    gates = _rms_matmul(x_all, g_mix2, w_in0[:, COL_GATE:], 0, 2 * D_MODEL, 1024, 512, "in_proj_gates")

    ya_p, ua_tail = _conv_a_prompt(proj, conv_a_w[0])
    ya_s, ua_s = _conv_a_sample(proj, conv_a_w[0], _pad_prev(state_conv_a[0], CONV_A_W))
    conv_a_prompt = ua_tail[:, SUBLANES - (CONV_A_W - 1):, :]
    conv_a_sample = ua_s.reshape(DEC_BATCH, DEC_SEQ, D_CONV)[:, DEC_SEQ - (CONV_A_W - 1):, :]

    conv_b2 = ssm_conv_b[0].reshape(1, D_XBC)
    xbc_p = _conv_m_prompt(proj, ssm_conv_w[0], conv_b2)
    xbc_s = _conv_m_sample(proj, ssm_conv_w[0], conv_b2, _pad_prev(state_conv_ssm[0], SSM_CONV_W))
    proj3 = proj.reshape(N_TOK // DEC_SEQ, DEC_SEQ, N_MAIN)
    t0 = DEC_SEQ - (SSM_CONV_W - 1)
    conv_m_prompt = proj3[SEQ // DEC_SEQ - 1:NP_TOK // DEC_SEQ:SEQ // DEC_SEQ, t0:, COL_XBC:COL_XBC + D_XBC]
    conv_m_sample = proj3[NP_TOK // DEC_SEQ:, t0:, COL_XBC:COL_XBC + D_XBC]

    ssd_w = (dt_bias[0].reshape(1, SSM_HEADS), dt_bias[0].reshape(SSM_HEADS, 1),
             a_log[0].reshape(1, SSM_HEADS), a_log[0].reshape(SSM_HEADS, 1),
             jnp.repeat(d_skip[0], SSM_HEAD_DIM).reshape(1, D_INNER), ssm_norm_g[0].reshape(1, D_INNER))
    ym_p, ssm_p = _ssd_call(xbc_p, proj, dt_raw, *ssd_w)
    ym_s, ssm_s = _ssd_step_call(xbc_s, proj, dt_raw, *ssd_w, state_ssm)
    ssm_shape = (SSM_HEADS, SSM_HEAD_DIM, SSM_STATE)

    ya = jnp.concatenate([ya_p, ya_s], axis=0)
    ym = jnp.concatenate([ym_p, ym_s.astype(BF16)], axis=0)
    merged = _merge(ya, ym, w_branch_a[0], w_branch_b[0], gates)
    h = _resid_matmul(merged, w_out[0], x_all)

    wr_pad = jnp.pad(w_router[0], ((0, 0), (0, LANES - N_EXPERTS)))
    br_pad = jnp.pad(b_router[0].reshape(1, N_EXPERTS), ((0, 0), (0, LANES - N_EXPERTS)), constant_values=-1e30)
    g_ffn2 = g_ffn[0].reshape(1, D_MODEL)
    eidx, gate_w, rank, cnt = _router(h, g_ffn2, wr_pad, br_pad)
    counts = cnt[0, :N_EXPERTS].astype(I32)
    dest, sup_e, sup_row, sup_n = _routing_tables(counts, eidx[:, :TOP_K], rank[:, :TOP_K])
    xs_rows = _dispatch(dest, h, g_ffn2)
    yb_rows = _experts(sup_e, sup_row, sup_n, xs_rows, w_up[0], b_up[0], w_down[0], b_down[0])

    y_all = _combine(dest, yb_rows, h, gate_w, g_ple[0].reshape(1, D_MODEL), w_ple_gate[0], p_all,
                     w_ple_up[0], g_final.reshape(1, D_MODEL))
    y_prompt = y_all[:NP_TOK].reshape(BATCH, SEQ, D_MODEL)
    y_sample = y_all[NP_TOK:].reshape(DEC_BATCH, DEC_SEQ, D_MODEL)
    return (y_prompt, y_sample,
            conv_a_prompt[None], conv_m_prompt[None], ssm_p.reshape((1, BATCH) + ssm_shape),
            conv_a_sample[None], conv_m_sample[None], ssm_s)
```

```python
import functools

import jax
import jax.numpy as jnp
from jax import lax
from jax.experimental import pallas as pl
from jax.experimental.pallas import tpu as pltpu

F32 = jnp.float32
BF16 = jnp.bfloat16
I32 = jnp.int32
HI = lax.Precision.HIGHEST

D_MODEL = 2048
BATCH = 4
SEQ = 2048
DEC_BATCH = 128
DEC_SEQ = 8
D_PLE = 256
D_CONV = D_MODEL
CONV_A_W = 3
D_INNER = 2 * D_MODEL
SSM_HEAD_DIM = 64
SSM_HEADS = D_INNER // SSM_HEAD_DIM
SSM_GROUPS = 8
SSM_STATE = 128
SSM_CONV_W = 4
SSM_CHUNK = 128
D_XBC = D_INNER + 2 * SSM_GROUPS * SSM_STATE
N_EXPERTS = 32
TOP_K = 4
D_FF = D_MODEL
SWIGLU_LIMIT = 7.0
SWIGLU_ALPHA = 1.702
EPS = 1e-6

NP_TOK = BATCH * SEQ
NS_TOK = DEC_BATCH * DEC_SEQ
N_TOK = NP_TOK + NS_TOK
N_ASSIGN = N_TOK * TOP_K

LANES = 128
SUBLANES = 8
ROW_TILE = D_MODEL // LANES

COL_AX = 0
COL_AB = D_CONV
COL_AC = 2 * D_CONV
COL_Z = 3 * D_CONV
COL_XBC = COL_Z + D_INNER
COL_DT = COL_XBC + D_XBC
COL_GATE = COL_DT + SSM_HEADS
N_MAIN = COL_DT

MOE_SUB = 256
MOE_TS = 1280
MOE_NSB = MOE_TS // MOE_SUB
MOE_CHUNK = MOE_TS // 2
MOE_ROWS = N_ASSIGN + N_EXPERTS * MOE_SUB
MOE_SMAX = N_ASSIGN // MOE_TS + N_EXPERTS
MOE_TN = 512
MOE_J1 = D_FF // MOE_TN
MOE_J2 = D_MODEL // MOE_TN

MIB = 1024 * 1024


def _cparams(n_grid, vmem_mib):
    return pltpu.CompilerParams(dimension_semantics=("arbitrary",) * n_grid,
                                vmem_limit_bytes=vmem_mib * MIB)


def _sigmoid(x):
    return 1.0 / (1.0 + jnp.exp(-x))


def _softplus(x):
    return jnp.maximum(x, 0.0) + jnp.log1p(jnp.exp(-jnp.abs(x)))


def _group_specs(tm, width, n_grid):
    npt = NP_TOK // tm
    if n_grid == 1:
        return (pl.BlockSpec((tm, width), lambda i: (jnp.minimum(i, npt - 1), 0)),
                pl.BlockSpec((tm, width), lambda i: (jnp.maximum(i - npt, 0), 0)))
    return (pl.BlockSpec((tm, width), lambda i, j, *_: (jnp.minimum(i, npt - 1), 0)),
            pl.BlockSpec((tm, width), lambda i, j, *_: (jnp.maximum(i - npt, 0), 0)))


def _rms_mm_body(npt, xp_ref, xs_ref, g_ref, w_ref, o_ref, n_ref):
    i = pl.program_id(0)
    first = pl.program_id(1) == 0

    def norm(x_ref):
        x = x_ref[...]
        ms = jnp.mean(x * x, axis=-1, keepdims=True)
        n_ref[...] = (x * lax.rsqrt(ms + EPS) * g_ref[...]).astype(BF16)

    @pl.when(jnp.logical_and(first, i < npt))
    def _():
        norm(xp_ref)

    @pl.when(jnp.logical_and(first, i >= npt))
    def _():
        norm(xs_ref)

    o_ref[...] = jnp.dot(n_ref[...], w_ref[...].astype(BF16), preferred_element_type=F32)


def _rms_matmul(xp, xs, g, w, col0_blocks, ncols, tm, tn, name):
    k = xp.shape[1]
    xp_spec, xs_spec = _group_specs(tm, k, 2)
    return pl.pallas_call(
        functools.partial(_rms_mm_body, NP_TOK // tm),
        grid=(N_TOK // tm, ncols // tn),
        in_specs=[xp_spec, xs_spec,
                  pl.BlockSpec((1, k), lambda i, j: (0, 0)),
                  pl.BlockSpec((None, k, tn), lambda i, j: (0, 0, col0_blocks + j))],
        out_specs=pl.BlockSpec((tm, tn), lambda i, j: (i, j)),
        out_shape=jax.ShapeDtypeStruct((N_TOK, ncols), F32),
        scratch_shapes=[pltpu.VMEM((tm, k), BF16)],
        compiler_params=_cparams(2, 56),
        name=name,
    )(xp, xs, g, w)


def _causal_taps(u, w_ref, width, rowmod, prev):
    rows = u.shape[0]
    acc = u * w_ref[width - 1:width, :]
    for s in range(1, width):
        shifted = pltpu.roll(u, s, 0)
        if prev is None:
            fill = jnp.zeros_like(u)
        else:
            fill = pltpu.roll(prev, rows - SUBLANES + s, 0)
        acc = acc + jnp.where(rowmod >= s, shifted, fill) * w_ref[width - 1 - s:width - s, :]
    return acc


def _row_in_seq(shape, period):
    r = lax.broadcasted_iota(I32, shape, 0)
    return r if period >= shape[0] else r & (period - 1)


def _conv_a_body(period, has_prev, *refs):
    if has_prev:
        ax_ref, ab_ref, ac_ref, w_ref, prev_ref, ya_ref, u_ref = refs
        prev = prev_ref[...]
    else:
        ax_ref, ab_ref, ac_ref, w_ref, ya_ref, u_ref = refs
        prev = None
    u = ac_ref[...] * ax_ref[...]
    v = _causal_taps(u, w_ref, CONV_A_W, _row_in_seq(u.shape, period), prev)
    ya_ref[...] = (ab_ref[...] * v).astype(BF16)
    if has_prev:
        u_ref[...] = u
    else:
        u_ref[...] = u[u.shape[0] - SUBLANES:, :]


def _conv_m_body(period, has_prev, *refs):
    if has_prev:
        x_ref, w_ref, b_ref, prev_ref, o_ref = refs
        prev = prev_ref[...]
    else:
        x_ref, w_ref, b_ref, o_ref = refs
        prev = None
    u = x_ref[...]
    v = _causal_taps(u, w_ref, SSM_CONV_W, _row_in_seq(u.shape, period), prev) + b_ref[...]
    o_ref[...] = v * _sigmoid(v)


CONV_TC = 256


def _conv_a_prompt(proj, conv_w):
    nb = D_CONV // CONV_TC
    blk = lambda off: pl.BlockSpec((SEQ, CONV_TC), lambda b, c, off=off: (b, off // CONV_TC + c))
    return pl.pallas_call(
        functools.partial(_conv_a_body, SEQ, False),
        grid=(BATCH, nb),
        in_specs=[blk(COL_AX), blk(COL_AB), blk(COL_AC),
                  pl.BlockSpec((CONV_A_W, CONV_TC), lambda b, c: (0, c))],
        out_specs=[pl.BlockSpec((SEQ, CONV_TC), lambda b, c: (b, c)),
                   pl.BlockSpec((None, SUBLANES, CONV_TC), lambda b, c: (b, 0, c))],
        out_shape=[jax.ShapeDtypeStruct((NP_TOK, D_CONV), BF16),
                   jax.ShapeDtypeStruct((BATCH, SUBLANES, D_CONV), F32)],
        compiler_params=_cparams(2, 48),
        name="conv_a_prompt",
    )(proj, proj, proj, conv_w)


def _conv_a_sample(proj, conv_w, prev_pad):
    nb = D_CONV // CONV_TC
    rb = NP_TOK // NS_TOK
    blk = lambda off: pl.BlockSpec((NS_TOK, CONV_TC), lambda c, off=off: (rb, off // CONV_TC + c))
    own = pl.BlockSpec((NS_TOK, CONV_TC), lambda c: (0, c))
    return pl.pallas_call(
        functools.partial(_conv_a_body, DEC_SEQ, True),
        grid=(nb,),
        in_specs=[blk(COL_AX), blk(COL_AB), blk(COL_AC),
                  pl.BlockSpec((CONV_A_W, CONV_TC), lambda c: (0, c)), own],
        out_specs=[own, own],
        out_shape=[jax.ShapeDtypeStruct((NS_TOK, D_CONV), BF16),
                   jax.ShapeDtypeStruct((NS_TOK, D_CONV), F32)],
        compiler_params=_cparams(1, 48),
        name="conv_a_sample",
    )(proj, proj, proj, conv_w, prev_pad)


def _conv_m_prompt(proj, conv_w, conv_b):
    nb = D_XBC // CONV_TC
    return pl.pallas_call(
        functools.partial(_conv_m_body, SEQ, False),
        grid=(BATCH, nb),
        in_specs=[pl.BlockSpec((SEQ, CONV_TC), lambda b, c: (b, COL_XBC // CONV_TC + c)),
                  pl.BlockSpec((SSM_CONV_W, CONV_TC), lambda b, c: (0, c)),
                  pl.BlockSpec((1, CONV_TC), lambda b, c: (0, c))],
        out_specs=pl.BlockSpec((SEQ, CONV_TC), lambda b, c: (b, c)),
        out_shape=jax.ShapeDtypeStruct((NP_TOK, D_XBC), F32),
        compiler_params=_cparams(2, 48),
        name="conv_m_prompt",
    )(proj, conv_w, conv_b)


def _conv_m_sample(proj, conv_w, conv_b, prev_pad):
    nb = D_XBC // CONV_TC
    rb = NP_TOK // NS_TOK
    own = pl.BlockSpec((NS_TOK, CONV_TC), lambda c: (0, c))
    return pl.pallas_call(
        functools.partial(_conv_m_body, DEC_SEQ, True),
        grid=(nb,),
        in_specs=[pl.BlockSpec((NS_TOK, CONV_TC), lambda c: (rb, COL_XBC // CONV_TC + c)),
                  pl.BlockSpec((SSM_CONV_W, CONV_TC), lambda c: (0, c)),
                  pl.BlockSpec((1, CONV_TC), lambda c: (0, c)), own],
        out_specs=own,
        out_shape=jax.ShapeDtypeStruct((NS_TOK, D_XBC), F32),
        compiler_params=_cparams(1, 48),
        name="conv_m_sample",
    )(proj, conv_w, conv_b, prev_pad)


def _ssd_body(xbc_ref, zlo_ref, zhi_ref, dtr_ref, dtb_ref, dtbc_ref, alog_ref, alogc_ref, dsk_ref, ng_ref,
              ym_ref, hl_ref, ht_ref, y_ref):
    L = SSM_CHUNK
    H = SSM_HEADS
    c = pl.program_id(1)

    @pl.when(c == 0)
    def _():
        ht_ref[...] = jnp.zeros_like(ht_ref)

    dtr = dtr_ref[...]
    dt = _softplus(dtr + dtb_ref[...])
    eye = (lax.broadcasted_iota(I32, (H, H), 0) == lax.broadcasted_iota(I32, (H, H), 1)).astype(F32)
    dtr_t = lax.dot_general(eye, dtr, (((1,), (1,)), ((), ())), precision=HI,
                            preferred_element_type=F32)
    dt_t = _softplus(dtr_t + dtbc_ref[...])
    adt = dt * (-jnp.exp(alog_ref[...]))
    adt_t = dt_t * (-jnp.exp(alogc_ref[...]))
    li = lax.broadcasted_iota(I32, (L, L), 0)
    si = lax.broadcasted_iota(I32, (L, L), 1)
    causal = li >= si
    a_cum = jnp.dot(causal.astype(F32), adt, precision=HI, preferred_element_type=F32)
    a_cum_t = jnp.dot(adt_t, (li <= si).astype(F32), precision=HI, preferred_element_type=F32)
    ea = jnp.exp(a_cum)
    a_tot_t = a_cum_t[:, L - 1:L]
    dte_t = jnp.exp(a_tot_t - a_cum_t)
    ea_tot_t = jnp.exp(a_tot_t)

    lane = lax.broadcasted_iota(I32, (L, LANES), 1)
    lo = lane < SSM_HEAD_DIM
    lane1 = lax.broadcasted_iota(I32, (1, LANES), 1) < SSM_HEAD_DIM
    gn = SSM_GROUPS * SSM_STATE
    for g in range(SSM_GROUPS):
        bg = xbc_ref[:, D_INNER + g * SSM_STATE:D_INNER + (g + 1) * SSM_STATE]
        cg = xbc_ref[:, D_INNER + gn + g * SSM_STATE:D_INNER + gn + (g + 1) * SSM_STATE]
        bg_t = bg.T
        cb = jnp.dot(cg.astype(BF16), bg_t.astype(BF16), preferred_element_type=F32)
        for jj in range(SSM_HEADS // SSM_GROUPS // 2):
            j = g * (SSM_HEADS // SSM_GROUPS // 2) + jj
            cols = slice(j * LANES, (j + 1) * LANES)
            lhs1, lhs2 = [], []
            for h in (2 * j, 2 * j + 1):
                diff = a_cum[:, h:h + 1] - a_cum_t[h:h + 1, :]
                decay = jnp.exp(jnp.where(causal, diff, -1e30))
                lhs1.append((cb * decay).astype(BF16))
            for h in (2 * j, 2 * j + 1):
                lhs1.append((cg * ea[:, h:h + 1]).astype(BF16))
                lhs2.append((bg_t * dte_t[h:h + 1, :]).astype(BF16))
            dt_pair = jnp.where(lo, dt[:, 2 * j:2 * j + 1], dt[:, 2 * j + 1:2 * j + 2])
            xdt = xbc_ref[:, cols] * dt_pair
            zero = jnp.zeros_like(xdt)
            x_top = jnp.where(lo, xdt, zero).astype(BF16)
            x_bot = jnp.where(lo, zero, xdt).astype(BF16)
            ht = ht_ref[:, cols]
            h_top = jnp.where(lo, ht, zero).astype(BF16)
            h_bot = jnp.where(lo, zero, ht).astype(BF16)
            y_ref[:, cols] = jnp.dot(jnp.concatenate(lhs1, axis=1),
                                     jnp.concatenate([x_top, x_bot, h_top, h_bot], axis=0),
                                     preferred_element_type=F32)
            upd = jnp.dot(jnp.concatenate(lhs2, axis=1), jnp.concatenate([x_top, x_bot], axis=0),
                          preferred_element_type=F32)
            keep = jnp.where(lane1, ea_tot_t[2 * j:2 * j + 1, :], ea_tot_t[2 * j + 1:2 * j + 2, :])
            ht_ref[:, cols] = ht * keep + upd

    gw = D_INNER // SSM_GROUPS
    for g in range(SSM_GROUPS):
        cols = slice(g * gw, (g + 1) * gw)
        zr = zlo_ref if g < SSM_GROUPS // 2 else zhi_ref
        zc = slice((g % (SSM_GROUPS // 2)) * gw, (g % (SSM_GROUPS // 2) + 1) * gw)
        z = zr[:, zc]
        y = y_ref[:, cols] + dsk_ref[:, cols] * xbc_ref[:, cols]
        y = y * (z * _sigmoid(z))
        y = y * lax.rsqrt(jnp.mean(y * y, axis=-1, keepdims=True) + EPS)
        ym_ref[:, cols] = (y * ng_ref[:, cols]).astype(ym_ref.dtype)

    @pl.when(c == pl.num_programs(1) - 1)
    def _():
        hl_ref[...] = ht_ref[...].T


def _ssd_call(xbc, proj, dt_raw, dtb, dtbc, alog, alogc, dsk, ng):
    L = SSM_CHUNK
    n_chunk = SEQ // L
    hp = SSM_HEADS * SSM_HEAD_DIM
    zb = D_INNER // 2
    vec = lambda shape: pl.BlockSpec(shape, lambda b, c: (0, 0))
    return pl.pallas_call(
        _ssd_body,
        grid=(BATCH, n_chunk),
        in_specs=[pl.BlockSpec((L, D_XBC), lambda b, c: (b * n_chunk + c, 0)),
                  pl.BlockSpec((L, zb), lambda b, c: (b * n_chunk + c, COL_Z // zb)),
                  pl.BlockSpec((L, zb), lambda b, c: (b * n_chunk + c, COL_Z // zb + 1)),
                  pl.BlockSpec((L, SSM_HEADS), lambda b, c: (b * n_chunk + c, 0)),
                  vec((1, SSM_HEADS)), vec((SSM_HEADS, 1)), vec((1, SSM_HEADS)), vec((SSM_HEADS, 1)),
                  vec((1, D_INNER)), vec((1, D_INNER))],
        out_specs=[pl.BlockSpec((L, D_INNER), lambda b, c: (b * n_chunk + c, 0)),
                   pl.BlockSpec((None, hp, SSM_STATE), lambda b, c: (b, 0, 0))],
        out_shape=[jax.ShapeDtypeStruct((NP_TOK, D_INNER), BF16),
                   jax.ShapeDtypeStruct((BATCH, hp, SSM_STATE), F32)],
        scratch_shapes=[pltpu.VMEM((SSM_STATE, hp), F32), pltpu.VMEM((L, hp), F32)],
        compiler_params=_cparams(2, 48),
        name="ssd_prompt",
    )(xbc, proj, proj, dt_raw, dtb, dtbc, alog, alogc, dsk, ng)


def _ssd_step_body(xbc_in, zlo_ref, zhi_ref, dtr_in, dtb_ref, dtbc_ref, alog_ref, alogc_ref, dsk_ref, ng_ref,
                   h0_ref, ym_ref, hl_ref, xp_ref, dp_ref):
    T = DEC_SEQ
    L = SSM_CHUNK
    H = SSM_HEADS
    P = SSM_HEAD_DIM

    @pl.when(pl.program_id(0) == 0)
    def _():
        xp_ref[...] = jnp.zeros_like(xp_ref)
        dp_ref[...] = jnp.zeros_like(dp_ref)

    xp_ref[0:T, :] = xbc_in[...]
    dp_ref[0:T, :] = dtr_in[...]

    dt8 = _softplus(dtr_in[...] + dtb_ref[...])
    a_cum8 = dt8 * (-jnp.exp(alog_ref[...]))
    row8 = lax.broadcasted_iota(I32, (T, H), 0)
    k = 1
    while k < T:
        a_cum8 = a_cum8 + jnp.where(row8 >= k, pltpu.roll(a_cum8, k, 0), 0.0)
        k *= 2
    ea8 = jnp.exp(a_cum8)
    dte8 = jnp.exp(a_cum8[T - 1:T, :] - a_cum8)
    eye = (lax.broadcasted_iota(I32, (H, H), 0) == lax.broadcasted_iota(I32, (H, H), 1)).astype(F32)
    dtr_t = lax.dot_general(eye, dp_ref[...], (((1,), (1,)), ((), ())), precision=HI,
                            preferred_element_type=F32)
    dt_t = jnp.where(lax.broadcasted_iota(I32, (H, L), 1) < T, _softplus(dtr_t + dtbc_ref[...]), 0.0)
    adt_t = dt_t * (-jnp.exp(alogc_ref[...]))
    upper = (lax.broadcasted_iota(I32, (L, L), 0) <= lax.broadcasted_iota(I32, (L, L), 1)).astype(F32)
    a_cum_t = jnp.dot(adt_t, upper, precision=HI, preferred_element_type=F32)
    ea_tot_t = jnp.exp(a_cum_t[:, L - 1:L])

    causal8 = lax.broadcasted_iota(I32, (T, L), 0) >= lax.broadcasted_iota(I32, (T, L), 1)
    lo8 = lax.broadcasted_iota(I32, (T, LANES), 1) < P
    zeros_t = jnp.zeros((T, LANES), F32)
    zeros_pad = jnp.zeros((L - T, LANES), F32)
    gn = SSM_GROUPS * SSM_STATE
    gw = D_INNER // SSM_GROUPS
    pairs = SSM_HEADS // SSM_GROUPS // 2
    for g in range(SSM_GROUPS):
        bgp = xp_ref[:, D_INNER + g * SSM_STATE:D_INNER + (g + 1) * SSM_STATE].astype(BF16)
        cg8 = xbc_in[:, D_INNER + gn + g * SSM_STATE:D_INNER + gn + (g + 1) * SSM_STATE]
        cg16 = jnp.concatenate([cg8, zeros_t], axis=0).astype(BF16)
        cb = lax.dot_general(cg16, bgp, (((1,), (1,)), ((), ())), preferred_element_type=F32)[0:T]
        ys = []
        for jj in range(pairs):
            j = g * pairs + jj
            ha, hb = 2 * j, 2 * j + 1
            cols = slice(j * LANES, (j + 1) * LANES)

            def decay(h):
                diff = a_cum8[:, h:h + 1] - a_cum_t[h:h + 1, :]
                return jnp.exp(jnp.where(causal8, diff, -1e30))

            m = jnp.concatenate([cb * decay(ha), cb * decay(hb)], axis=0).astype(BF16)
            cd = jnp.concatenate([cg8 * ea8[:, ha:ha + 1], cg8 * ea8[:, hb:hb + 1]], axis=0).astype(BF16)
            xdt8 = xbc_in[:, cols] * jnp.where(lo8, dt8[:, ha:ha + 1], dt8[:, hb:hb + 1])
            xd8 = xdt8 * jnp.where(lo8, dte8[:, ha:ha + 1], dte8[:, hb:hb + 1])
            xdt_pad = jnp.concatenate([xdt8, zeros_pad], axis=0).astype(BF16)
            xd_pad = jnp.concatenate([xd8, zeros_pad], axis=0).astype(BF16)
            hp = h0_ref[ha:hb + 1].reshape(2 * P, SSM_STATE)
            out = (jnp.dot(m, xdt_pad, preferred_element_type=F32)
                   + lax.dot_general(cd, hp.astype(BF16), (((1,), (1,)), ((), ())),
                                     preferred_element_type=F32))
            ys.append(jnp.where(lo8, out[0:T], out[T:2 * T]))
            upd = lax.dot_general(xd_pad, bgp, (((0,), (0,)), ((), ())), preferred_element_type=F32)
            keep = jnp.concatenate([jnp.broadcast_to(ea_tot_t[ha:ha + 1, :], (P, SSM_STATE)),
                                    jnp.broadcast_to(ea_tot_t[hb:hb + 1, :], (P, SSM_STATE))], axis=0)
            hl_ref[ha:hb + 1] = (hp * keep + upd).reshape(2, P, SSM_STATE)

        cols = slice(g * gw, (g + 1) * gw)
        zr = zlo_ref if g < SSM_GROUPS // 2 else zhi_ref
        zc = slice((g % (SSM_GROUPS // 2)) * gw, (g % (SSM_GROUPS // 2) + 1) * gw)
        z = zr[:, zc]
        y = jnp.concatenate(ys, axis=1) + dsk_ref[:, cols] * xbc_in[:, cols]
        y = y * (z * _sigmoid(z))
        y = y * lax.rsqrt(jnp.mean(y * y, axis=-1, keepdims=True) + EPS)
        ym_ref[:, cols] = y * ng_ref[:, cols]


def _ssd_step_call(xbc, proj, dt_raw, dtb, dtbc, alog, alogc, dsk, ng, state):
    T = DEC_SEQ
    rb0 = NP_TOK // T
    zb = D_INNER // 2
    st = pl.BlockSpec((None, None, SSM_HEADS, SSM_HEAD_DIM, SSM_STATE), lambda b: (0, b, 0, 0, 0))
    vec = lambda shape: pl.BlockSpec(shape, lambda b: (0, 0))
    return pl.pallas_call(
        _ssd_step_body,
        grid=(DEC_BATCH,),
        in_specs=[pl.BlockSpec((T, D_XBC), lambda b: (b, 0)),
                  pl.BlockSpec((T, zb), lambda b: (rb0 + b, COL_Z // zb)),
                  pl.BlockSpec((T, zb), lambda b: (rb0 + b, COL_Z // zb + 1)),
                  pl.BlockSpec((T, SSM_HEADS), lambda b: (rb0 + b, 0)),
                  vec((1, SSM_HEADS)), vec((SSM_HEADS, 1)), vec((1, SSM_HEADS)), vec((SSM_HEADS, 1)),
                  vec((1, D_INNER)), vec((1, D_INNER)), st],
        out_specs=[pl.BlockSpec((T, D_INNER), lambda b: (b, 0)), st],
        out_shape=[jax.ShapeDtypeStruct((NS_TOK, D_INNER), F32),
                   jax.ShapeDtypeStruct(state.shape, F32)],
        scratch_shapes=[pltpu.VMEM((SSM_CHUNK, D_XBC), F32), pltpu.VMEM((SSM_CHUNK, SSM_HEADS), F32)],
        compiler_params=_cparams(1, 32),
        name="ssd_sample",
    )(xbc, proj, proj, dt_raw, dtb, dtbc, alog, alogc, dsk, ng, state)


def _merge_body(ya_ref, ym_ref, wa_ref, wb_ref, ga_ref, gb_ref, o_ref):
    a = jnp.dot(ya_ref[...], wa_ref[...].astype(BF16), preferred_element_type=F32)
    b = jnp.dot(ym_ref[...], wb_ref[...].astype(BF16), preferred_element_type=F32)
    o_ref[...] = (_sigmoid(ga_ref[...]) * a + _sigmoid(gb_ref[...]) * b).astype(BF16)


def _merge(ya, ym, wa, wb, gates):
    tm, tn = 1024, 256
    return pl.pallas_call(
        _merge_body,
        grid=(N_TOK // tm, D_MODEL // tn),
        in_specs=[pl.BlockSpec((tm, D_CONV), lambda i, j: (i, 0)),
                  pl.BlockSpec((tm, D_INNER), lambda i, j: (i, 0)),
                  pl.BlockSpec((D_CONV, tn), lambda i, j: (0, j)),
                  pl.BlockSpec((D_INNER, tn), lambda i, j: (0, j)),
                  pl.BlockSpec((tm, tn), lambda i, j: (i, j)),
                  pl.BlockSpec((tm, tn), lambda i, j: (i, D_MODEL // tn + j))],
        out_specs=pl.BlockSpec((tm, tn), lambda i, j: (i, j)),
        out_shape=jax.ShapeDtypeStruct((N_TOK, D_MODEL), BF16),
        compiler_params=_cparams(2, 52),
        name="merge",
    )(ya, ym, wa, wb, gates, gates)


def _resid_body(npt, m_ref, w_ref, xp_ref, xs_ref, o_ref):
    x = jnp.where(pl.program_id(0) < npt, xp_ref[...], xs_ref[...])
    o_ref[...] = x + jnp.dot(m_ref[...], w_ref[...].astype(BF16), preferred_element_type=F32)


def _resid_matmul(m, w, xp, xs):
    tm, tn = 1024, 512
    npt = NP_TOK // tm
    return pl.pallas_call(
        functools.partial(_resid_body, npt),
        grid=(N_TOK // tm, D_MODEL // tn),
        in_specs=[pl.BlockSpec((tm, D_MODEL), lambda i, j: (i, 0)),
                  pl.BlockSpec((D_MODEL, tn), lambda i, j: (0, j)),
                  pl.BlockSpec((tm, tn), lambda i, j: (jnp.minimum(i, npt - 1), j)),
                  pl.BlockSpec((tm, tn), lambda i, j: (jnp.maximum(i - npt, 0), j))],
        out_specs=pl.BlockSpec((tm, tn), lambda i, j: (i, j)),
        out_shape=jax.ShapeDtypeStruct((N_TOK, D_MODEL), F32),
        compiler_params=_cparams(2, 48),
        name="out_proj",
    )(m, w, xp, xs)


ROUTER_TM = 512


def _router_body(h_ref, g_ref, wr_ref, br_ref, ei_ref, gt_ref, rk_ref, cnt_ref, carry_ref):
    tm = ROUTER_TM

    @pl.when(pl.program_id(0) == 0)
    def _():
        carry_ref[...] = jnp.zeros_like(carry_ref)

    h = h_ref[...]
    n2 = h * lax.rsqrt(jnp.mean(h * h, axis=-1, keepdims=True) + EPS) * g_ref[...]
    logits = jnp.dot(n2, wr_ref[...], precision=HI, preferred_element_type=F32) + br_ref[...]

    lane = lax.broadcasted_iota(I32, (tm, LANES), 1).astype(F32)
    work = logits
    vals, idxs = [], []
    for _ in range(TOP_K):
        m = jnp.max(work, axis=-1, keepdims=True)
        idx = jnp.min(jnp.where(work == m, lane, float(LANES)), axis=-1, keepdims=True)
        vals.append(m)
        idxs.append(idx)
        work = jnp.where(lane == idx, -jnp.inf, work)
    ex = [jnp.exp(v - vals[0]) for v in vals]
    den = ex[0] + ex[1] + ex[2] + ex[3]
    sel = jnp.zeros((tm, LANES), F32)
    for idx in idxs:
        sel = sel + jnp.where(lane == idx, 1.0, 0.0)
    before = (lax.broadcasted_iota(I32, (tm, tm), 0) > lax.broadcasted_iota(I32, (tm, tm), 1))
    rank_e = jnp.dot(before.astype(BF16), sel.astype(BF16), preferred_element_type=F32) + carry_ref[0:1, :]
    carry_ref[...] = carry_ref[...] + jnp.sum(sel, axis=0, keepdims=True)
    ei = jnp.zeros((tm, LANES), F32)
    gt = jnp.zeros((tm, LANES), F32)
    rk = jnp.zeros((tm, LANES), F32)
    for k in range(TOP_K):
        rank_k = jnp.sum(jnp.where(lane == idxs[k], rank_e, 0.0), axis=-1, keepdims=True)
        here = lane == float(k)
        ei = jnp.where(here, idxs[k], ei)
        gt = jnp.where(here, ex[k] / den, gt)
        rk = jnp.where(here, rank_k, rk)
    ei_ref[...] = ei.astype(I32)
    gt_ref[...] = gt
    rk_ref[...] = rk.astype(I32)
    cnt_ref[...] = carry_ref[...]


def _router(h, g, wr_pad, br_pad):
    tm = ROUTER_TM
    tok = pl.BlockSpec((tm, LANES), lambda i: (i, 0))
    return pl.pallas_call(
        _router_body,
        grid=(N_TOK // tm,),
        in_specs=[pl.BlockSpec((tm, D_MODEL), lambda i: (i, 0)),
                  pl.BlockSpec((1, D_MODEL), lambda i: (0, 0)),
                  pl.BlockSpec((D_MODEL, LANES), lambda i: (0, 0)),
                  pl.BlockSpec((1, LANES), lambda i: (0, 0))],
        out_specs=[tok, tok, tok, pl.BlockSpec((SUBLANES, LANES), lambda i: (0, 0))],
        out_shape=[jax.ShapeDtypeStruct((N_TOK, LANES), I32),
                   jax.ShapeDtypeStruct((N_TOK, LANES), F32),
                   jax.ShapeDtypeStruct((N_TOK, LANES), I32),
                   jax.ShapeDtypeStruct((SUBLANES, LANES), F32)],
        scratch_shapes=[pltpu.VMEM((SUBLANES, LANES), F32)],
        compiler_params=_cparams(1, 48),
        name="router",
    )(h, g, wr_pad, br_pad)


DISPATCH_TOK = 512


def _row_slab(ref, row):
    start = row * ROW_TILE
    if not isinstance(start, int):
        start = pl.multiple_of(start, ROW_TILE)
    return ref.at[pl.ds(start, ROW_TILE)]


def _row_copy(src, src_row, dst, dst_row, sem):
    return pltpu.make_async_copy(_row_slab(src, src_row), _row_slab(dst, dst_row), sem)


def _dispatch_body(dest_ref, h_ref, g_ref, zero_hbm, xs_hbm, rows_ref, sem):
    del zero_hbm
    tm = DISPATCH_TOK
    base = pl.program_id(0) * tm
    h = h_ref[...]
    n2 = h * lax.rsqrt(jnp.mean(h * h, axis=-1, keepdims=True) + EPS) * g_ref[...]
    for cidx in range(ROW_TILE):
        rows_ref[pl.ds(cidx, tm, stride=ROW_TILE), :] = n2[:, cidx * LANES:(cidx + 1) * LANES]

    def issue(i, carry):
        for k in range(TOP_K):
            _row_copy(rows_ref, i, xs_hbm, dest_ref[(base + i) * TOP_K + k], sem).start()
        return carry

    lax.fori_loop(0, tm, issue, 0)

    def drain(i, carry):
        for k in range(TOP_K):
            _row_copy(rows_ref, 0, xs_hbm, 0, sem).wait()
        return carry

    lax.fori_loop(0, tm, drain, 0)


def _dispatch(dest_flat, h, g):
    tm = DISPATCH_TOK
    zeros = jnp.zeros((MOE_ROWS * ROW_TILE, LANES), F32)
    return pl.pallas_call(
        _dispatch_body,
        grid_spec=pltpu.PrefetchScalarGridSpec(
            num_scalar_prefetch=1,
            grid=(N_TOK // tm,),
            in_specs=[pl.BlockSpec((tm, D_MODEL), lambda i, d: (i, 0)),
                      pl.BlockSpec((1, D_MODEL), lambda i, d: (0, 0)),
                      pl.BlockSpec(memory_space=pl.ANY)],
            out_specs=pl.BlockSpec(memory_space=pl.ANY),
            scratch_shapes=[pltpu.VMEM((tm * ROW_TILE, LANES), F32), pltpu.SemaphoreType.DMA(())]),
        out_shape=jax.ShapeDtypeStruct((MOE_ROWS * ROW_TILE, LANES), F32),
        input_output_aliases={3: 0},
        compiler_params=_cparams(1, 32),
        name="moe_dispatch",
    )(dest_flat, h, g, zeros)


def _expert_body(e_ref, r_ref, n_ref, xs_hbm, wg_ref, wu_ref, bg_ref, bu_ref, wd_ref, bd_ref, yb_hbm,
                 stage_ref, xb_ref, act_ref, sem_in, sem_out):
    del e_ref
    s = pl.program_id(0)
    j = pl.program_id(1)
    nsub = n_ref[s]
    row0 = r_ref[s]
    sub_rows = MOE_SUB * ROW_TILE

    def in_copy(k):
        return pltpu.make_async_copy(
            xs_hbm.at[pl.ds(pl.multiple_of((row0 + k * MOE_SUB) * ROW_TILE, sub_rows), sub_rows)],
            stage_ref.at[pl.ds(k * sub_rows, sub_rows)], sem_in)

    def out_copy(k):
        return pltpu.make_async_copy(
            stage_ref.at[pl.ds(k * sub_rows, sub_rows)],
            yb_hbm.at[pl.ds(pl.multiple_of((row0 + k * MOE_SUB) * ROW_TILE, sub_rows), sub_rows)], sem_out)

    @pl.when(j == 0)
    def _():
        for k in range(MOE_NSB):
            @pl.when(k < nsub)
            def _():
                in_copy(k).start()
        for k in range(MOE_NSB):
            @pl.when(k < nsub)
            def _():
                in_copy(k).wait()
        for k in range(MOE_NSB):
            @pl.when(k < nsub)
            def _():
                for cidx in range(ROW_TILE):
                    xb_ref[k * MOE_SUB:(k + 1) * MOE_SUB, cidx * LANES:(cidx + 1) * LANES] = (
                        stage_ref[pl.ds(k * sub_rows + cidx, MOE_SUB, stride=ROW_TILE), :].astype(BF16))

    full = nsub == MOE_NSB

    @pl.when(j < MOE_J1)
    def _():
        wg = wg_ref[...].astype(BF16)
        wu = wu_ref[...].astype(BF16)

        def up_rows(r0, nrows):
            x = xb_ref[r0:r0 + nrows, :]
            gate = jnp.dot(x, wg, preferred_element_type=F32) + bg_ref[...]
            up = jnp.dot(x, wu, preferred_element_type=F32) + bu_ref[...]
            gate = jnp.minimum(gate, SWIGLU_LIMIT)
            up = jnp.clip(up, -SWIGLU_LIMIT, SWIGLU_LIMIT)
            a = gate * _sigmoid(SWIGLU_ALPHA * gate) * (up + 1.0)
            act_ref[jnp.minimum(j, MOE_J1 - 1), r0:r0 + nrows, :] = a.astype(BF16)

        @pl.when(full)
        def _():
            for r0 in range(0, MOE_TS, MOE_CHUNK):
                up_rows(r0, MOE_CHUNK)

        for k in range(MOE_NSB - 1):
            @pl.when(jnp.logical_and(jnp.logical_not(full), k < nsub))
            def _():
                up_rows(k * MOE_SUB, MOE_SUB)

    per = MOE_TN // LANES
    for jd in range(MOE_J2):
        @pl.when(j == MOE_J1 + jd)
        def _():
            wd = wd_ref[...].astype(BF16)

            def down_rows(r0, nrows):
                y = bd_ref[...] + jnp.dot(act_ref[0, r0:r0 + nrows, :], wd[0:MOE_TN, :],
                                          preferred_element_type=F32)
                for q in range(1, MOE_J1):
                    y = y + jnp.dot(act_ref[q, r0:r0 + nrows, :], wd[q * MOE_TN:(q + 1) * MOE_TN, :],
                                    preferred_element_type=F32)
                for cc in range(per):
                    stage_ref[pl.ds(r0 * ROW_TILE + jd * per + cc, nrows, stride=ROW_TILE), :] = (
                        y[:, cc * LANES:(cc + 1) * LANES])

            @pl.when(full)
            def _():
                for r0 in range(0, MOE_TS, MOE_CHUNK):
                    down_rows(r0, MOE_CHUNK)

            for k in range(MOE_NSB - 1):
                @pl.when(jnp.logical_and(jnp.logical_not(full), k < nsub))
                def _():
                    down_rows(k * MOE_SUB, MOE_SUB)

    @pl.when(j == MOE_J1 + MOE_J2 - 1)
    def _():
        for k in range(MOE_NSB):
            @pl.when(k < nsub)
            def _():
                out_copy(k).start()
        for k in range(MOE_NSB):
            @pl.when(k < nsub)
            def _():
                out_copy(k).wait()


def _experts(sup_e, sup_row, sup_n, xs_rows, w_up, b_up, w_down, b_down):
    def up_j(s, j, e, r, n):
        return jnp.where(n[s] > 0, jnp.minimum(j, MOE_J1 - 1), MOE_J1 - 1)

    def dn_j(s, j, e, r, n):
        return jnp.where(n[s] > 0, jnp.maximum(j - MOE_J1, 0), MOE_J2 - 1)

    b_up3 = b_up.reshape(N_EXPERTS, 1, 2 * D_FF)
    b_down3 = b_down.reshape(N_EXPERTS, 1, D_MODEL)
    return pl.pallas_call(
        _expert_body,
        grid_spec=pltpu.PrefetchScalarGridSpec(
            num_scalar_prefetch=3,
            grid=(MOE_SMAX, MOE_J1 + MOE_J2),
            in_specs=[
                pl.BlockSpec(memory_space=pl.ANY),
                pl.BlockSpec((None, D_MODEL, MOE_TN), lambda s, j, e, r, n: (e[s], 0, up_j(s, j, e, r, n))),
                pl.BlockSpec((None, D_MODEL, MOE_TN),
                             lambda s, j, e, r, n: (e[s], 0, MOE_J1 + up_j(s, j, e, r, n))),
                pl.BlockSpec((None, 1, MOE_TN), lambda s, j, e, r, n: (e[s], 0, up_j(s, j, e, r, n))),
                pl.BlockSpec((None, 1, MOE_TN), lambda s, j, e, r, n: (e[s], 0, MOE_J1 + up_j(s, j, e, r, n))),
                pl.BlockSpec((None, D_FF, MOE_TN), lambda s, j, e, r, n: (e[s], 0, dn_j(s, j, e, r, n))),
                pl.BlockSpec((None, 1, MOE_TN), lambda s, j, e, r, n: (e[s], 0, dn_j(s, j, e, r, n))),
            ],
            out_specs=pl.BlockSpec(memory_space=pl.ANY),
            scratch_shapes=[pltpu.VMEM((MOE_TS * ROW_TILE, LANES), F32),
                            pltpu.VMEM((MOE_TS, D_MODEL), BF16),
                            pltpu.VMEM((MOE_J1, MOE_TS, MOE_TN), BF16),
                            pltpu.SemaphoreType.DMA(()),
                            pltpu.SemaphoreType.DMA(())]),
        out_shape=jax.ShapeDtypeStruct((MOE_ROWS * ROW_TILE, LANES), F32),
        input_output_aliases={3: 0},
        compiler_params=_cparams(2, 58),
        name="moe_experts",
    )(sup_e, sup_row, sup_n, xs_rows, w_up, w_up, b_up3, b_up3, w_down, b_down3)


COMB_TM = 512
COMB_TN = 512
COMB_J = D_MODEL // COMB_TN


def _combine_body(dest_ref, yb_hbm, h_ref, gt_ref, gp_ref, wg_ref, p_ref, wu_ref, gf_ref, o_ref,
                  gath_ref, n3_ref, sem):
    tm = COMB_TM
    i = pl.program_id(0)
    j = pl.program_id(1)

    @pl.when(j == 0)
    def _():
        base = i * tm

        def issue(r, carry):
            for k in range(TOP_K):
                _row_copy(yb_hbm, dest_ref[(base + r) * TOP_K + k], gath_ref, k * tm + r, sem).start()
            return carry

        lax.fori_loop(0, tm, issue, 0)

        def drain(r, carry):
            for k in range(TOP_K):
                _row_copy(yb_hbm, 0, gath_ref, 0, sem).wait()
            return carry

        lax.fori_loop(0, tm, drain, 0)

        gt = gt_ref[...]
        ssq = jnp.zeros((tm, 1), F32)
        for cidx in range(ROW_TILE):
            cols = slice(cidx * LANES, (cidx + 1) * LANES)
            acc = h_ref[:, cols]
            for k in range(TOP_K):
                acc = acc + gt[:, k:k + 1] * gath_ref[pl.ds(k * tm * ROW_TILE + cidx, tm, stride=ROW_TILE), :]
            o_ref[:, cols] = acc
            ssq = ssq + jnp.sum(acc * acc, axis=-1, keepdims=True)
        scale = lax.rsqrt(ssq / D_MODEL + EPS)
        for q in range(COMB_J):
            cols = slice(q * COMB_TN, (q + 1) * COMB_TN)
            n3_ref[:, cols] = (o_ref[:, cols] * scale * gp_ref[:, cols]).astype(BF16)

    for q in range(COMB_J):
        @pl.when(j == q)
        def _():
            cols = slice(q * COMB_TN, (q + 1) * COMB_TN)
            gate = _sigmoid(jnp.dot(n3_ref[...], wg_ref[...].astype(BF16), preferred_element_type=F32))
            up = jnp.dot(p_ref[...].astype(BF16), wu_ref[...].astype(BF16), preferred_element_type=F32)
            o_ref[:, cols] = o_ref[:, cols] + gate * up

    @pl.when(j == COMB_J - 1)
    def _():
        ssq = jnp.zeros((tm, 1), F32)
        for q in range(COMB_J):
            v = o_ref[:, q * COMB_TN:(q + 1) * COMB_TN]
            ssq = ssq + jnp.sum(v * v, axis=-1, keepdims=True)
        scale = lax.rsqrt(ssq / D_MODEL + EPS)
        for q in range(COMB_J):
            cols = slice(q * COMB_TN, (q + 1) * COMB_TN)
            o_ref[:, cols] = o_ref[:, cols] * scale * gf_ref[:, cols]


def _combine(dest_flat, yb_rows, h, gates, g_ple, w_ple_gate, p_all, w_ple_up, g_final):
    tm = COMB_TM
    return pl.pallas_call(
        _combine_body,
        grid_spec=pltpu.PrefetchScalarGridSpec(
            num_scalar_prefetch=1,
            grid=(N_TOK // tm, COMB_J),
            in_specs=[
                pl.BlockSpec(memory_space=pl.ANY),
                pl.BlockSpec((tm, D_MODEL), lambda i, j, d: (i, 0)),
                pl.BlockSpec((tm, LANES), lambda i, j, d: (i, 0)),
                pl.BlockSpec((1, D_MODEL), lambda i, j, d: (0, 0)),
                pl.BlockSpec((D_MODEL, COMB_TN), lambda i, j, d: (0, j)),
                pl.BlockSpec((tm, D_PLE), lambda i, j, d: (i, 0)),
                pl.BlockSpec((D_PLE, COMB_TN), lambda i, j, d: (0, j)),
                pl.BlockSpec((1, D_MODEL), lambda i, j, d: (0, 0)),
            ],
            out_specs=pl.BlockSpec((tm, D_MODEL), lambda i, j, d: (i, 0)),
            scratch_shapes=[pltpu.VMEM((TOP_K * tm * ROW_TILE, LANES), F32),
                            pltpu.VMEM((tm, D_MODEL), BF16),
                            pltpu.SemaphoreType.DMA(())]),
        out_shape=jax.ShapeDtypeStruct((N_TOK, D_MODEL), F32),
        compiler_params=_cparams(2, 56),
        name="combine_ple_norm",
    )(dest_flat, yb_rows, h, gates, g_ple, w_ple_gate, p_all, w_ple_up, g_final)


def _routing_tables(counts, eidx, rank):
    nsub_e = (counts + MOE_SUB - 1) // MOE_SUB
    prow_e = (jnp.cumsum(nsub_e) - nsub_e) * MOE_SUB
    dest = (prow_e[eidx] + rank).reshape(-1).astype(I32)
    nsup_e = (nsub_e + MOE_NSB - 1) // MOE_NSB
    csup = jnp.cumsum(nsup_e)
    total = csup[-1]
    s = jnp.arange(MOE_SMAX, dtype=I32)
    e_raw = jnp.minimum(jnp.searchsorted(csup, s, side="right").astype(I32), N_EXPERTS - 1)
    valid = s < total
    e_last = e_raw[jnp.maximum(total - 1, 0)]
    e_s = jnp.where(valid, e_raw, e_last)
    local = s - (csup[e_s] - nsup_e[e_s])
    row_s = jnp.where(valid, prow_e[e_s] + local * MOE_TS, 0)
    n_s = jnp.where(valid, jnp.clip(nsub_e[e_s] - local * MOE_NSB, 0, MOE_NSB), 0)
    return dest, e_s.astype(I32), row_s.astype(I32), n_s.astype(I32)


def _pad_prev(state, width):
    b, _, c = state.shape
    return jnp.pad(state, ((0, 0), (SUBLANES - (width - 1), 0), (0, 0))).reshape(b * SUBLANES, c)


def kernel(x_prompt, x_sample, p_prompt, p_sample, state_conv_a, state_conv_ssm, state_ssm, g_mix, w_in, conv_a_w, ssm_conv_w, ssm_conv_b, dt_bias, a_log, d_skip, ssm_norm_g, w_branch_a, w_branch_b, w_out, g_ffn, w_router, b_router, w_up, b_up, w_down, b_down, g_ple, w_ple_gate, w_ple_up, g_final):
    xp = x_prompt.reshape(NP_TOK, D_MODEL)
    xs = x_sample.reshape(NS_TOK, D_MODEL)
    p_all = jnp.concatenate([p_prompt[0].reshape(NP_TOK, D_PLE), p_sample[0].reshape(NS_TOK, D_PLE)], axis=0)
    g_mix2 = g_mix[0].reshape(1, D_MODEL)

    proj = _rms_matmul(xp, xs, g_mix2, w_in, 0, N_MAIN, 1024, 512, "in_proj")
    dt_raw = _rms_matmul(xp, xs, g_mix2, w_in[:, :, COL_DT:COL_GATE], 0, SSM_HEADS, 1024, SSM_HEADS, "in_proj_dt")
    gates = _rms_matmul(xp, xs, g_mix2, w_in[:, :, COL_GATE:], 0, 2 * D_MODEL, 1024, 512, "in_proj_gates")

    ya_p, ua_tail = _conv_a_prompt(proj, conv_a_w[0])
    ya_s, ua_s = _conv_a_sample(proj, conv_a_w[0], _pad_prev(state_conv_a[0], CONV_A_W))
    conv_a_prompt = ua_tail[:, SUBLANES - (CONV_A_W - 1):, :]
    conv_a_sample = ua_s.reshape(DEC_BATCH, DEC_SEQ, D_CONV)[:, DEC_SEQ - (CONV_A_W - 1):, :]

    conv_b2 = ssm_conv_b[0].reshape(1, D_XBC)
    xbc_p = _conv_m_prompt(proj, ssm_conv_w[0], conv_b2)
    xbc_s = _conv_m_sample(proj, ssm_conv_w[0], conv_b2, _pad_prev(state_conv_ssm[0], SSM_CONV_W))
    proj3 = proj.reshape(N_TOK // DEC_SEQ, DEC_SEQ, N_MAIN)
    t0 = DEC_SEQ - (SSM_CONV_W - 1)
    conv_m_prompt = proj3[SEQ // DEC_SEQ - 1:NP_TOK // DEC_SEQ:SEQ // DEC_SEQ, t0:, COL_XBC:COL_XBC + D_XBC]
    conv_m_sample = proj3[NP_TOK // DEC_SEQ:, t0:, COL_XBC:COL_XBC + D_XBC]

    ssd_w = (dt_bias[0].reshape(1, SSM_HEADS), dt_bias[0].reshape(SSM_HEADS, 1),
             a_log[0].reshape(1, SSM_HEADS), a_log[0].reshape(SSM_HEADS, 1),
             jnp.repeat(d_skip[0], SSM_HEAD_DIM).reshape(1, D_INNER), ssm_norm_g[0].reshape(1, D_INNER))
    ym_p, ssm_p = _ssd_call(xbc_p, proj, dt_raw, *ssd_w)
    ym_s, ssm_s = _ssd_step_call(xbc_s, proj, dt_raw, *ssd_w, state_ssm)
    ssm_shape = (SSM_HEADS, SSM_HEAD_DIM, SSM_STATE)

    ya = jnp.concatenate([ya_p, ya_s], axis=0)
    ym = jnp.concatenate([ym_p, ym_s.astype(BF16)], axis=0)
    merged = _merge(ya, ym, w_branch_a[0], w_branch_b[0], gates)
    h = _resid_matmul(merged, w_out[0], xp, xs)

    wr_pad = jnp.pad(w_router[0], ((0, 0), (0, LANES - N_EXPERTS)))
    br_pad = jnp.pad(b_router[0].reshape(1, N_EXPERTS), ((0, 0), (0, LANES - N_EXPERTS)), constant_values=-1e30)
    g_ffn2 = g_ffn[0].reshape(1, D_MODEL)
    eidx, gate_w, rank, cnt = _router(h, g_ffn2, wr_pad, br_pad)
    counts = cnt[0, :N_EXPERTS].astype(I32)
    dest, sup_e, sup_row, sup_n = _routing_tables(counts, eidx[:, :TOP_K], rank[:, :TOP_K])
    xs_rows = _dispatch(dest, h, g_ffn2)
    yb_rows = _experts(sup_e, sup_row, sup_n, xs_rows, w_up[0], b_up[0], w_down[0], b_down[0])

    y_all = _combine(dest, yb_rows, h, gate_w, g_ple[0].reshape(1, D_MODEL), w_ple_gate[0], p_all,
                     w_ple_up[0], g_final.reshape(1, D_MODEL))
    y_prompt = y_all[:NP_TOK].reshape(BATCH, SEQ, D_MODEL)
    y_sample = y_all[NP_TOK:].reshape(DEC_BATCH, DEC_SEQ, D_MODEL)
    return (y_prompt, y_sample,
            conv_a_prompt[None], conv_m_prompt[None], ssm_p.reshape((1, BATCH) + ssm_shape),
            conv_a_sample[None], conv_m_sample[None], ssm_s)
```

```python
import functools

import jax
import jax.numpy as jnp
from jax import lax
from jax.experimental import pallas as pl
from jax.experimental.pallas import tpu as pltpu

F32 = jnp.float32
BF16 = jnp.bfloat16
I32 = jnp.int32
HI = lax.Precision.HIGHEST

D_MODEL = 2048
BATCH = 4
SEQ = 2048
DEC_BATCH = 128
DEC_SEQ = 8
D_PLE = 256
D_CONV = D_MODEL
CONV_A_W = 3
D_INNER = 2 * D_MODEL
SSM_HEAD_DIM = 64
SSM_HEADS = D_INNER // SSM_HEAD_DIM
SSM_GROUPS = 8
SSM_STATE = 128
SSM_CONV_W = 4
SSM_CHUNK = 128
D_XBC = D_INNER + 2 * SSM_GROUPS * SSM_STATE
N_EXPERTS = 32
TOP_K = 4
D_FF = D_MODEL
SWIGLU_LIMIT = 7.0
SWIGLU_ALPHA = 1.702
EPS = 1e-6

NP_TOK = BATCH * SEQ
NS_TOK = DEC_BATCH * DEC_SEQ
N_TOK = NP_TOK + NS_TOK
N_ASSIGN = N_TOK * TOP_K

LANES = 128
SUBLANES = 8
ROW_TILE = D_MODEL // LANES

COL_AX = 0
COL_AB = D_CONV
COL_AC = 2 * D_CONV
COL_Z = 3 * D_CONV
COL_XBC = COL_Z + D_INNER
COL_DT = COL_XBC + D_XBC
COL_GATE = COL_DT + SSM_HEADS
N_MAIN = COL_DT
N_PROJ = N_MAIN + 2 * D_MODEL

MOE_SUB = 256
MOE_TS = 1280
MOE_NSB = MOE_TS // MOE_SUB
MOE_CHUNK = MOE_TS // 2
MOE_ROWS = N_ASSIGN + N_EXPERTS * MOE_SUB
MOE_SMAX = N_ASSIGN // MOE_TS + N_EXPERTS
MOE_TN = 512
MOE_J1 = D_FF // MOE_TN
MOE_J2 = D_MODEL // MOE_TN

MIB = 1024 * 1024


def _cparams(n_grid, vmem_mib):
    return pltpu.CompilerParams(dimension_semantics=("arbitrary",) * n_grid,
                                vmem_limit_bytes=vmem_mib * MIB)


def _sigmoid(x):
    return 1.0 / (1.0 + jnp.exp(-x))


def _softplus(x):
    return jnp.maximum(x, 0.0) + jnp.log1p(jnp.exp(-jnp.abs(x)))


def _group_specs(tm, width, n_grid):
    npt = NP_TOK // tm
    if n_grid == 1:
        return (pl.BlockSpec((tm, width), lambda i: (jnp.minimum(i, npt - 1), 0)),
                pl.BlockSpec((tm, width), lambda i: (jnp.maximum(i - npt, 0), 0)))
    return (pl.BlockSpec((tm, width), lambda i, j, *_: (jnp.minimum(i, npt - 1), 0)),
            pl.BlockSpec((tm, width), lambda i, j, *_: (jnp.maximum(i - npt, 0), 0)))


def _rms_mm_body(npt, xp_ref, xs_ref, g_ref, w_ref, o_ref, n_ref):
    i = pl.program_id(0)
    first = pl.program_id(1) == 0

    def norm(x_ref):
        x = x_ref[...]
        ms = jnp.mean(x * x, axis=-1, keepdims=True)
        n_ref[...] = (x * lax.rsqrt(ms + EPS) * g_ref[...]).astype(BF16)

    @pl.when(jnp.logical_and(first, i < npt))
    def _():
        norm(xp_ref)

    @pl.when(jnp.logical_and(first, i >= npt))
    def _():
        norm(xs_ref)

    o_ref[...] = lax.dot_general(n_ref[...], w_ref[...].astype(BF16), (((1,), (1,)), ((), ())),
                                 preferred_element_type=F32)


def _rms_matmul(xp, xs, g, w_t, feature_offset, ncols, tm, tn, name):
    k = xp.shape[1]
    xp_spec, xs_spec = _group_specs(tm, k, 2)
    return pl.pallas_call(
        functools.partial(_rms_mm_body, NP_TOK // tm),
        grid=(N_TOK // tm, ncols // tn),
        in_specs=[xp_spec, xs_spec,
                  pl.BlockSpec((1, k), lambda i, j: (0, 0)),
                  pl.BlockSpec((pl.Element(tn), pl.Element(k)),
                               lambda i, j: (pl.multiple_of(feature_offset(j), SUBLANES), 0))],
        out_specs=pl.BlockSpec((tm, tn), lambda i, j: (i, j)),
        out_shape=jax.ShapeDtypeStruct((N_TOK, ncols), F32),
        scratch_shapes=[pltpu.VMEM((tm, k), BF16)],
        compiler_params=_cparams(2, 56),
        name=name,
    )(xp, xs, g, w_t)


def _causal_taps(u, w_ref, width, rowmod, prev):
    rows = u.shape[0]
    acc = u * w_ref[width - 1:width, :]
    for s in range(1, width):
        shifted = pltpu.roll(u, s, 0)
        if prev is None:
            fill = jnp.zeros_like(u)
        else:
            fill = pltpu.roll(prev, rows - SUBLANES + s, 0)
        acc = acc + jnp.where(rowmod >= s, shifted, fill) * w_ref[width - 1 - s:width - s, :]
    return acc


def _row_in_seq(shape, period):
    r = lax.broadcasted_iota(I32, shape, 0)
    return r if period >= shape[0] else r & (period - 1)


def _conv_a_body(period, has_prev, *refs):
    if has_prev:
        ax_ref, ab_ref, ac_ref, w_ref, prev_ref, ya_ref, u_ref = refs
        prev = prev_ref[...]
    else:
        ax_ref, ab_ref, ac_ref, w_ref, ya_ref, u_ref = refs
        prev = None
    u = ac_ref[...] * ax_ref[...]
    v = _causal_taps(u, w_ref, CONV_A_W, _row_in_seq(u.shape, period), prev)
    ya_ref[...] = (ab_ref[...] * v).astype(BF16)
    if has_prev:
        u_ref[...] = u
    else:
        u_ref[...] = u[u.shape[0] - SUBLANES:, :]


def _conv_m_body(period, has_prev, *refs):
    if has_prev:
        x_ref, w_ref, b_ref, prev_ref, o_ref = refs
        prev = prev_ref[...]
    else:
        x_ref, w_ref, b_ref, o_ref = refs
        prev = None
    u = x_ref[...]
    v = _causal_taps(u, w_ref, SSM_CONV_W, _row_in_seq(u.shape, period), prev) + b_ref[...]
    o_ref[...] = v * _sigmoid(v)


CONV_TC = 256


def _conv_a_prompt(proj, conv_w):
    nb = D_CONV // CONV_TC
    blk = lambda off: pl.BlockSpec((SEQ, CONV_TC), lambda b, c, off=off: (b, off // CONV_TC + c))
    return pl.pallas_call(
        functools.partial(_conv_a_body, SEQ, False),
        grid=(BATCH, nb),
        in_specs=[blk(COL_AX), blk(COL_AB), blk(COL_AC),
                  pl.BlockSpec((CONV_A_W, CONV_TC), lambda b, c: (0, c))],
        out_specs=[pl.BlockSpec((SEQ, CONV_TC), lambda b, c: (b, c)),
                   pl.BlockSpec((None, SUBLANES, CONV_TC), lambda b, c: (b, 0, c))],
        out_shape=[jax.ShapeDtypeStruct((NP_TOK, D_CONV), BF16),
                   jax.ShapeDtypeStruct((BATCH, SUBLANES, D_CONV), F32)],
        compiler_params=_cparams(2, 48),
        name="conv_a_prompt",
    )(proj, proj, proj, conv_w)


def _conv_a_sample(proj, conv_w, prev_pad):
    nb = D_CONV // CONV_TC
    rb = NP_TOK // NS_TOK
    blk = lambda off: pl.BlockSpec((NS_TOK, CONV_TC), lambda c, off=off: (rb, off // CONV_TC + c))
    own = pl.BlockSpec((NS_TOK, CONV_TC), lambda c: (0, c))
    return pl.pallas_call(
        functools.partial(_conv_a_body, DEC_SEQ, True),
        grid=(nb,),
        in_specs=[blk(COL_AX), blk(COL_AB), blk(COL_AC),
                  pl.BlockSpec((CONV_A_W, CONV_TC), lambda c: (0, c)), own],
        out_specs=[own, own],
        out_shape=[jax.ShapeDtypeStruct((NS_TOK, D_CONV), BF16),
                   jax.ShapeDtypeStruct((NS_TOK, D_CONV), F32)],
        compiler_params=_cparams(1, 48),
        name="conv_a_sample",
    )(proj, proj, proj, conv_w, prev_pad)


def _conv_m_prompt(proj, conv_w, conv_b):
    nb = D_XBC // CONV_TC
    return pl.pallas_call(
        functools.partial(_conv_m_body, SEQ, False),
        grid=(BATCH, nb),
        in_specs=[pl.BlockSpec((SEQ, CONV_TC), lambda b, c: (b, COL_XBC // CONV_TC + c)),
                  pl.BlockSpec((SSM_CONV_W, CONV_TC), lambda b, c: (0, c)),
                  pl.BlockSpec((1, CONV_TC), lambda b, c: (0, c))],
        out_specs=pl.BlockSpec((SEQ, CONV_TC), lambda b, c: (b, c)),
        out_shape=jax.ShapeDtypeStruct((NP_TOK, D_XBC), F32),
        compiler_params=_cparams(2, 48),
        name="conv_m_prompt",
    )(proj, conv_w, conv_b)


def _conv_m_sample(proj, conv_w, conv_b, prev_pad):
    nb = D_XBC // CONV_TC
    rb = NP_TOK // NS_TOK
    own = pl.BlockSpec((NS_TOK, CONV_TC), lambda c: (0, c))
    return pl.pallas_call(
        functools.partial(_conv_m_body, DEC_SEQ, True),
        grid=(nb,),
        in_specs=[pl.BlockSpec((NS_TOK, CONV_TC), lambda c: (rb, COL_XBC // CONV_TC + c)),
                  pl.BlockSpec((SSM_CONV_W, CONV_TC), lambda c: (0, c)),
                  pl.BlockSpec((1, CONV_TC), lambda c: (0, c)), own],
        out_specs=own,
        out_shape=jax.ShapeDtypeStruct((NS_TOK, D_XBC), F32),
        compiler_params=_cparams(1, 48),
        name="conv_m_sample",
    )(proj, conv_w, conv_b, prev_pad)


def _ssd_body(xbc_ref, zlo_ref, zhi_ref, dtr_ref, dtb_ref, dtbc_ref, alog_ref, alogc_ref, dsk_ref, ng_ref,
              ym_ref, hl_ref, ht_ref, y_ref):
    L = SSM_CHUNK
    H = SSM_HEADS
    c = pl.program_id(1)

    @pl.when(c == 0)
    def _():
        ht_ref[...] = jnp.zeros_like(ht_ref)

    dtr = dtr_ref[...]
    dt = _softplus(dtr + dtb_ref[...])
    eye = (lax.broadcasted_iota(I32, (H, H), 0) == lax.broadcasted_iota(I32, (H, H), 1)).astype(F32)
    dtr_t = lax.dot_general(eye, dtr, (((1,), (1,)), ((), ())), precision=HI,
                            preferred_element_type=F32)
    dt_t = _softplus(dtr_t + dtbc_ref[...])
    adt = dt * (-jnp.exp(alog_ref[...]))
    adt_t = dt_t * (-jnp.exp(alogc_ref[...]))
    li = lax.broadcasted_iota(I32, (L, L), 0)
    si = lax.broadcasted_iota(I32, (L, L), 1)
    causal = li >= si
    a_cum = jnp.dot(causal.astype(F32), adt, precision=HI, preferred_element_type=F32)
    a_cum_t = jnp.dot(adt_t, (li <= si).astype(F32), precision=HI, preferred_element_type=F32)
    ea = jnp.exp(a_cum)
    a_tot_t = a_cum_t[:, L - 1:L]
    dte_t = jnp.exp(a_tot_t - a_cum_t)
    ea_tot_t = jnp.exp(a_tot_t)

    lane = lax.broadcasted_iota(I32, (L, LANES), 1)
    lo = lane < SSM_HEAD_DIM
    lane1 = lax.broadcasted_iota(I32, (1, LANES), 1) < SSM_HEAD_DIM
    gn = SSM_GROUPS * SSM_STATE
    for g in range(SSM_GROUPS):
        bg = xbc_ref[:, D_INNER + g * SSM_STATE:D_INNER + (g + 1) * SSM_STATE]
        cg = xbc_ref[:, D_INNER + gn + g * SSM_STATE:D_INNER + gn + (g + 1) * SSM_STATE]
        bg_t = bg.T
        cb = jnp.dot(cg.astype(BF16), bg_t.astype(BF16), preferred_element_type=F32)
        for jj in range(SSM_HEADS // SSM_GROUPS // 2):
            j = g * (SSM_HEADS // SSM_GROUPS // 2) + jj
            cols = slice(j * LANES, (j + 1) * LANES)
            lhs1, lhs2 = [], []
            for h in (2 * j, 2 * j + 1):
                diff = a_cum[:, h:h + 1] - a_cum_t[h:h + 1, :]
                decay = jnp.exp(jnp.where(causal, diff, -1e30))
                lhs1.append((cb * decay).astype(BF16))
            for h in (2 * j, 2 * j + 1):
                lhs1.append((cg * ea[:, h:h + 1]).astype(BF16))
                lhs2.append((bg_t * dte_t[h:h + 1, :]).astype(BF16))
            dt_pair = jnp.where(lo, dt[:, 2 * j:2 * j + 1], dt[:, 2 * j + 1:2 * j + 2])
            xdt = xbc_ref[:, cols] * dt_pair
            zero = jnp.zeros_like(xdt)
            x_top = jnp.where(lo, xdt, zero).astype(BF16)
            x_bot = jnp.where(lo, zero, xdt).astype(BF16)
            ht = ht_ref[:, cols]
            h_top = jnp.where(lo, ht, zero).astype(BF16)
            h_bot = jnp.where(lo, zero, ht).astype(BF16)
            y_ref[:, cols] = jnp.dot(jnp.concatenate(lhs1, axis=1),
                                     jnp.concatenate([x_top, x_bot, h_top, h_bot], axis=0),
                                     preferred_element_type=F32)
            upd = jnp.dot(jnp.concatenate(lhs2, axis=1), jnp.concatenate([x_top, x_bot], axis=0),
                          preferred_element_type=F32)
            keep = jnp.where(lane1, ea_tot_t[2 * j:2 * j + 1, :], ea_tot_t[2 * j + 1:2 * j + 2, :])
            ht_ref[:, cols] = ht * keep + upd

    gw = D_INNER // SSM_GROUPS
    for g in range(SSM_GROUPS):
        cols = slice(g * gw, (g + 1) * gw)
        zr = zlo_ref if g < SSM_GROUPS // 2 else zhi_ref
        zc = slice((g % (SSM_GROUPS // 2)) * gw, (g % (SSM_GROUPS // 2) + 1) * gw)
        z = zr[:, zc]
        y = y_ref[:, cols] + dsk_ref[:, cols] * xbc_ref[:, cols]
        y = y * (z * _sigmoid(z))
        y = y * lax.rsqrt(jnp.mean(y * y, axis=-1, keepdims=True) + EPS)
        ym_ref[:, cols] = (y * ng_ref[:, cols]).astype(ym_ref.dtype)

    @pl.when(c == pl.num_programs(1) - 1)
    def _():
        hl_ref[...] = ht_ref[...].T


def _ssd_call(xbc, proj, dt_raw, dtb, dtbc, alog, alogc, dsk, ng):
    L = SSM_CHUNK
    n_chunk = SEQ // L
    hp = SSM_HEADS * SSM_HEAD_DIM
    zb = D_INNER // 2
    vec = lambda shape: pl.BlockSpec(shape, lambda b, c: (0, 0))
    return pl.pallas_call(
        _ssd_body,
        grid=(BATCH, n_chunk),
        in_specs=[pl.BlockSpec((L, D_XBC), lambda b, c: (b * n_chunk + c, 0)),
                  pl.BlockSpec((L, zb), lambda b, c: (b * n_chunk + c, COL_Z // zb)),
                  pl.BlockSpec((L, zb), lambda b, c: (b * n_chunk + c, COL_Z // zb + 1)),
                  pl.BlockSpec((L, SSM_HEADS), lambda b, c: (b * n_chunk + c, 0)),
                  vec((1, SSM_HEADS)), vec((SSM_HEADS, 1)), vec((1, SSM_HEADS)), vec((SSM_HEADS, 1)),
                  vec((1, D_INNER)), vec((1, D_INNER))],
        out_specs=[pl.BlockSpec((L, D_INNER), lambda b, c: (b * n_chunk + c, 0)),
                   pl.BlockSpec((None, hp, SSM_STATE), lambda b, c: (b, 0, 0))],
        out_shape=[jax.ShapeDtypeStruct((NP_TOK, D_INNER), BF16),
                   jax.ShapeDtypeStruct((BATCH, hp, SSM_STATE), F32)],
        scratch_shapes=[pltpu.VMEM((SSM_STATE, hp), F32), pltpu.VMEM((L, hp), F32)],
        compiler_params=_cparams(2, 48),
        name="ssd_prompt",
    )(xbc, proj, proj, dt_raw, dtb, dtbc, alog, alogc, dsk, ng)


def _ssd_step_body(xbc_in, zlo_ref, zhi_ref, dtr_in, dtb_ref, dtbc_ref, alog_ref, alogc_ref, dsk_ref, ng_ref,
                   h0_ref, ym_ref, hl_ref, xp_ref, dp_ref):
    T = DEC_SEQ
    L = SSM_CHUNK
    H = SSM_HEADS
    P = SSM_HEAD_DIM

    @pl.when(pl.program_id(0) == 0)
    def _():
        xp_ref[...] = jnp.zeros_like(xp_ref)
        dp_ref[...] = jnp.zeros_like(dp_ref)

    xp_ref[0:T, :] = xbc_in[...]
    dp_ref[0:T, :] = dtr_in[...]

    dt8 = _softplus(dtr_in[...] + dtb_ref[...])
    a_cum8 = dt8 * (-jnp.exp(alog_ref[...]))
    row8 = lax.broadcasted_iota(I32, (T, H), 0)
    k = 1
    while k < T:
        a_cum8 = a_cum8 + jnp.where(row8 >= k, pltpu.roll(a_cum8, k, 0), 0.0)
        k *= 2
    ea8 = jnp.exp(a_cum8)
    dte8 = jnp.exp(a_cum8[T - 1:T, :] - a_cum8)
    eye = (lax.broadcasted_iota(I32, (H, H), 0) == lax.broadcasted_iota(I32, (H, H), 1)).astype(F32)
    dtr_t = lax.dot_general(eye, dp_ref[...], (((1,), (1,)), ((), ())), precision=HI,
                            preferred_element_type=F32)
    dt_t = jnp.where(lax.broadcasted_iota(I32, (H, L), 1) < T, _softplus(dtr_t + dtbc_ref[...]), 0.0)
    adt_t = dt_t * (-jnp.exp(alogc_ref[...]))
    upper = (lax.broadcasted_iota(I32, (L, L), 0) <= lax.broadcasted_iota(I32, (L, L), 1)).astype(F32)
    a_cum_t = jnp.dot(adt_t, upper, precision=HI, preferred_element_type=F32)
    ea_tot_t = jnp.exp(a_cum_t[:, L - 1:L])

    causal8 = lax.broadcasted_iota(I32, (T, L), 0) >= lax.broadcasted_iota(I32, (T, L), 1)
    lo8 = lax.broadcasted_iota(I32, (T, LANES), 1) < P
    zeros_t = jnp.zeros((T, LANES), F32)
    zeros_pad = jnp.zeros((L - T, LANES), F32)
    gn = SSM_GROUPS * SSM_STATE
    gw = D_INNER // SSM_GROUPS
    pairs = SSM_HEADS // SSM_GROUPS // 2
    for g in range(SSM_GROUPS):
        bgp = xp_ref[:, D_INNER + g * SSM_STATE:D_INNER + (g + 1) * SSM_STATE].astype(BF16)
        cg8 = xbc_in[:, D_INNER + gn + g * SSM_STATE:D_INNER + gn + (g + 1) * SSM_STATE]
        cg16 = jnp.concatenate([cg8, zeros_t], axis=0).astype(BF16)
        cb = lax.dot_general(cg16, bgp, (((1,), (1,)), ((), ())), preferred_element_type=F32)[0:T]
        ys = []
        for jj in range(pairs):
            j = g * pairs + jj
            ha, hb = 2 * j, 2 * j + 1
            cols = slice(j * LANES, (j + 1) * LANES)

            def decay(h):
                diff = a_cum8[:, h:h + 1] - a_cum_t[h:h + 1, :]
                return jnp.exp(jnp.where(causal8, diff, -1e30))

            m = jnp.concatenate([cb * decay(ha), cb * decay(hb)], axis=0).astype(BF16)
            cd = jnp.concatenate([cg8 * ea8[:, ha:ha + 1], cg8 * ea8[:, hb:hb + 1]], axis=0).astype(BF16)
            xdt8 = xbc_in[:, cols] * jnp.where(lo8, dt8[:, ha:ha + 1], dt8[:, hb:hb + 1])
            xd8 = xdt8 * jnp.where(lo8, dte8[:, ha:ha + 1], dte8[:, hb:hb + 1])
            xdt_pad = jnp.concatenate([xdt8, zeros_pad], axis=0).astype(BF16)
            xd_pad = jnp.concatenate([xd8, zeros_pad], axis=0).astype(BF16)
            hp = h0_ref[ha:hb + 1].reshape(2 * P, SSM_STATE)
            out = (jnp.dot(m, xdt_pad, preferred_element_type=F32)
                   + lax.dot_general(cd, hp.astype(BF16), (((1,), (1,)), ((), ())),
                                     preferred_element_type=F32))
            ys.append(jnp.where(lo8, out[0:T], out[T:2 * T]))
            upd = lax.dot_general(xd_pad, bgp, (((0,), (0,)), ((), ())), preferred_element_type=F32)
            keep = jnp.concatenate([jnp.broadcast_to(ea_tot_t[ha:ha + 1, :], (P, SSM_STATE)),
                                    jnp.broadcast_to(ea_tot_t[hb:hb + 1, :], (P, SSM_STATE))], axis=0)
            hl_ref[ha:hb + 1] = (hp * keep + upd).reshape(2, P, SSM_STATE)

        cols = slice(g * gw, (g + 1) * gw)
        zr = zlo_ref if g < SSM_GROUPS // 2 else zhi_ref
        zc = slice((g % (SSM_GROUPS // 2)) * gw, (g % (SSM_GROUPS // 2) + 1) * gw)
        z = zr[:, zc]
        y = jnp.concatenate(ys, axis=1) + dsk_ref[:, cols] * xbc_in[:, cols]
        y = y * (z * _sigmoid(z))
        y = y * lax.rsqrt(jnp.mean(y * y, axis=-1, keepdims=True) + EPS)
        ym_ref[:, cols] = y * ng_ref[:, cols]


def _ssd_step_call(xbc, proj, dt_raw, dtb, dtbc, alog, alogc, dsk, ng, state):
    T = DEC_SEQ
    rb0 = NP_TOK // T
    zb = D_INNER // 2
    st = pl.BlockSpec((None, None, SSM_HEADS, SSM_HEAD_DIM, SSM_STATE), lambda b: (0, b, 0, 0, 0))
    vec = lambda shape: pl.BlockSpec(shape, lambda b: (0, 0))
    return pl.pallas_call(
        _ssd_step_body,
        grid=(DEC_BATCH,),
        in_specs=[pl.BlockSpec((T, D_XBC), lambda b: (b, 0)),
                  pl.BlockSpec((T, zb), lambda b: (rb0 + b, COL_Z // zb)),
                  pl.BlockSpec((T, zb), lambda b: (rb0 + b, COL_Z // zb + 1)),
                  pl.BlockSpec((T, SSM_HEADS), lambda b: (rb0 + b, 0)),
                  vec((1, SSM_HEADS)), vec((SSM_HEADS, 1)), vec((1, SSM_HEADS)), vec((SSM_HEADS, 1)),
                  vec((1, D_INNER)), vec((1, D_INNER)), st],
        out_specs=[pl.BlockSpec((T, D_INNER), lambda b: (b, 0)), st],
        out_shape=[jax.ShapeDtypeStruct((NS_TOK, D_INNER), F32),
                   jax.ShapeDtypeStruct(state.shape, F32)],
        scratch_shapes=[pltpu.VMEM((SSM_CHUNK, D_XBC), F32), pltpu.VMEM((SSM_CHUNK, SSM_HEADS), F32)],
        compiler_params=_cparams(1, 32),
        name="ssd_sample",
    )(xbc, proj, proj, dt_raw, dtb, dtbc, alog, alogc, dsk, ng, state)


def _merge_body(ya_ref, ym_ref, wa_ref, wb_ref, ga_ref, gb_ref, o_ref):
    a = jnp.dot(ya_ref[...], wa_ref[...].astype(BF16), preferred_element_type=F32)
    b = jnp.dot(ym_ref[...], wb_ref[...].astype(BF16), preferred_element_type=F32)
    o_ref[...] = (_sigmoid(ga_ref[...]) * a + _sigmoid(gb_ref[...]) * b).astype(BF16)


def _merge(ya, ym, wa, wb, proj):
    tm, tn = 1024, 256
    g0 = N_MAIN // tn
    return pl.pallas_call(
        _merge_body,
        grid=(N_TOK // tm, D_MODEL // tn),
        in_specs=[pl.BlockSpec((tm, D_CONV), lambda i, j: (i, 0)),
                  pl.BlockSpec((tm, D_INNER), lambda i, j: (i, 0)),
                  pl.BlockSpec((D_CONV, tn), lambda i, j: (0, j)),
                  pl.BlockSpec((D_INNER, tn), lambda i, j: (0, j)),
                  pl.BlockSpec((tm, tn), lambda i, j: (i, g0 + j)),
                  pl.BlockSpec((tm, tn), lambda i, j: (i, g0 + D_MODEL // tn + j))],
        out_specs=pl.BlockSpec((tm, tn), lambda i, j: (i, j)),
        out_shape=jax.ShapeDtypeStruct((N_TOK, D_MODEL), BF16),
        compiler_params=_cparams(2, 52),
        name="merge",
    )(ya, ym, wa, wb, proj, proj)


def _resid_body(npt, m_ref, w_ref, xp_ref, xs_ref, o_ref):
    x = jnp.where(pl.program_id(0) < npt, xp_ref[...], xs_ref[...])
    o_ref[...] = x + jnp.dot(m_ref[...], w_ref[...].astype(BF16), preferred_element_type=F32)


def _resid_matmul(m, w, xp, xs):
    tm, tn = 1024, 512
    npt = NP_TOK // tm
    return pl.pallas_call(
        functools.partial(_resid_body, npt),
        grid=(N_TOK // tm, D_MODEL // tn),
        in_specs=[pl.BlockSpec((tm, D_MODEL), lambda i, j: (i, 0)),
                  pl.BlockSpec((D_MODEL, tn), lambda i, j: (0, j)),
                  pl.BlockSpec((tm, tn), lambda i, j: (jnp.minimum(i, npt - 1), j)),
                  pl.BlockSpec((tm, tn), lambda i, j: (jnp.maximum(i - npt, 0), j))],
        out_specs=pl.BlockSpec((tm, tn), lambda i, j: (i, j)),
        out_shape=jax.ShapeDtypeStruct((N_TOK, D_MODEL), F32),
        compiler_params=_cparams(2, 48),
        name="out_proj",
    )(m, w, xp, xs)


ROUTER_TM = 512


def _router_body(h_ref, g_ref, wr_ref, br_ref, ei_ref, gt_ref, rk_ref, cnt_ref, carry_ref):
    tm = ROUTER_TM

    @pl.when(pl.program_id(0) == 0)
    def _():
        carry_ref[...] = jnp.zeros_like(carry_ref)

    h = h_ref[...]
    n2 = h * lax.rsqrt(jnp.mean(h * h, axis=-1, keepdims=True) + EPS) * g_ref[...]
    logits = jnp.dot(n2, wr_ref[...], precision=HI, preferred_element_type=F32) + br_ref[...]

    lane = lax.broadcasted_iota(I32, (tm, LANES), 1).astype(F32)
    work = logits
    vals, idxs = [], []
    for _ in range(TOP_K):
        m = jnp.max(work, axis=-1, keepdims=True)
        idx = jnp.min(jnp.where(work == m, lane, float(LANES)), axis=-1, keepdims=True)
        vals.append(m)
        idxs.append(idx)
        work = jnp.where(lane == idx, -jnp.inf, work)
    ex = [jnp.exp(v - vals[0]) for v in vals]
    den = ex[0] + ex[1] + ex[2] + ex[3]
    sel = jnp.zeros((tm, LANES), F32)
    for idx in idxs:
        sel = sel + jnp.where(lane == idx, 1.0, 0.0)
    before = (lax.broadcasted_iota(I32, (tm, tm), 0) > lax.broadcasted_iota(I32, (tm, tm), 1))
    rank_e = jnp.dot(before.astype(BF16), sel.astype(BF16), preferred_element_type=F32) + carry_ref[0:1, :]
    carry_ref[...] = carry_ref[...] + jnp.sum(sel, axis=0, keepdims=True)
    ei = jnp.zeros((tm, LANES), F32)
    gt = jnp.zeros((tm, LANES), F32)
    rk = jnp.zeros((tm, LANES), F32)
    for k in range(TOP_K):
        rank_k = jnp.sum(jnp.where(lane == idxs[k], rank_e, 0.0), axis=-1, keepdims=True)
        here = lane == float(k)
        ei = jnp.where(here, idxs[k], ei)
        gt = jnp.where(here, ex[k] / den, gt)
        rk = jnp.where(here, rank_k, rk)
    ei_ref[...] = ei.astype(I32)
    gt_ref[...] = gt
    rk_ref[...] = rk.astype(I32)
    cnt_ref[...] = carry_ref[...]


def _router(h, g, wr_pad, br_pad):
    tm = ROUTER_TM
    tok = pl.BlockSpec((tm, LANES), lambda i: (i, 0))
    return pl.pallas_call(
        _router_body,
        grid=(N_TOK // tm,),
        in_specs=[pl.BlockSpec((tm, D_MODEL), lambda i: (i, 0)),
                  pl.BlockSpec((1, D_MODEL), lambda i: (0, 0)),
                  pl.BlockSpec((D_MODEL, LANES), lambda i: (0, 0)),
                  pl.BlockSpec((1, LANES), lambda i: (0, 0))],
        out_specs=[tok, tok, tok, pl.BlockSpec((SUBLANES, LANES), lambda i: (0, 0))],
        out_shape=[jax.ShapeDtypeStruct((N_TOK, LANES), I32),
                   jax.ShapeDtypeStruct((N_TOK, LANES), F32),
                   jax.ShapeDtypeStruct((N_TOK, LANES), I32),
                   jax.ShapeDtypeStruct((SUBLANES, LANES), F32)],
        scratch_shapes=[pltpu.VMEM((SUBLANES, LANES), F32)],
        compiler_params=_cparams(1, 48),
        name="router",
    )(h, g, wr_pad, br_pad)


DISPATCH_TOK = 512


def _row_slab(ref, row):
    start = row * ROW_TILE
    if not isinstance(start, int):
        start = pl.multiple_of(start, ROW_TILE)
    return ref.at[pl.ds(start, ROW_TILE)]


def _row_copy(src, src_row, dst, dst_row, sem):
    return pltpu.make_async_copy(_row_slab(src, src_row), _row_slab(dst, dst_row), sem)


def _dispatch_body(dest_ref, h_ref, g_ref, zero_hbm, xs_hbm, rows_ref, sem):
    del zero_hbm
    tm = DISPATCH_TOK
    base = pl.program_id(0) * tm
    h = h_ref[...]
    n2 = h * lax.rsqrt(jnp.mean(h * h, axis=-1, keepdims=True) + EPS) * g_ref[...]
    for cidx in range(ROW_TILE):
        rows_ref[pl.ds(cidx, tm, stride=ROW_TILE), :] = n2[:, cidx * LANES:(cidx + 1) * LANES]

    def issue(i, carry):
        for k in range(TOP_K):
            _row_copy(rows_ref, i, xs_hbm, dest_ref[(base + i) * TOP_K + k], sem).start()
        return carry

    lax.fori_loop(0, tm, issue, 0)

    for k in range(TOP_K):
        pltpu.make_async_copy(rows_ref, xs_hbm.at[pl.ds(0, tm * ROW_TILE)], sem).wait()


def _dispatch(dest_flat, h, g):
    tm = DISPATCH_TOK
    zeros = jnp.zeros((MOE_ROWS * ROW_TILE, LANES), F32)
    return pl.pallas_call(
        _dispatch_body,
        grid_spec=pltpu.PrefetchScalarGridSpec(
            num_scalar_prefetch=1,
            grid=(N_TOK // tm,),
            in_specs=[pl.BlockSpec((tm, D_MODEL), lambda i, d: (i, 0)),
                      pl.BlockSpec((1, D_MODEL), lambda i, d: (0, 0)),
                      pl.BlockSpec(memory_space=pl.ANY)],
            out_specs=pl.BlockSpec(memory_space=pl.ANY),
            scratch_shapes=[pltpu.VMEM((tm * ROW_TILE, LANES), F32), pltpu.SemaphoreType.DMA(())]),
        out_shape=jax.ShapeDtypeStruct((MOE_ROWS * ROW_TILE, LANES), F32),
        input_output_aliases={3: 0},
        compiler_params=_cparams(1, 32),
        name="moe_dispatch",
    )(dest_flat, h, g, zeros)


def _expert_body(e_ref, r_ref, n_ref, xs_hbm, wg_ref, wu_ref, bg_ref, bu_ref, wd_ref, bd_ref, yb_hbm,
                 stage_ref, xb_ref, act_ref, sem_in, sem_out):
    del e_ref
    s = pl.program_id(0)
    j = pl.program_id(1)
    nsub = n_ref[s]
    row0 = r_ref[s]
    sub_rows = MOE_SUB * ROW_TILE

    def in_copy(k):
        return pltpu.make_async_copy(
            xs_hbm.at[pl.ds(pl.multiple_of((row0 + k * MOE_SUB) * ROW_TILE, sub_rows), sub_rows)],
            stage_ref.at[pl.ds(k * sub_rows, sub_rows)], sem_in)

    def out_copy(k):
        return pltpu.make_async_copy(
            stage_ref.at[pl.ds(k * sub_rows, sub_rows)],
            yb_hbm.at[pl.ds(pl.multiple_of((row0 + k * MOE_SUB) * ROW_TILE, sub_rows), sub_rows)], sem_out)

    @pl.when(j == 0)
    def _():
        for k in range(MOE_NSB):
            @pl.when(k < nsub)
            def _():
                in_copy(k).start()
        for k in range(MOE_NSB):
            @pl.when(k < nsub)
            def _():
                in_copy(k).wait()
        for k in range(MOE_NSB):
            @pl.when(k < nsub)
            def _():
                for cidx in range(ROW_TILE):
                    xb_ref[k * MOE_SUB:(k + 1) * MOE_SUB, cidx * LANES:(cidx + 1) * LANES] = (
                        stage_ref[pl.ds(k * sub_rows + cidx, MOE_SUB, stride=ROW_TILE), :].astype(BF16))

    full = nsub == MOE_NSB

    @pl.when(j < MOE_J1)
    def _():
        wg = wg_ref[...].astype(BF16)
        wu = wu_ref[...].astype(BF16)

        def up_rows(r0, nrows):
            x = xb_ref[r0:r0 + nrows, :]
            gate = jnp.dot(x, wg, preferred_element_type=F32) + bg_ref[...]
            up = jnp.dot(x, wu, preferred_element_type=F32) + bu_ref[...]
            gate = jnp.minimum(gate, SWIGLU_LIMIT)
            up = jnp.clip(up, -SWIGLU_LIMIT, SWIGLU_LIMIT)
            a = gate * _sigmoid(SWIGLU_ALPHA * gate) * (up + 1.0)
            act_ref[jnp.minimum(j, MOE_J1 - 1), r0:r0 + nrows, :] = a.astype(BF16)

        @pl.when(full)
        def _():
            for r0 in range(0, MOE_TS, MOE_CHUNK):
                up_rows(r0, MOE_CHUNK)

        for k in range(MOE_NSB - 1):
            @pl.when(jnp.logical_and(jnp.logical_not(full), k < nsub))
            def _():
                up_rows(k * MOE_SUB, MOE_SUB)

    per = MOE_TN // LANES
    for jd in range(MOE_J2):
        @pl.when(j == MOE_J1 + jd)
        def _():
            wd = wd_ref[...].astype(BF16)

            def down_rows(r0, nrows):
                y = bd_ref[...] + jnp.dot(act_ref[0, r0:r0 + nrows, :], wd[0:MOE_TN, :],
                                          preferred_element_type=F32)
                for q in range(1, MOE_J1):
                    y = y + jnp.dot(act_ref[q, r0:r0 + nrows, :], wd[q * MOE_TN:(q + 1) * MOE_TN, :],
                                    preferred_element_type=F32)
                for cc in range(per):
                    stage_ref[pl.ds(r0 * ROW_TILE + jd * per + cc, nrows, stride=ROW_TILE), :] = (
                        y[:, cc * LANES:(cc + 1) * LANES])

            @pl.when(full)
            def _():
                for r0 in range(0, MOE_TS, MOE_CHUNK):
                    down_rows(r0, MOE_CHUNK)

            for k in range(MOE_NSB - 1):
                @pl.when(jnp.logical_and(jnp.logical_not(full), k < nsub))
                def _():
                    down_rows(k * MOE_SUB, MOE_SUB)

    @pl.when(j == MOE_J1 + MOE_J2 - 1)
    def _():
        for k in range(MOE_NSB):
            @pl.when(k < nsub)
            def _():
                out_copy(k).start()
        for k in range(MOE_NSB):
            @pl.when(k < nsub)
            def _():
                out_copy(k).wait()


def _experts(sup_e, sup_row, sup_n, xs_rows, w_up, b_up, w_down, b_down):
    def up_j(s, j, e, r, n):
        return jnp.where(n[s] > 0, jnp.minimum(j, MOE_J1 - 1), MOE_J1 - 1)

    def dn_j(s, j, e, r, n):
        return jnp.where(n[s] > 0, jnp.maximum(j - MOE_J1, 0), MOE_J2 - 1)

    b_up3 = b_up.reshape(N_EXPERTS, 1, 2 * D_FF)
    b_down3 = b_down.reshape(N_EXPERTS, 1, D_MODEL)
    return pl.pallas_call(
        _expert_body,
        grid_spec=pltpu.PrefetchScalarGridSpec(
            num_scalar_prefetch=3,
            grid=(MOE_SMAX, MOE_J1 + MOE_J2),
            in_specs=[
                pl.BlockSpec(memory_space=pl.ANY),
                pl.BlockSpec((None, D_MODEL, MOE_TN), lambda s, j, e, r, n: (e[s], 0, up_j(s, j, e, r, n))),
                pl.BlockSpec((None, D_MODEL, MOE_TN),
                             lambda s, j, e, r, n: (e[s], 0, MOE_J1 + up_j(s, j, e, r, n))),
                pl.BlockSpec((None, 1, MOE_TN), lambda s, j, e, r, n: (e[s], 0, up_j(s, j, e, r, n))),
                pl.BlockSpec((None, 1, MOE_TN), lambda s, j, e, r, n: (e[s], 0, MOE_J1 + up_j(s, j, e, r, n))),
                pl.BlockSpec((None, D_FF, MOE_TN), lambda s, j, e, r, n: (e[s], 0, dn_j(s, j, e, r, n))),
                pl.BlockSpec((None, 1, MOE_TN), lambda s, j, e, r, n: (e[s], 0, dn_j(s, j, e, r, n))),
            ],
            out_specs=pl.BlockSpec(memory_space=pl.ANY),
            scratch_shapes=[pltpu.VMEM((MOE_TS * ROW_TILE, LANES), F32),
                            pltpu.VMEM((MOE_TS, D_MODEL), BF16),
                            pltpu.VMEM((MOE_J1, MOE_TS, MOE_TN), BF16),
                            pltpu.SemaphoreType.DMA(()),
                            pltpu.SemaphoreType.DMA(())]),
        out_shape=jax.ShapeDtypeStruct((MOE_ROWS * ROW_TILE, LANES), F32),
        input_output_aliases={3: 0},
        compiler_params=_cparams(2, 58),
        name="moe_experts",
    )(sup_e, sup_row, sup_n, xs_rows, w_up, w_up, b_up3, b_up3, w_down, b_down3)


COMB_TM = 512
COMB_TN = 512
COMB_J = D_MODEL // COMB_TN


def _combine_body(dest_ref, yb_hbm, h_ref, gt_ref, gp_ref, wg_ref, p_ref, wu_ref, gf_ref, o_ref,
                  gath_ref, n3_ref, sem):
    tm = COMB_TM
    i = pl.program_id(0)
    j = pl.program_id(1)

    @pl.when(j == 0)
    def _():
        base = i * tm

        def issue(r, carry):
            for k in range(TOP_K):
                _row_copy(yb_hbm, dest_ref[(base + r) * TOP_K + k], gath_ref, k * tm + r, sem).start()
            return carry

        lax.fori_loop(0, tm, issue, 0)

        pltpu.make_async_copy(yb_hbm.at[pl.ds(0, TOP_K * tm * ROW_TILE)], gath_ref, sem).wait()

        gt = gt_ref[...]
        ssq = jnp.zeros((tm, 1), F32)
        for cidx in range(ROW_TILE):
            cols = slice(cidx * LANES, (cidx + 1) * LANES)
            acc = h_ref[:, cols]
            for k in range(TOP_K):
                acc = acc + gt[:, k:k + 1] * gath_ref[pl.ds(k * tm * ROW_TILE + cidx, tm, stride=ROW_TILE), :]
            o_ref[:, cols] = acc
            ssq = ssq + jnp.sum(acc * acc, axis=-1, keepdims=True)
        scale = lax.rsqrt(ssq / D_MODEL + EPS)
        for q in range(COMB_J):
            cols = slice(q * COMB_TN, (q + 1) * COMB_TN)
            n3_ref[:, cols] = (o_ref[:, cols] * scale * gp_ref[:, cols]).astype(BF16)

    for q in range(COMB_J):
        @pl.when(j == q)
        def _():
            cols = slice(q * COMB_TN, (q + 1) * COMB_TN)
            gate = _sigmoid(jnp.dot(n3_ref[...], wg_ref[...].astype(BF16), preferred_element_type=F32))
            up = jnp.dot(p_ref[...].astype(BF16), wu_ref[...].astype(BF16), preferred_element_type=F32)
            o_ref[:, cols] = o_ref[:, cols] + gate * up

    @pl.when(j == COMB_J - 1)
    def _():
        ssq = jnp.zeros((tm, 1), F32)
        for q in range(COMB_J):
            v = o_ref[:, q * COMB_TN:(q + 1) * COMB_TN]
            ssq = ssq + jnp.sum(v * v, axis=-1, keepdims=True)
        scale = lax.rsqrt(ssq / D_MODEL + EPS)
        for q in range(COMB_J):
            cols = slice(q * COMB_TN, (q + 1) * COMB_TN)
            o_ref[:, cols] = o_ref[:, cols] * scale * gf_ref[:, cols]


def _combine(dest_flat, yb_rows, h, gates, g_ple, w_ple_gate, p_all, w_ple_up, g_final):
    tm = COMB_TM
    return pl.pallas_call(
        _combine_body,
        grid_spec=pltpu.PrefetchScalarGridSpec(
            num_scalar_prefetch=1,
            grid=(N_TOK // tm, COMB_J),
            in_specs=[
                pl.BlockSpec(memory_space=pl.ANY),
                pl.BlockSpec((tm, D_MODEL), lambda i, j, d: (i, 0)),
                pl.BlockSpec((tm, LANES), lambda i, j, d: (i, 0)),
                pl.BlockSpec((1, D_MODEL), lambda i, j, d: (0, 0)),
                pl.BlockSpec((D_MODEL, COMB_TN), lambda i, j, d: (0, j)),
                pl.BlockSpec((tm, D_PLE), lambda i, j, d: (i, 0)),
                pl.BlockSpec((D_PLE, COMB_TN), lambda i, j, d: (0, j)),
                pl.BlockSpec((1, D_MODEL), lambda i, j, d: (0, 0)),
            ],
            out_specs=pl.BlockSpec((tm, D_MODEL), lambda i, j, d: (i, 0)),
            scratch_shapes=[pltpu.VMEM((TOP_K * tm * ROW_TILE, LANES), F32),
                            pltpu.VMEM((tm, D_MODEL), BF16),
                            pltpu.SemaphoreType.DMA(())]),
        out_shape=jax.ShapeDtypeStruct((N_TOK, D_MODEL), F32),
        compiler_params=_cparams(2, 56),
        name="combine_ple_norm",
    )(dest_flat, yb_rows, h, gates, g_ple, w_ple_gate, p_all, w_ple_up, g_final)


def _routing_tables(counts, eidx, rank):
    nsub_e = (counts + MOE_SUB - 1) // MOE_SUB
    prow_e = (jnp.cumsum(nsub_e) - nsub_e) * MOE_SUB
    dest = (prow_e[eidx.T] + rank.T).T.reshape(-1).astype(I32)
    nsup_e = (nsub_e + MOE_NSB - 1) // MOE_NSB
    csup = jnp.cumsum(nsup_e)
    total = csup[-1]
    s = jnp.arange(MOE_SMAX, dtype=I32)
    e_raw = jnp.minimum(jnp.searchsorted(csup, s, side="right").astype(I32), N_EXPERTS - 1)
    valid = s < total
    e_last = e_raw[jnp.maximum(total - 1, 0)]
    e_s = jnp.where(valid, e_raw, e_last)
    local = s - (csup[e_s] - nsup_e[e_s])
    row_s = jnp.where(valid, prow_e[e_s] + local * MOE_TS, 0)
    n_s = jnp.where(valid, jnp.clip(nsub_e[e_s] - local * MOE_NSB, 0, MOE_NSB), 0)
    return dest, e_s.astype(I32), row_s.astype(I32), n_s.astype(I32)


def _pad_prev(state, width):
    b, _, c = state.shape
    return jnp.pad(state, ((0, 0), (SUBLANES - (width - 1), 0), (0, 0))).reshape(b * SUBLANES, c)


def kernel(x_prompt, x_sample, p_prompt, p_sample, state_conv_a, state_conv_ssm, state_ssm, g_mix, w_in, conv_a_w, ssm_conv_w, ssm_conv_b, dt_bias, a_log, d_skip, ssm_norm_g, w_branch_a, w_branch_b, w_out, g_ffn, w_router, b_router, w_up, b_up, w_down, b_down, g_ple, w_ple_gate, w_ple_up, g_final):
    xp = x_prompt.reshape(NP_TOK, D_MODEL)
    xs = x_sample.reshape(NS_TOK, D_MODEL)
    p_all = jnp.concatenate([p_prompt[0].reshape(NP_TOK, D_PLE), p_sample[0].reshape(NS_TOK, D_PLE)], axis=0)
    g_mix2 = g_mix[0].reshape(1, D_MODEL)

    w_in_t = jnp.swapaxes(w_in, 1, 2).reshape(-1, D_MODEL)
    tn = 512
    proj = _rms_matmul(xp, xs, g_mix2, w_in_t,
                       lambda j: j * tn + (j // (N_MAIN // tn)) * (COL_GATE - N_MAIN),
                       N_PROJ, 1024, tn, "in_proj")
    dt_raw = _rms_matmul(xp, xs, g_mix2, w_in_t, lambda j: COL_DT + j * SSM_HEADS, SSM_HEADS, 1024, SSM_HEADS,
                         "in_proj_dt")

    ya_p, ua_tail = _conv_a_prompt(proj, conv_a_w[0])
    ya_s, ua_s = _conv_a_sample(proj, conv_a_w[0], _pad_prev(state_conv_a[0], CONV_A_W))
    conv_a_prompt = ua_tail[:, SUBLANES - (CONV_A_W - 1):, :]
    conv_a_sample = ua_s.reshape(DEC_BATCH, DEC_SEQ, D_CONV)[:, DEC_SEQ - (CONV_A_W - 1):, :]

    conv_b2 = ssm_conv_b[0].reshape(1, D_XBC)
    xbc_p = _conv_m_prompt(proj, ssm_conv_w[0], conv_b2)
    xbc_s = _conv_m_sample(proj, ssm_conv_w[0], conv_b2, _pad_prev(state_conv_ssm[0], SSM_CONV_W))
    proj3 = proj.reshape(N_TOK // DEC_SEQ, DEC_SEQ, N_PROJ)
    t0 = DEC_SEQ - (SSM_CONV_W - 1)
    conv_m_prompt = proj3[SEQ // DEC_SEQ - 1:NP_TOK // DEC_SEQ:SEQ // DEC_SEQ, t0:, COL_XBC:COL_XBC + D_XBC]
    conv_m_sample = proj3[NP_TOK // DEC_SEQ:, t0:, COL_XBC:COL_XBC + D_XBC]

    ssd_w = (dt_bias[0].reshape(1, SSM_HEADS), dt_bias[0].reshape(SSM_HEADS, 1),
             a_log[0].reshape(1, SSM_HEADS), a_log[0].reshape(SSM_HEADS, 1),
             jnp.repeat(d_skip[0], SSM_HEAD_DIM).reshape(1, D_INNER), ssm_norm_g[0].reshape(1, D_INNER))
    ym_p, ssm_p = _ssd_call(xbc_p, proj, dt_raw, *ssd_w)
    ym_s, ssm_s = _ssd_step_call(xbc_s, proj, dt_raw, *ssd_w, state_ssm)
    ssm_shape = (SSM_HEADS, SSM_HEAD_DIM, SSM_STATE)

    ya = jnp.concatenate([ya_p, ya_s], axis=0)
    ym = jnp.concatenate([ym_p, ym_s.astype(BF16)], axis=0)
    merged = _merge(ya, ym, w_branch_a[0], w_branch_b[0], proj)
    h = _resid_matmul(merged, w_out[0], xp, xs)

    wr_pad = jnp.pad(w_router[0], ((0, 0), (0, LANES - N_EXPERTS)))
    br_pad = jnp.pad(b_router[0].reshape(1, N_EXPERTS), ((0, 0), (0, LANES - N_EXPERTS)), constant_values=-1e30)
    g_ffn2 = g_ffn[0].reshape(1, D_MODEL)
    eidx, gate_w, rank, cnt = _router(h, g_ffn2, wr_pad, br_pad)
    counts = cnt[0, :N_EXPERTS].astype(I32)
    dest, sup_e, sup_row, sup_n = _routing_tables(counts, eidx[:, :TOP_K], rank[:, :TOP_K])
    xs_rows = _dispatch(dest, h, g_ffn2)
    yb_rows = _experts(sup_e, sup_row, sup_n, xs_rows, w_up[0], b_up[0], w_down[0], b_down[0])

    y_all = _combine(dest, yb_rows, h, gate_w, g_ple[0].reshape(1, D_MODEL), w_ple_gate[0], p_all,
                     w_ple_up[0], g_final.reshape(1, D_MODEL))
    y_prompt = y_all[:NP_TOK].reshape(BATCH, SEQ, D_MODEL)
    y_sample = y_all[NP_TOK:].reshape(DEC_BATCH, DEC_SEQ, D_MODEL)
    return (y_prompt, y_sample,
            conv_a_prompt[None], conv_m_prompt[None], ssm_p.reshape((1, BATCH) + ssm_shape),
            conv_a_sample[None], conv_m_sample[None], ssm_s)
```

```python
import functools

import jax
import jax.numpy as jnp
from jax import lax
from jax.experimental import pallas as pl
from jax.experimental.pallas import tpu as pltpu

F32 = jnp.float32
BF16 = jnp.bfloat16
I32 = jnp.int32
HI = lax.Precision.HIGHEST

D_MODEL = 2048
BATCH = 4
SEQ = 2048
DEC_BATCH = 128
DEC_SEQ = 8
D_PLE = 256
D_CONV = D_MODEL
CONV_A_W = 3
D_INNER = 2 * D_MODEL
SSM_HEAD_DIM = 64
SSM_HEADS = D_INNER // SSM_HEAD_DIM
SSM_GROUPS = 8
SSM_STATE = 128
SSM_CONV_W = 4
SSM_CHUNK = 128
D_XBC = D_INNER + 2 * SSM_GROUPS * SSM_STATE
N_EXPERTS = 32
TOP_K = 4
D_FF = D_MODEL
SWIGLU_LIMIT = 7.0
SWIGLU_ALPHA = 1.702
EPS = 1e-6

NP_TOK = BATCH * SEQ
NS_TOK = DEC_BATCH * DEC_SEQ
N_TOK = NP_TOK + NS_TOK
N_ASSIGN = N_TOK * TOP_K

LANES = 128
SUBLANES = 8
ROW_TILE = D_MODEL // LANES

COL_AX = 0
COL_AB = D_CONV
COL_AC = 2 * D_CONV
COL_Z = 3 * D_CONV
COL_XBC = COL_Z + D_INNER
COL_DT = COL_XBC + D_XBC
COL_GATE = COL_DT + SSM_HEADS
N_MAIN = COL_DT
N_PROJ = N_MAIN + 2 * D_MODEL

MOE_SUB = 256
MOE_TS = 1280
MOE_NSB = MOE_TS // MOE_SUB
MOE_CHUNK = MOE_TS // 2
MOE_ROWS = N_ASSIGN + N_EXPERTS * MOE_SUB
MOE_SMAX = N_ASSIGN // MOE_TS + N_EXPERTS
MOE_TN = 512
MOE_J1 = D_FF // MOE_TN
MOE_J2 = D_MODEL // MOE_TN

MIB = 1024 * 1024


def _cparams(n_grid, vmem_mib):
    return pltpu.CompilerParams(dimension_semantics=("arbitrary",) * n_grid,
                                vmem_limit_bytes=vmem_mib * MIB)


def _sigmoid(x):
    return 1.0 / (1.0 + jnp.exp(-x))


def _softplus(x):
    return jnp.maximum(x, 0.0) + jnp.log1p(jnp.exp(-jnp.abs(x)))


def _group_specs(tm, width, n_grid):
    npt = NP_TOK // tm
    if n_grid == 1:
        return (pl.BlockSpec((tm, width), lambda i: (jnp.minimum(i, npt - 1), 0)),
                pl.BlockSpec((tm, width), lambda i: (jnp.maximum(i - npt, 0), 0)))
    return (pl.BlockSpec((tm, width), lambda i, j, *_: (jnp.minimum(i, npt - 1), 0)),
            pl.BlockSpec((tm, width), lambda i, j, *_: (jnp.maximum(i - npt, 0), 0)))


def _rms_mm_body(npt, xp_ref, xs_ref, g_ref, w_ref, o_ref, n_ref):
    i = pl.program_id(0)
    first = pl.program_id(1) == 0

    def norm(x_ref):
        x = x_ref[...]
        ms = jnp.mean(x * x, axis=-1, keepdims=True)
        n_ref[...] = (x * lax.rsqrt(ms + EPS) * g_ref[...]).astype(BF16)

    @pl.when(jnp.logical_and(first, i < npt))
    def _():
        norm(xp_ref)

    @pl.when(jnp.logical_and(first, i >= npt))
    def _():
        norm(xs_ref)

    o_ref[...] = lax.dot_general(n_ref[...], w_ref[...].astype(BF16), (((1,), (1,)), ((), ())),
                                 preferred_element_type=F32)


def _rms_matmul(xp, xs, g, w_t, feature_offset, ncols, tm, tn, name):
    k = xp.shape[1]
    xp_spec, xs_spec = _group_specs(tm, k, 2)
    return pl.pallas_call(
        functools.partial(_rms_mm_body, NP_TOK // tm),
        grid=(N_TOK // tm, ncols // tn),
        in_specs=[xp_spec, xs_spec,
                  pl.BlockSpec((1, k), lambda i, j: (0, 0)),
                  pl.BlockSpec((pl.Element(tn), pl.Element(k)),
                               lambda i, j: (pl.multiple_of(feature_offset(j), SUBLANES), 0))],
        out_specs=pl.BlockSpec((tm, tn), lambda i, j: (i, j)),
        out_shape=jax.ShapeDtypeStruct((N_TOK, ncols), F32),
        scratch_shapes=[pltpu.VMEM((tm, k), BF16)],
        compiler_params=_cparams(2, 56),
        name=name,
    )(xp, xs, g, w_t)


def _causal_taps(u, w_ref, width, rowmod, prev):
    rows = u.shape[0]
    acc = u * w_ref[width - 1:width, :]
    for s in range(1, width):
        shifted = pltpu.roll(u, s, 0)
        if prev is None:
            fill = jnp.zeros_like(u)
        else:
            fill = pltpu.roll(prev, rows - SUBLANES + s, 0)
        acc = acc + jnp.where(rowmod >= s, shifted, fill) * w_ref[width - 1 - s:width - s, :]
    return acc


def _row_in_seq(shape, period):
    r = lax.broadcasted_iota(I32, shape, 0)
    return r if period >= shape[0] else r & (period - 1)


def _conv_a_body(period, has_prev, *refs):
    if has_prev:
        ax_ref, ab_ref, ac_ref, w_ref, prev_ref, ya_ref, u_ref = refs
        prev = prev_ref[...]
    else:
        ax_ref, ab_ref, ac_ref, w_ref, ya_ref, u_ref = refs
        prev = None
    u = ac_ref[...] * ax_ref[...]
    v = _causal_taps(u, w_ref, CONV_A_W, _row_in_seq(u.shape, period), prev)
    ya_ref[...] = (ab_ref[...] * v).astype(BF16)
    if has_prev:
        u_ref[...] = u
    else:
        u_ref[...] = u[u.shape[0] - SUBLANES:, :]


def _conv_m_body(period, has_prev, *refs):
    if has_prev:
        x_ref, w_ref, b_ref, prev_ref, o_ref = refs
        prev = prev_ref[...]
    else:
        x_ref, w_ref, b_ref, o_ref = refs
        prev = None
    u = x_ref[...]
    v = _causal_taps(u, w_ref, SSM_CONV_W, _row_in_seq(u.shape, period), prev) + b_ref[...]
    o_ref[...] = v * _sigmoid(v)


CONV_TC = 256


def _conv_a_prompt(proj, conv_w):
    nb = D_CONV // CONV_TC
    blk = lambda off: pl.BlockSpec((SEQ, CONV_TC), lambda b, c, off=off: (b, off // CONV_TC + c))
    return pl.pallas_call(
        functools.partial(_conv_a_body, SEQ, False),
        grid=(BATCH, nb),
        in_specs=[blk(COL_AX), blk(COL_AB), blk(COL_AC),
                  pl.BlockSpec((CONV_A_W, CONV_TC), lambda b, c: (0, c))],
        out_specs=[pl.BlockSpec((SEQ, CONV_TC), lambda b, c: (b, c)),
                   pl.BlockSpec((None, SUBLANES, CONV_TC), lambda b, c: (b, 0, c))],
        out_shape=[jax.ShapeDtypeStruct((NP_TOK, D_CONV), BF16),
                   jax.ShapeDtypeStruct((BATCH, SUBLANES, D_CONV), F32)],
        compiler_params=_cparams(2, 48),
        name="conv_a_prompt",
    )(proj, proj, proj, conv_w)


def _conv_a_sample(proj, conv_w, prev_pad):
    nb = D_CONV // CONV_TC
    rb = NP_TOK // NS_TOK
    blk = lambda off: pl.BlockSpec((NS_TOK, CONV_TC), lambda c, off=off: (rb, off // CONV_TC + c))
    own = pl.BlockSpec((NS_TOK, CONV_TC), lambda c: (0, c))
    return pl.pallas_call(
        functools.partial(_conv_a_body, DEC_SEQ, True),
        grid=(nb,),
        in_specs=[blk(COL_AX), blk(COL_AB), blk(COL_AC),
                  pl.BlockSpec((CONV_A_W, CONV_TC), lambda c: (0, c)), own],
        out_specs=[own, own],
        out_shape=[jax.ShapeDtypeStruct((NS_TOK, D_CONV), BF16),
                   jax.ShapeDtypeStruct((NS_TOK, D_CONV), F32)],
        compiler_params=_cparams(1, 48),
        name="conv_a_sample",
    )(proj, proj, proj, conv_w, prev_pad)


def _conv_m_prompt(proj, conv_w, conv_b):
    nb = D_XBC // CONV_TC
    return pl.pallas_call(
        functools.partial(_conv_m_body, SEQ, False),
        grid=(BATCH, nb),
        in_specs=[pl.BlockSpec((SEQ, CONV_TC), lambda b, c: (b, COL_XBC // CONV_TC + c)),
                  pl.BlockSpec((SSM_CONV_W, CONV_TC), lambda b, c: (0, c)),
                  pl.BlockSpec((1, CONV_TC), lambda b, c: (0, c))],
        out_specs=pl.BlockSpec((SEQ, CONV_TC), lambda b, c: (b, c)),
        out_shape=jax.ShapeDtypeStruct((NP_TOK, D_XBC), F32),
        compiler_params=_cparams(2, 48),
        name="conv_m_prompt",
    )(proj, conv_w, conv_b)


def _conv_m_sample(proj, conv_w, conv_b, prev_pad):
    nb = D_XBC // CONV_TC
    rb = NP_TOK // NS_TOK
    own = pl.BlockSpec((NS_TOK, CONV_TC), lambda c: (0, c))
    return pl.pallas_call(
        functools.partial(_conv_m_body, DEC_SEQ, True),
        grid=(nb,),
        in_specs=[pl.BlockSpec((NS_TOK, CONV_TC), lambda c: (rb, COL_XBC // CONV_TC + c)),
                  pl.BlockSpec((SSM_CONV_W, CONV_TC), lambda c: (0, c)),
                  pl.BlockSpec((1, CONV_TC), lambda c: (0, c)), own],
        out_specs=own,
        out_shape=jax.ShapeDtypeStruct((NS_TOK, D_XBC), F32),
        compiler_params=_cparams(1, 48),
        name="conv_m_sample",
    )(proj, conv_w, conv_b, prev_pad)


def _ssd_body(xbc_ref, zlo_ref, zhi_ref, dtr_ref, dtb_ref, dtbc_ref, alog_ref, alogc_ref, dsk_ref, ng_ref,
              ym_ref, hl_ref, ht_ref, y_ref):
    L = SSM_CHUNK
    H = SSM_HEADS
    c = pl.program_id(1)

    @pl.when(c == 0)
    def _():
        ht_ref[...] = jnp.zeros_like(ht_ref)

    dtr = dtr_ref[...]
    dt = _softplus(dtr + dtb_ref[...])
    eye = (lax.broadcasted_iota(I32, (H, H), 0) == lax.broadcasted_iota(I32, (H, H), 1)).astype(F32)
    dtr_t = lax.dot_general(eye, dtr, (((1,), (1,)), ((), ())), precision=HI,
                            preferred_element_type=F32)
    dt_t = _softplus(dtr_t + dtbc_ref[...])
    adt = dt * (-jnp.exp(alog_ref[...]))
    adt_t = dt_t * (-jnp.exp(alogc_ref[...]))
    li = lax.broadcasted_iota(I32, (L, L), 0)
    si = lax.broadcasted_iota(I32, (L, L), 1)
    causal = li >= si
    a_cum = jnp.dot(causal.astype(F32), adt, precision=HI, preferred_element_type=F32)
    a_cum_t = jnp.dot(adt_t, (li <= si).astype(F32), precision=HI, preferred_element_type=F32)
    ea = jnp.exp(a_cum)
    a_tot_t = a_cum_t[:, L - 1:L]
    dte_t = jnp.exp(a_tot_t - a_cum_t)
    ea_tot_t = jnp.exp(a_tot_t)

    lane = lax.broadcasted_iota(I32, (L, LANES), 1)
    lo = lane < SSM_HEAD_DIM
    lane1 = lax.broadcasted_iota(I32, (1, LANES), 1) < SSM_HEAD_DIM
    gn = SSM_GROUPS * SSM_STATE
    for g in range(SSM_GROUPS):
        bg = xbc_ref[:, D_INNER + g * SSM_STATE:D_INNER + (g + 1) * SSM_STATE]
        cg = xbc_ref[:, D_INNER + gn + g * SSM_STATE:D_INNER + gn + (g + 1) * SSM_STATE]
        bg_t = bg.T
        cb = jnp.dot(cg.astype(BF16), bg_t.astype(BF16), preferred_element_type=F32)
        for jj in range(SSM_HEADS // SSM_GROUPS // 2):
            j = g * (SSM_HEADS // SSM_GROUPS // 2) + jj
            cols = slice(j * LANES, (j + 1) * LANES)
            lhs1, lhs2 = [], []
            for h in (2 * j, 2 * j + 1):
                diff = a_cum[:, h:h + 1] - a_cum_t[h:h + 1, :]
                decay = jnp.exp(jnp.where(causal, diff, -1e30))
                lhs1.append((cb * decay).astype(BF16))
            for h in (2 * j, 2 * j + 1):
                lhs1.append((cg * ea[:, h:h + 1]).astype(BF16))
                lhs2.append((bg_t * dte_t[h:h + 1, :]).astype(BF16))
            dt_pair = jnp.where(lo, dt[:, 2 * j:2 * j + 1], dt[:, 2 * j + 1:2 * j + 2])
            xdt = xbc_ref[:, cols] * dt_pair
            zero = jnp.zeros_like(xdt)
            x_top = jnp.where(lo, xdt, zero).astype(BF16)
            x_bot = jnp.where(lo, zero, xdt).astype(BF16)
            ht = ht_ref[:, cols]
            h_top = jnp.where(lo, ht, zero).astype(BF16)
            h_bot = jnp.where(lo, zero, ht).astype(BF16)
            y_ref[:, cols] = jnp.dot(jnp.concatenate(lhs1, axis=1),
                                     jnp.concatenate([x_top, x_bot, h_top, h_bot], axis=0),
                                     preferred_element_type=F32)
            upd = jnp.dot(jnp.concatenate(lhs2, axis=1), jnp.concatenate([x_top, x_bot], axis=0),
                          preferred_element_type=F32)
            keep = jnp.where(lane1, ea_tot_t[2 * j:2 * j + 1, :], ea_tot_t[2 * j + 1:2 * j + 2, :])
            ht_ref[:, cols] = ht * keep + upd

    gw = D_INNER // SSM_GROUPS
    for g in range(SSM_GROUPS):
        cols = slice(g * gw, (g + 1) * gw)
        zr = zlo_ref if g < SSM_GROUPS // 2 else zhi_ref
        zc = slice((g % (SSM_GROUPS // 2)) * gw, (g % (SSM_GROUPS // 2) + 1) * gw)
        z = zr[:, zc]
        y = y_ref[:, cols] + dsk_ref[:, cols] * xbc_ref[:, cols]
        y = y * (z * _sigmoid(z))
        y = y * lax.rsqrt(jnp.mean(y * y, axis=-1, keepdims=True) + EPS)
        ym_ref[:, cols] = (y * ng_ref[:, cols]).astype(ym_ref.dtype)

    @pl.when(c == pl.num_programs(1) - 1)
    def _():
        hl_ref[...] = ht_ref[...].T


def _ssd_call(xbc, proj, dt_raw, dtb, dtbc, alog, alogc, dsk, ng):
    L = SSM_CHUNK
    n_chunk = SEQ // L
    hp = SSM_HEADS * SSM_HEAD_DIM
    zb = D_INNER // 2
    vec = lambda shape: pl.BlockSpec(shape, lambda b, c: (0, 0))
    return pl.pallas_call(
        _ssd_body,
        grid=(BATCH, n_chunk),
        in_specs=[pl.BlockSpec((L, D_XBC), lambda b, c: (b * n_chunk + c, 0)),
                  pl.BlockSpec((L, zb), lambda b, c: (b * n_chunk + c, COL_Z // zb)),
                  pl.BlockSpec((L, zb), lambda b, c: (b * n_chunk + c, COL_Z // zb + 1)),
                  pl.BlockSpec((L, SSM_HEADS), lambda b, c: (b * n_chunk + c, 0)),
                  vec((1, SSM_HEADS)), vec((SSM_HEADS, 1)), vec((1, SSM_HEADS)), vec((SSM_HEADS, 1)),
                  vec((1, D_INNER)), vec((1, D_INNER))],
        out_specs=[pl.BlockSpec((L, D_INNER), lambda b, c: (b * n_chunk + c, 0)),
                   pl.BlockSpec((None, hp, SSM_STATE), lambda b, c: (b, 0, 0))],
        out_shape=[jax.ShapeDtypeStruct((NP_TOK, D_INNER), BF16),
                   jax.ShapeDtypeStruct((BATCH, hp, SSM_STATE), F32)],
        scratch_shapes=[pltpu.VMEM((SSM_STATE, hp), F32), pltpu.VMEM((L, hp), F32)],
        compiler_params=_cparams(2, 48),
        name="ssd_prompt",
    )(xbc, proj, proj, dt_raw, dtb, dtbc, alog, alogc, dsk, ng)


SSD_STEP_SEQS = 2


def _ssd_step_body(xbc_in, zlo_ref, zhi_ref, dtr_in, dtb_ref, dtbc_ref, alog_ref, alogc_ref, dsk_ref, ng_ref,
                   h0_ref, ym_ref, hl_ref, xp_ref, dp_ref):
    T = DEC_SEQ

    @pl.when(pl.program_id(0) == 0)
    def _():
        xp_ref[...] = jnp.zeros_like(xp_ref)
        dp_ref[...] = jnp.zeros_like(dp_ref)

    for s in range(SSD_STEP_SEQS):
        rows = pl.ds(s * T, T)
        _ssd_step_one(xbc_in.at[rows], zlo_ref.at[rows], zhi_ref.at[rows], dtr_in.at[rows], dtb_ref, dtbc_ref,
                      alog_ref, alogc_ref, dsk_ref, ng_ref, h0_ref.at[s], ym_ref.at[rows], hl_ref.at[s],
                      xp_ref.at[s], dp_ref.at[s])


def _ssd_step_one(xbc_in, zlo_ref, zhi_ref, dtr_in, dtb_ref, dtbc_ref, alog_ref, alogc_ref, dsk_ref, ng_ref,
                  h0_ref, ym_ref, hl_ref, xp_ref, dp_ref):
    T = DEC_SEQ
    L = SSM_CHUNK
    H = SSM_HEADS
    P = SSM_HEAD_DIM

    xp_ref[0:T, :] = xbc_in[...]
    dp_ref[0:T, :] = dtr_in[...]

    dt8 = _softplus(dtr_in[...] + dtb_ref[...])
    a_cum8 = dt8 * (-jnp.exp(alog_ref[...]))
    row8 = lax.broadcasted_iota(I32, (T, H), 0)
    k = 1
    while k < T:
        a_cum8 = a_cum8 + jnp.where(row8 >= k, pltpu.roll(a_cum8, k, 0), 0.0)
        k *= 2
    ea8 = jnp.exp(a_cum8)
    dte8 = jnp.exp(a_cum8[T - 1:T, :] - a_cum8)
    eye = (lax.broadcasted_iota(I32, (H, H), 0) == lax.broadcasted_iota(I32, (H, H), 1)).astype(F32)
    dtr_t = lax.dot_general(eye, dp_ref[...], (((1,), (1,)), ((), ())), precision=HI,
                            preferred_element_type=F32)
    dt_t = jnp.where(lax.broadcasted_iota(I32, (H, L), 1) < T, _softplus(dtr_t + dtbc_ref[...]), 0.0)
    adt_t = dt_t * (-jnp.exp(alogc_ref[...]))
    upper = (lax.broadcasted_iota(I32, (L, L), 0) <= lax.broadcasted_iota(I32, (L, L), 1)).astype(F32)
    a_cum_t = jnp.dot(adt_t, upper, precision=HI, preferred_element_type=F32)
    ea_tot_t = jnp.exp(a_cum_t[:, L - 1:L])

    causal8 = lax.broadcasted_iota(I32, (T, L), 0) >= lax.broadcasted_iota(I32, (T, L), 1)
    lo8 = lax.broadcasted_iota(I32, (T, LANES), 1) < P
    zeros_t = jnp.zeros((T, LANES), F32)
    zeros_pad = jnp.zeros((L - T, LANES), F32)
    gn = SSM_GROUPS * SSM_STATE
    gw = D_INNER // SSM_GROUPS
    pairs = SSM_HEADS // SSM_GROUPS // 2
    for g in range(SSM_GROUPS):
        bgp = xp_ref[:, D_INNER + g * SSM_STATE:D_INNER + (g + 1) * SSM_STATE].astype(BF16)
        cg8 = xbc_in[:, D_INNER + gn + g * SSM_STATE:D_INNER + gn + (g + 1) * SSM_STATE]
        cg16 = jnp.concatenate([cg8, zeros_t], axis=0).astype(BF16)
        cb = lax.dot_general(cg16, bgp, (((1,), (1,)), ((), ())), preferred_element_type=F32)[0:T]
        ys = []
        for jj in range(pairs):
            j = g * pairs + jj
            ha, hb = 2 * j, 2 * j + 1
            cols = slice(j * LANES, (j + 1) * LANES)

            def decay(h):
                diff = a_cum8[:, h:h + 1] - a_cum_t[h:h + 1, :]
                return jnp.exp(jnp.where(causal8, diff, -1e30))

            m = jnp.concatenate([cb * decay(ha), cb * decay(hb)], axis=0).astype(BF16)
            cd = jnp.concatenate([cg8 * ea8[:, ha:ha + 1], cg8 * ea8[:, hb:hb + 1]], axis=0).astype(BF16)
            xdt8 = xbc_in[:, cols] * jnp.where(lo8, dt8[:, ha:ha + 1], dt8[:, hb:hb + 1])
            xd8 = xdt8 * jnp.where(lo8, dte8[:, ha:ha + 1], dte8[:, hb:hb + 1])
            xdt_pad = jnp.concatenate([xdt8, zeros_pad], axis=0).astype(BF16)
            xd_pad = jnp.concatenate([xd8, zeros_pad], axis=0).astype(BF16)
            hp = h0_ref[ha:hb + 1].reshape(2 * P, SSM_STATE)
            out = (jnp.dot(m, xdt_pad, preferred_element_type=F32)
                   + lax.dot_general(cd, hp.astype(BF16), (((1,), (1,)), ((), ())),
                                     preferred_element_type=F32))
            ys.append(jnp.where(lo8, out[0:T], out[T:2 * T]))
            upd = lax.dot_general(xd_pad, bgp, (((0,), (0,)), ((), ())), preferred_element_type=F32)
            keep = jnp.concatenate([jnp.broadcast_to(ea_tot_t[ha:ha + 1, :], (P, SSM_STATE)),
                                    jnp.broadcast_to(ea_tot_t[hb:hb + 1, :], (P, SSM_STATE))], axis=0)
            hl_ref[ha:hb + 1] = (hp * keep + upd).reshape(2, P, SSM_STATE)

        cols = slice(g * gw, (g + 1) * gw)
        zr = zlo_ref if g < SSM_GROUPS // 2 else zhi_ref
        zc = slice((g % (SSM_GROUPS // 2)) * gw, (g % (SSM_GROUPS // 2) + 1) * gw)
        z = zr[:, zc]
        y = jnp.concatenate(ys, axis=1) + dsk_ref[:, cols] * xbc_in[:, cols]
        y = y * (z * _sigmoid(z))
        y = y * lax.rsqrt(jnp.mean(y * y, axis=-1, keepdims=True) + EPS)
        ym_ref[:, cols] = y * ng_ref[:, cols]


def _ssd_step_call(xbc, proj, dt_raw, dtb, dtbc, alog, alogc, dsk, ng, state):
    ns = SSD_STEP_SEQS
    T = ns * DEC_SEQ
    rb0 = NP_TOK // T
    zb = D_INNER // 2
    st = pl.BlockSpec((None, ns, SSM_HEADS, SSM_HEAD_DIM, SSM_STATE), lambda b: (0, b, 0, 0, 0))
    vec = lambda shape: pl.BlockSpec(shape, lambda b: (0, 0))
    return pl.pallas_call(
        _ssd_step_body,
        grid=(DEC_BATCH // ns,),
        in_specs=[pl.BlockSpec((T, D_XBC), lambda b: (b, 0)),
                  pl.BlockSpec((T, zb), lambda b: (rb0 + b, COL_Z // zb)),
                  pl.BlockSpec((T, zb), lambda b: (rb0 + b, COL_Z // zb + 1)),
                  pl.BlockSpec((T, SSM_HEADS), lambda b: (rb0 + b, 0)),
                  vec((1, SSM_HEADS)), vec((SSM_HEADS, 1)), vec((1, SSM_HEADS)), vec((SSM_HEADS, 1)),
                  vec((1, D_INNER)), vec((1, D_INNER)), st],
        out_specs=[pl.BlockSpec((T, D_INNER), lambda b: (b, 0)), st],
        out_shape=[jax.ShapeDtypeStruct((NS_TOK, D_INNER), F32),
                   jax.ShapeDtypeStruct(state.shape, F32)],
        scratch_shapes=[pltpu.VMEM((ns, SSM_CHUNK, D_XBC), F32), pltpu.VMEM((ns, SSM_CHUNK, SSM_HEADS), F32)],
        compiler_params=_cparams(1, 40),
        name="ssd_sample",
    )(xbc, proj, proj, dt_raw, dtb, dtbc, alog, alogc, dsk, ng, state)


def _merge_body(ya_ref, ym_ref, wa_ref, wb_ref, ga_ref, gb_ref, o_ref):
    a = jnp.dot(ya_ref[...], wa_ref[...].astype(BF16), preferred_element_type=F32)
    b = jnp.dot(ym_ref[...], wb_ref[...].astype(BF16), preferred_element_type=F32)
    o_ref[...] = (_sigmoid(ga_ref[...]) * a + _sigmoid(gb_ref[...]) * b).astype(BF16)


def _merge(ya, ym, wa, wb, proj):
    tm, tn = 1024, 256
    g0 = N_MAIN // tn
    return pl.pallas_call(
        _merge_body,
        grid=(N_TOK // tm, D_MODEL // tn),
        in_specs=[pl.BlockSpec((tm, D_CONV), lambda i, j: (i, 0)),
                  pl.BlockSpec((tm, D_INNER), lambda i, j: (i, 0)),
                  pl.BlockSpec((D_CONV, tn), lambda i, j: (0, j)),
                  pl.BlockSpec((D_INNER, tn), lambda i, j: (0, j)),
                  pl.BlockSpec((tm, tn), lambda i, j: (i, g0 + j)),
                  pl.BlockSpec((tm, tn), lambda i, j: (i, g0 + D_MODEL // tn + j))],
        out_specs=pl.BlockSpec((tm, tn), lambda i, j: (i, j)),
        out_shape=jax.ShapeDtypeStruct((N_TOK, D_MODEL), BF16),
        compiler_params=_cparams(2, 52),
        name="merge",
    )(ya, ym, wa, wb, proj, proj)


def _resid_body(npt, m_ref, w_ref, xp_ref, xs_ref, o_ref):
    x = jnp.where(pl.program_id(0) < npt, xp_ref[...], xs_ref[...])
    o_ref[...] = x + jnp.dot(m_ref[...], w_ref[...].astype(BF16), preferred_element_type=F32)


def _resid_matmul(m, w, xp, xs):
    tm, tn = 1024, 512
    npt = NP_TOK // tm
    return pl.pallas_call(
        functools.partial(_resid_body, npt),
        grid=(N_TOK // tm, D_MODEL // tn),
        in_specs=[pl.BlockSpec((tm, D_MODEL), lambda i, j: (i, 0)),
                  pl.BlockSpec((D_MODEL, tn), lambda i, j: (0, j)),
                  pl.BlockSpec((tm, tn), lambda i, j: (jnp.minimum(i, npt - 1), j)),
                  pl.BlockSpec((tm, tn), lambda i, j: (jnp.maximum(i - npt, 0), j))],
        out_specs=pl.BlockSpec((tm, tn), lambda i, j: (i, j)),
        out_shape=jax.ShapeDtypeStruct((N_TOK, D_MODEL), F32),
        compiler_params=_cparams(2, 48),
        name="out_proj",
    )(m, w, xp, xs)


ROUTER_TM = 512


def _router_body(h_ref, g_ref, wr_ref, br_ref, ei_ref, gt_ref, rk_ref, cnt_ref, carry_ref):
    tm = ROUTER_TM

    @pl.when(pl.program_id(0) == 0)
    def _():
        carry_ref[...] = jnp.zeros_like(carry_ref)

    h = h_ref[...]
    n2 = h * lax.rsqrt(jnp.mean(h * h, axis=-1, keepdims=True) + EPS) * g_ref[...]
    logits = jnp.dot(n2, wr_ref[...], precision=HI, preferred_element_type=F32) + br_ref[...]

    lane = lax.broadcasted_iota(I32, (tm, LANES), 1).astype(F32)
    work = logits
    vals, idxs = [], []
    for _ in range(TOP_K):
        m = jnp.max(work, axis=-1, keepdims=True)
        idx = jnp.min(jnp.where(work == m, lane, float(LANES)), axis=-1, keepdims=True)
        vals.append(m)
        idxs.append(idx)
        work = jnp.where(lane == idx, -jnp.inf, work)
    ex = [jnp.exp(v - vals[0]) for v in vals]
    den = ex[0] + ex[1] + ex[2] + ex[3]
    sel = jnp.zeros((tm, LANES), F32)
    for idx in idxs:
        sel = sel + jnp.where(lane == idx, 1.0, 0.0)
    before = (lax.broadcasted_iota(I32, (tm, tm), 0) > lax.broadcasted_iota(I32, (tm, tm), 1))
    rank_e = jnp.dot(before.astype(BF16), sel.astype(BF16), preferred_element_type=F32) + carry_ref[0:1, :]
    carry_ref[...] = carry_ref[...] + jnp.sum(sel, axis=0, keepdims=True)
    ei = jnp.zeros((tm, LANES), F32)
    gt = jnp.zeros((tm, LANES), F32)
    rk = jnp.zeros((tm, LANES), F32)
    for k in range(TOP_K):
        rank_k = jnp.sum(jnp.where(lane == idxs[k], rank_e, 0.0), axis=-1, keepdims=True)
        here = lane == float(k)
        ei = jnp.where(here, idxs[k], ei)
        gt = jnp.where(here, ex[k] / den, gt)
        rk = jnp.where(here, rank_k, rk)
    ei_ref[...] = ei.astype(I32)
    gt_ref[...] = gt
    rk_ref[...] = rk.astype(I32)
    cnt_ref[...] = carry_ref[...]


def _router(h, g, wr_pad, br_pad):
    tm = ROUTER_TM
    tok = pl.BlockSpec((tm, LANES), lambda i: (i, 0))
    return pl.pallas_call(
        _router_body,
        grid=(N_TOK // tm,),
        in_specs=[pl.BlockSpec((tm, D_MODEL), lambda i: (i, 0)),
                  pl.BlockSpec((1, D_MODEL), lambda i: (0, 0)),
                  pl.BlockSpec((D_MODEL, LANES), lambda i: (0, 0)),
                  pl.BlockSpec((1, LANES), lambda i: (0, 0))],
        out_specs=[tok, tok, tok, pl.BlockSpec((SUBLANES, LANES), lambda i: (0, 0))],
        out_shape=[jax.ShapeDtypeStruct((N_TOK, LANES), I32),
                   jax.ShapeDtypeStruct((N_TOK, LANES), F32),
                   jax.ShapeDtypeStruct((N_TOK, LANES), I32),
                   jax.ShapeDtypeStruct((SUBLANES, LANES), F32)],
        scratch_shapes=[pltpu.VMEM((SUBLANES, LANES), F32)],
        compiler_params=_cparams(1, 48),
        name="router",
    )(h, g, wr_pad, br_pad)


DISPATCH_TOK = 512


def _row_slab(ref, row):
    start = row * ROW_TILE
    if not isinstance(start, int):
        start = pl.multiple_of(start, ROW_TILE)
    return ref.at[pl.ds(start, ROW_TILE)]


def _row_copy(src, src_row, dst, dst_row, sem):
    return pltpu.make_async_copy(_row_slab(src, src_row), _row_slab(dst, dst_row), sem)


def _dispatch_body(dest_ref, zmask_ref, h_ref, g_ref, xs_hbm, rows_ref, zero_ref, sem, zsem):
    tm = DISPATCH_TOK
    base = pl.program_id(0) * tm
    sub_rows = MOE_SUB * ROW_TILE

    @pl.when(pl.program_id(0) == 0)
    def _():
        zero_ref[...] = jnp.zeros_like(zero_ref)

        def zero_copy(b):
            return pltpu.make_async_copy(
                zero_ref, xs_hbm.at[pl.ds(pl.multiple_of(b * sub_rows, sub_rows), sub_rows)], zsem)

        def start(b, carry):
            @pl.when(zmask_ref[b] != 0)
            def _():
                zero_copy(b).start()
            return carry

        def wait(b, carry):
            @pl.when(zmask_ref[b] != 0)
            def _():
                zero_copy(b).wait()
            return carry

        lax.fori_loop(0, MOE_ROWS // MOE_SUB, start, 0)
        lax.fori_loop(0, MOE_ROWS // MOE_SUB, wait, 0)

    h = h_ref[...]
    n2 = h * lax.rsqrt(jnp.mean(h * h, axis=-1, keepdims=True) + EPS) * g_ref[...]
    for cidx in range(ROW_TILE):
        rows_ref[pl.ds(cidx, tm, stride=ROW_TILE), :] = n2[:, cidx * LANES:(cidx + 1) * LANES]

    def issue(i, carry):
        for k in range(TOP_K):
            _row_copy(rows_ref, i, xs_hbm, dest_ref[(base + i) * TOP_K + k], sem).start()
        return carry

    lax.fori_loop(0, tm, issue, 0)

    for k in range(TOP_K):
        pltpu.make_async_copy(rows_ref, xs_hbm.at[pl.ds(0, tm * ROW_TILE)], sem).wait()


def _dispatch(dest_flat, zmask, h, g):
    tm = DISPATCH_TOK
    return pl.pallas_call(
        _dispatch_body,
        grid_spec=pltpu.PrefetchScalarGridSpec(
            num_scalar_prefetch=2,
            grid=(N_TOK // tm,),
            in_specs=[pl.BlockSpec((tm, D_MODEL), lambda i, d, z: (i, 0)),
                      pl.BlockSpec((1, D_MODEL), lambda i, d, z: (0, 0))],
            out_specs=pl.BlockSpec(memory_space=pl.ANY),
            scratch_shapes=[pltpu.VMEM((tm * ROW_TILE, LANES), F32),
                            pltpu.VMEM((MOE_SUB * ROW_TILE, LANES), F32),
                            pltpu.SemaphoreType.DMA(()), pltpu.SemaphoreType.DMA(())]),
        out_shape=jax.ShapeDtypeStruct((MOE_ROWS * ROW_TILE, LANES), F32),
        compiler_params=_cparams(1, 32),
        name="moe_dispatch",
    )(dest_flat, zmask, h, g)


def _expert_body(e_ref, r_ref, n_ref, xs_hbm, wg_ref, wu_ref, bg_ref, bu_ref, wd_ref, bd_ref, yb_hbm,
                 stage_ref, xb_ref, act_ref, sem_in, sem_out):
    del e_ref
    s = pl.program_id(0)
    j = pl.program_id(1)
    nsub = n_ref[s]
    row0 = r_ref[s]
    sub_rows = MOE_SUB * ROW_TILE

    def in_copy(k):
        return pltpu.make_async_copy(
            xs_hbm.at[pl.ds(pl.multiple_of((row0 + k * MOE_SUB) * ROW_TILE, sub_rows), sub_rows)],
            stage_ref.at[pl.ds(k * sub_rows, sub_rows)], sem_in)

    def out_copy(k):
        return pltpu.make_async_copy(
            stage_ref.at[pl.ds(k * sub_rows, sub_rows)],
            yb_hbm.at[pl.ds(pl.multiple_of((row0 + k * MOE_SUB) * ROW_TILE, sub_rows), sub_rows)], sem_out)

    @pl.when(j == 0)
    def _():
        for k in range(MOE_NSB):
            @pl.when(k < nsub)
            def _():
                in_copy(k).start()
        for k in range(MOE_NSB):
            @pl.when(k < nsub)
            def _():
                in_copy(k).wait()
        for k in range(MOE_NSB):
            @pl.when(k < nsub)
            def _():
                for cidx in range(ROW_TILE):
                    xb_ref[k * MOE_SUB:(k + 1) * MOE_SUB, cidx * LANES:(cidx + 1) * LANES] = (
                        stage_ref[pl.ds(k * sub_rows + cidx, MOE_SUB, stride=ROW_TILE), :].astype(BF16))

    full = nsub == MOE_NSB

    @pl.when(j < MOE_J1)
    def _():
        wg = wg_ref[...].astype(BF16)
        wu = wu_ref[...].astype(BF16)

        def up_rows(r0, nrows):
            x = xb_ref[r0:r0 + nrows, :]
            gate = jnp.dot(x, wg, preferred_element_type=F32) + bg_ref[...]
            up = jnp.dot(x, wu, preferred_element_type=F32) + bu_ref[...]
            gate = jnp.minimum(gate, SWIGLU_LIMIT)
            up = jnp.clip(up, -SWIGLU_LIMIT, SWIGLU_LIMIT)
            a = gate * _sigmoid(SWIGLU_ALPHA * gate) * (up + 1.0)
            act_ref[jnp.minimum(j, MOE_J1 - 1), r0:r0 + nrows, :] = a.astype(BF16)

        @pl.when(full)
        def _():
            for r0 in range(0, MOE_TS, MOE_CHUNK):
                up_rows(r0, MOE_CHUNK)

        for k in range(MOE_NSB - 1):
            @pl.when(jnp.logical_and(jnp.logical_not(full), k < nsub))
            def _():
                up_rows(k * MOE_SUB, MOE_SUB)

    per = MOE_TN // LANES
    for jd in range(MOE_J2):
        @pl.when(j == MOE_J1 + jd)
        def _():
            wd = wd_ref[...].astype(BF16)

            def down_rows(r0, nrows):
                y = bd_ref[...] + jnp.dot(act_ref[0, r0:r0 + nrows, :], wd[0:MOE_TN, :],
                                          preferred_element_type=F32)
                for q in range(1, MOE_J1):
                    y = y + jnp.dot(act_ref[q, r0:r0 + nrows, :], wd[q * MOE_TN:(q + 1) * MOE_TN, :],
                                    preferred_element_type=F32)
                for cc in range(per):
                    stage_ref[pl.ds(r0 * ROW_TILE + jd * per + cc, nrows, stride=ROW_TILE), :] = (
                        y[:, cc * LANES:(cc + 1) * LANES])

            @pl.when(full)
            def _():
                for r0 in range(0, MOE_TS, MOE_CHUNK):
                    down_rows(r0, MOE_CHUNK)

            for k in range(MOE_NSB - 1):
                @pl.when(jnp.logical_and(jnp.logical_not(full), k < nsub))
                def _():
                    down_rows(k * MOE_SUB, MOE_SUB)

    @pl.when(j == MOE_J1 + MOE_J2 - 1)
    def _():
        for k in range(MOE_NSB):
            @pl.when(k < nsub)
            def _():
                out_copy(k).start()
        for k in range(MOE_NSB):
            @pl.when(k < nsub)
            def _():
                out_copy(k).wait()


def _experts(sup_e, sup_row, sup_n, xs_rows, w_up, b_up, w_down, b_down):
    def up_j(s, j, e, r, n):
        return jnp.where(n[s] > 0, jnp.minimum(j, MOE_J1 - 1), MOE_J1 - 1)

    def dn_j(s, j, e, r, n):
        return jnp.where(n[s] > 0, jnp.maximum(j - MOE_J1, 0), MOE_J2 - 1)

    b_up3 = b_up.reshape(N_EXPERTS, 1, 2 * D_FF)
    b_down3 = b_down.reshape(N_EXPERTS, 1, D_MODEL)
    return pl.pallas_call(
        _expert_body,
        grid_spec=pltpu.PrefetchScalarGridSpec(
            num_scalar_prefetch=3,
            grid=(MOE_SMAX, MOE_J1 + MOE_J2),
            in_specs=[
                pl.BlockSpec(memory_space=pl.ANY),
                pl.BlockSpec((None, D_MODEL, MOE_TN), lambda s, j, e, r, n: (e[s], 0, up_j(s, j, e, r, n))),
                pl.BlockSpec((None, D_MODEL, MOE_TN),
                             lambda s, j, e, r, n: (e[s], 0, MOE_J1 + up_j(s, j, e, r, n))),
                pl.BlockSpec((None, 1, MOE_TN), lambda s, j, e, r, n: (e[s], 0, up_j(s, j, e, r, n))),
                pl.BlockSpec((None, 1, MOE_TN), lambda s, j, e, r, n: (e[s], 0, MOE_J1 + up_j(s, j, e, r, n))),
                pl.BlockSpec((None, D_FF, MOE_TN), lambda s, j, e, r, n: (e[s], 0, dn_j(s, j, e, r, n))),
                pl.BlockSpec((None, 1, MOE_TN), lambda s, j, e, r, n: (e[s], 0, dn_j(s, j, e, r, n))),
            ],
            out_specs=pl.BlockSpec(memory_space=pl.ANY),
            scratch_shapes=[pltpu.VMEM((MOE_TS * ROW_TILE, LANES), F32),
                            pltpu.VMEM((MOE_TS, D_MODEL), BF16),
                            pltpu.VMEM((MOE_J1, MOE_TS, MOE_TN), BF16),
                            pltpu.SemaphoreType.DMA(()),
                            pltpu.SemaphoreType.DMA(())]),
        out_shape=jax.ShapeDtypeStruct((MOE_ROWS * ROW_TILE, LANES), F32),
        input_output_aliases={3: 0},
        compiler_params=_cparams(2, 58),
        name="moe_experts",
    )(sup_e, sup_row, sup_n, xs_rows, w_up, w_up, b_up3, b_up3, w_down, b_down3)


COMB_TM = 512
COMB_TN = 512
COMB_J = D_MODEL // COMB_TN


def _combine_body(dest_ref, yb_hbm, h_ref, gt_ref, gp_ref, wg_ref, p_ref, wu_ref, gf_ref, o_ref,
                  gath_ref, n3_ref, sem):
    tm = COMB_TM
    i = pl.program_id(0)
    j = pl.program_id(1)

    @pl.when(j == 0)
    def _():
        base = i * tm

        def issue(r, carry):
            for k in range(TOP_K):
                _row_copy(yb_hbm, dest_ref[(base + r) * TOP_K + k], gath_ref, k * tm + r, sem).start()
            return carry

        lax.fori_loop(0, tm, issue, 0)

        pltpu.make_async_copy(yb_hbm.at[pl.ds(0, TOP_K * tm * ROW_TILE)], gath_ref, sem).wait()

        gt = gt_ref[...]
        ssq = jnp.zeros((tm, 1), F32)
        for cidx in range(ROW_TILE):
            cols = slice(cidx * LANES, (cidx + 1) * LANES)
            acc = h_ref[:, cols]
            for k in range(TOP_K):
                acc = acc + gt[:, k:k + 1] * gath_ref[pl.ds(k * tm * ROW_TILE + cidx, tm, stride=ROW_TILE), :]
            o_ref[:, cols] = acc
            ssq = ssq + jnp.sum(acc * acc, axis=-1, keepdims=True)
        scale = lax.rsqrt(ssq / D_MODEL + EPS)
        for q in range(COMB_J):
            cols = slice(q * COMB_TN, (q + 1) * COMB_TN)
            n3_ref[:, cols] = (o_ref[:, cols] * scale * gp_ref[:, cols]).astype(BF16)

    for q in range(COMB_J):
        @pl.when(j == q)
        def _():
            cols = slice(q * COMB_TN, (q + 1) * COMB_TN)
            gate = _sigmoid(jnp.dot(n3_ref[...], wg_ref[...].astype(BF16), preferred_element_type=F32))
            up = jnp.dot(p_ref[...].astype(BF16), wu_ref[...].astype(BF16), preferred_element_type=F32)
            o_ref[:, cols] = o_ref[:, cols] + gate * up

    @pl.when(j == COMB_J - 1)
    def _():
        ssq = jnp.zeros((tm, 1), F32)
        for q in range(COMB_J):
            v = o_ref[:, q * COMB_TN:(q + 1) * COMB_TN]
            ssq = ssq + jnp.sum(v * v, axis=-1, keepdims=True)
        scale = lax.rsqrt(ssq / D_MODEL + EPS)
        for q in range(COMB_J):
            cols = slice(q * COMB_TN, (q + 1) * COMB_TN)
            o_ref[:, cols] = o_ref[:, cols] * scale * gf_ref[:, cols]


def _combine(dest_flat, yb_rows, h, gates, g_ple, w_ple_gate, p_all, w_ple_up, g_final):
    tm = COMB_TM
    return pl.pallas_call(
        _combine_body,
        grid_spec=pltpu.PrefetchScalarGridSpec(
            num_scalar_prefetch=1,
            grid=(N_TOK // tm, COMB_J),
            in_specs=[
                pl.BlockSpec(memory_space=pl.ANY),
                pl.BlockSpec((tm, D_MODEL), lambda i, j, d: (i, 0)),
                pl.BlockSpec((tm, LANES), lambda i, j, d: (i, 0)),
                pl.BlockSpec((1, D_MODEL), lambda i, j, d: (0, 0)),
                pl.BlockSpec((D_MODEL, COMB_TN), lambda i, j, d: (0, j)),
                pl.BlockSpec((tm, D_PLE), lambda i, j, d: (i, 0)),
                pl.BlockSpec((D_PLE, COMB_TN), lambda i, j, d: (0, j)),
                pl.BlockSpec((1, D_MODEL), lambda i, j, d: (0, 0)),
            ],
            out_specs=pl.BlockSpec((tm, D_MODEL), lambda i, j, d: (i, 0)),
            scratch_shapes=[pltpu.VMEM((TOP_K * tm * ROW_TILE, LANES), F32),
                            pltpu.VMEM((tm, D_MODEL), BF16),
                            pltpu.SemaphoreType.DMA(())]),
        out_shape=jax.ShapeDtypeStruct((N_TOK, D_MODEL), F32),
        compiler_params=_cparams(2, 56),
        name="combine_ple_norm",
    )(dest_flat, yb_rows, h, gates, g_ple, w_ple_gate, p_all, w_ple_up, g_final)


def _routing_tables(counts, eidx, rank):
    nsub_e = (counts + MOE_SUB - 1) // MOE_SUB
    prow_e = (jnp.cumsum(nsub_e) - nsub_e) * MOE_SUB
    e_t = eidx.T
    first_row = jnp.zeros_like(e_t)
    for e in range(N_EXPERTS):
        first_row = jnp.where(e_t == e, prow_e[e], first_row)
    dest = (first_row + rank.T).T.reshape(-1).astype(I32)
    csub = jnp.cumsum(nsub_e)
    blk = jnp.arange(MOE_ROWS // MOE_SUB, dtype=I32)
    is_last = jnp.any((blk[:, None] == csub[None, :] - 1) & (nsub_e[None, :] > 0), axis=1)
    zmask = (is_last | (blk >= csub[-1])).astype(I32)
    nsup_e = (nsub_e + MOE_NSB - 1) // MOE_NSB
    csup = jnp.cumsum(nsup_e)
    total = csup[-1]
    s = jnp.arange(MOE_SMAX, dtype=I32)
    e_raw = jnp.minimum(jnp.searchsorted(csup, s, side="right").astype(I32), N_EXPERTS - 1)
    valid = s < total
    e_last = e_raw[jnp.maximum(total - 1, 0)]
    e_s = jnp.where(valid, e_raw, e_last)
    local = s - (csup[e_s] - nsup_e[e_s])
    row_s = jnp.where(valid, prow_e[e_s] + local * MOE_TS, 0)
    n_s = jnp.where(valid, jnp.clip(nsub_e[e_s] - local * MOE_NSB, 0, MOE_NSB), 0)
    return dest, zmask, e_s.astype(I32), row_s.astype(I32), n_s.astype(I32)


def _pad_prev(state, width):
    b, _, c = state.shape
    return jnp.pad(state, ((0, 0), (SUBLANES - (width - 1), 0), (0, 0))).reshape(b * SUBLANES, c)


def kernel(x_prompt, x_sample, p_prompt, p_sample, state_conv_a, state_conv_ssm, state_ssm, g_mix, w_in, conv_a_w, ssm_conv_w, ssm_conv_b, dt_bias, a_log, d_skip, ssm_norm_g, w_branch_a, w_branch_b, w_out, g_ffn, w_router, b_router, w_up, b_up, w_down, b_down, g_ple, w_ple_gate, w_ple_up, g_final):
    xp = x_prompt.reshape(NP_TOK, D_MODEL)
    xs = x_sample.reshape(NS_TOK, D_MODEL)
    p_all = jnp.concatenate([p_prompt[0].reshape(NP_TOK, D_PLE), p_sample[0].reshape(NS_TOK, D_PLE)], axis=0)
    g_mix2 = g_mix[0].reshape(1, D_MODEL)

    w_in_t = jnp.swapaxes(w_in, 1, 2).reshape(-1, D_MODEL)
    tn = 512
    proj = _rms_matmul(xp, xs, g_mix2, w_in_t,
                       lambda j: j * tn + (j // (N_MAIN // tn)) * (COL_GATE - N_MAIN),
                       N_PROJ, 1024, tn, "in_proj")
    dt_raw = _rms_matmul(xp, xs, g_mix2, w_in_t, lambda j: COL_DT + j * SSM_HEADS, SSM_HEADS, 1024, SSM_HEADS,
                         "in_proj_dt")

    ya_p, ua_tail = _conv_a_prompt(proj, conv_a_w[0])
    ya_s, ua_s = _conv_a_sample(proj, conv_a_w[0], _pad_prev(state_conv_a[0], CONV_A_W))
    conv_a_prompt = ua_tail[:, SUBLANES - (CONV_A_W - 1):, :]
    conv_a_sample = ua_s.reshape(DEC_BATCH, DEC_SEQ, D_CONV)[:, DEC_SEQ - (CONV_A_W - 1):, :]

    conv_b2 = ssm_conv_b[0].reshape(1, D_XBC)
    xbc_p = _conv_m_prompt(proj, ssm_conv_w[0], conv_b2)
    xbc_s = _conv_m_sample(proj, ssm_conv_w[0], conv_b2, _pad_prev(state_conv_ssm[0], SSM_CONV_W))
    proj3 = proj.reshape(N_TOK // DEC_SEQ, DEC_SEQ, N_PROJ)
    t0 = DEC_SEQ - (SSM_CONV_W - 1)
    conv_m_prompt = proj3[SEQ // DEC_SEQ - 1:NP_TOK // DEC_SEQ:SEQ // DEC_SEQ, t0:, COL_XBC:COL_XBC + D_XBC]
    conv_m_sample = proj3[NP_TOK // DEC_SEQ:, t0:, COL_XBC:COL_XBC + D_XBC]

    ssd_w = (dt_bias[0].reshape(1, SSM_HEADS), dt_bias[0].reshape(SSM_HEADS, 1),
             a_log[0].reshape(1, SSM_HEADS), a_log[0].reshape(SSM_HEADS, 1),
             jnp.repeat(d_skip[0], SSM_HEAD_DIM).reshape(1, D_INNER), ssm_norm_g[0].reshape(1, D_INNER))
    ym_p, ssm_p = _ssd_call(xbc_p, proj, dt_raw, *ssd_w)
    ym_s, ssm_s = _ssd_step_call(xbc_s, proj, dt_raw, *ssd_w, state_ssm)
    ssm_shape = (SSM_HEADS, SSM_HEAD_DIM, SSM_STATE)

    ya = jnp.concatenate([ya_p, ya_s], axis=0)
    ym = jnp.concatenate([ym_p, ym_s.astype(BF16)], axis=0)
    merged = _merge(ya, ym, w_branch_a[0], w_branch_b[0], proj)
    h = _resid_matmul(merged, w_out[0], xp, xs)

    wr_pad = jnp.pad(w_router[0], ((0, 0), (0, LANES - N_EXPERTS)))
    br_pad = jnp.pad(b_router[0].reshape(1, N_EXPERTS), ((0, 0), (0, LANES - N_EXPERTS)), constant_values=-1e30)
    g_ffn2 = g_ffn[0].reshape(1, D_MODEL)
    eidx, gate_w, rank, cnt = _router(h, g_ffn2, wr_pad, br_pad)
    counts = cnt[0, :N_EXPERTS].astype(I32)
    dest, zmask, sup_e, sup_row, sup_n = _routing_tables(counts, eidx[:, :TOP_K], rank[:, :TOP_K])
    xs_rows = _dispatch(dest, zmask, h, g_ffn2)
    yb_rows = _experts(sup_e, sup_row, sup_n, xs_rows, w_up[0], b_up[0], w_down[0], b_down[0])

    y_all = _combine(dest, yb_rows, h, gate_w, g_ple[0].reshape(1, D_MODEL), w_ple_gate[0], p_all,
                     w_ple_up[0], g_final.reshape(1, D_MODEL))
    y_prompt = y_all[:NP_TOK].reshape(BATCH, SEQ, D_MODEL)
    y_sample = y_all[NP_TOK:].reshape(DEC_BATCH, DEC_SEQ, D_MODEL)
    return (y_prompt, y_sample,
            conv_a_prompt[None], conv_m_prompt[None], ssm_p.reshape((1, BATCH) + ssm_shape),
            conv_a_sample[None], conv_m_sample[None], ssm_s)
```

```python
import functools

import jax
import jax.numpy as jnp
from jax import lax
from jax.experimental import pallas as pl
from jax.experimental.pallas import tpu as pltpu

F32 = jnp.float32
BF16 = jnp.bfloat16
I32 = jnp.int32
HI = lax.Precision.HIGHEST

D_MODEL = 2048
BATCH = 4
SEQ = 2048
DEC_BATCH = 128
DEC_SEQ = 8
D_PLE = 256
D_CONV = D_MODEL
CONV_A_W = 3
D_INNER = 2 * D_MODEL
SSM_HEAD_DIM = 64
SSM_HEADS = D_INNER // SSM_HEAD_DIM
SSM_GROUPS = 8
SSM_STATE = 128
SSM_CONV_W = 4
SSM_CHUNK = 128
D_XBC = D_INNER + 2 * SSM_GROUPS * SSM_STATE
N_EXPERTS = 32
TOP_K = 4
D_FF = D_MODEL
SWIGLU_LIMIT = 7.0
SWIGLU_ALPHA = 1.702
EPS = 1e-6

NP_TOK = BATCH * SEQ
NS_TOK = DEC_BATCH * DEC_SEQ
N_TOK = NP_TOK + NS_TOK
N_ASSIGN = N_TOK * TOP_K

LANES = 128
SUBLANES = 8
ROW_TILE = D_MODEL // LANES

COL_AX = 0
COL_AB = D_CONV
COL_AC = 2 * D_CONV
COL_Z = 3 * D_CONV
COL_XBC = COL_Z + D_INNER
COL_DT = COL_XBC + D_XBC
COL_GATE = COL_DT + SSM_HEADS
N_MAIN = COL_DT
N_PROJ = N_MAIN + 2 * D_MODEL

MOE_SUB = 256
MOE_TS = 1280
MOE_NSB = MOE_TS // MOE_SUB
MOE_CHUNK = MOE_TS // 2
MOE_ROWS = N_ASSIGN + N_EXPERTS * MOE_SUB
MOE_SMAX = N_ASSIGN // MOE_TS + N_EXPERTS
MOE_TN = 512
MOE_J1 = D_FF // MOE_TN
MOE_J2 = D_MODEL // MOE_TN

MIB = 1024 * 1024


def _cparams(n_grid, vmem_mib):
    return pltpu.CompilerParams(dimension_semantics=("arbitrary",) * n_grid,
                                vmem_limit_bytes=vmem_mib * MIB)


def _sigmoid(x):
    return 1.0 / (1.0 + jnp.exp(-x))


def _softplus(x):
    return jnp.maximum(x, 0.0) + jnp.log1p(jnp.exp(-jnp.abs(x)))


def _group_specs(tm, width, n_grid):
    npt = NP_TOK // tm
    if n_grid == 1:
        return (pl.BlockSpec((tm, width), lambda i: (jnp.minimum(i, npt - 1), 0)),
                pl.BlockSpec((tm, width), lambda i: (jnp.maximum(i - npt, 0), 0)))
    return (pl.BlockSpec((tm, width), lambda i, j, *_: (jnp.minimum(i, npt - 1), 0)),
            pl.BlockSpec((tm, width), lambda i, j, *_: (jnp.maximum(i - npt, 0), 0)))


def _rms_mm_body(npt, xp_ref, xs_ref, g_ref, w_ref, o_ref, n_ref):
    i = pl.program_id(0)
    first = pl.program_id(1) == 0

    def norm(x_ref):
        x = x_ref[...]
        ms = jnp.mean(x * x, axis=-1, keepdims=True)
        n_ref[...] = (x * lax.rsqrt(ms + EPS) * g_ref[...]).astype(BF16)

    @pl.when(jnp.logical_and(first, i < npt))
    def _():
        norm(xp_ref)

    @pl.when(jnp.logical_and(first, i >= npt))
    def _():
        norm(xs_ref)

    o_ref[...] = lax.dot_general(n_ref[...], w_ref[...].astype(BF16), (((1,), (1,)), ((), ())),
                                 preferred_element_type=F32)


def _rms_matmul(xp, xs, g, w_t, feature_offset, ncols, tm, tn, name):
    k = xp.shape[1]
    xp_spec, xs_spec = _group_specs(tm, k, 2)
    return pl.pallas_call(
        functools.partial(_rms_mm_body, NP_TOK // tm),
        grid=(N_TOK // tm, ncols // tn),
        in_specs=[xp_spec, xs_spec,
                  pl.BlockSpec((1, k), lambda i, j: (0, 0)),
                  pl.BlockSpec((pl.Element(tn), pl.Element(k)),
                               lambda i, j: (pl.multiple_of(feature_offset(j), SUBLANES), 0))],
        out_specs=pl.BlockSpec((tm, tn), lambda i, j: (i, j)),
        out_shape=jax.ShapeDtypeStruct((N_TOK, ncols), F32),
        scratch_shapes=[pltpu.VMEM((tm, k), BF16)],
        compiler_params=_cparams(2, 56),
        name=name,
    )(xp, xs, g, w_t)


def _causal_taps(u, w_ref, width, rowmod, prev):
    rows = u.shape[0]
    acc = u * w_ref[width - 1:width, :]
    for s in range(1, width):
        shifted = pltpu.roll(u, s, 0)
        if prev is None:
            fill = jnp.zeros_like(u)
        else:
            fill = pltpu.roll(prev, rows - SUBLANES + s, 0)
        acc = acc + jnp.where(rowmod >= s, shifted, fill) * w_ref[width - 1 - s:width - s, :]
    return acc


def _row_in_seq(shape, period):
    r = lax.broadcasted_iota(I32, shape, 0)
    return r if period >= shape[0] else r & (period - 1)


def _conv_a_body(period, has_prev, *refs):
    if has_prev:
        ax_ref, ab_ref, ac_ref, w_ref, prev_ref, ya_ref, u_ref = refs
        prev = prev_ref[...]
    else:
        ax_ref, ab_ref, ac_ref, w_ref, ya_ref, u_ref = refs
        prev = None
    u = ac_ref[...] * ax_ref[...]
    v = _causal_taps(u, w_ref, CONV_A_W, _row_in_seq(u.shape, period), prev)
    ya_ref[...] = (ab_ref[...] * v).astype(BF16)
    if has_prev:
        u_ref[...] = u
    else:
        u_ref[...] = u[u.shape[0] - SUBLANES:, :]


def _conv_m_body(period, has_prev, *refs):
    if has_prev:
        x_ref, w_ref, b_ref, prev_ref, o_ref = refs
        prev = prev_ref[...]
    else:
        x_ref, w_ref, b_ref, o_ref = refs
        prev = None
    u = x_ref[...]
    v = _causal_taps(u, w_ref, SSM_CONV_W, _row_in_seq(u.shape, period), prev) + b_ref[...]
    o_ref[...] = v * _sigmoid(v)


CONV_TC = 256


def _conv_a_prompt(proj, conv_w):
    nb = D_CONV // CONV_TC
    blk = lambda off: pl.BlockSpec((SEQ, CONV_TC), lambda b, c, off=off: (b, off // CONV_TC + c))
    return pl.pallas_call(
        functools.partial(_conv_a_body, SEQ, False),
        grid=(BATCH, nb),
        in_specs=[blk(COL_AX), blk(COL_AB), blk(COL_AC),
                  pl.BlockSpec((CONV_A_W, CONV_TC), lambda b, c: (0, c))],
        out_specs=[pl.BlockSpec((SEQ, CONV_TC), lambda b, c: (b, c)),
                   pl.BlockSpec((None, SUBLANES, CONV_TC), lambda b, c: (b, 0, c))],
        out_shape=[jax.ShapeDtypeStruct((NP_TOK, D_CONV), BF16),
                   jax.ShapeDtypeStruct((BATCH, SUBLANES, D_CONV), F32)],
        compiler_params=_cparams(2, 48),
        name="conv_a_prompt",
    )(proj, proj, proj, conv_w)


def _conv_a_sample(proj, conv_w, prev_pad):
    nb = D_CONV // CONV_TC
    rb = NP_TOK // NS_TOK
    blk = lambda off: pl.BlockSpec((NS_TOK, CONV_TC), lambda c, off=off: (rb, off // CONV_TC + c))
    own = pl.BlockSpec((NS_TOK, CONV_TC), lambda c: (0, c))
    return pl.pallas_call(
        functools.partial(_conv_a_body, DEC_SEQ, True),
        grid=(nb,),
        in_specs=[blk(COL_AX), blk(COL_AB), blk(COL_AC),
                  pl.BlockSpec((CONV_A_W, CONV_TC), lambda c: (0, c)), own],
        out_specs=[own, own],
        out_shape=[jax.ShapeDtypeStruct((NS_TOK, D_CONV), BF16),
                   jax.ShapeDtypeStruct((NS_TOK, D_CONV), F32)],
        compiler_params=_cparams(1, 48),
        name="conv_a_sample",
    )(proj, proj, proj, conv_w, prev_pad)


def _conv_m_prompt(proj, conv_w, conv_b):
    nb = D_XBC // CONV_TC
    return pl.pallas_call(
        functools.partial(_conv_m_body, SEQ, False),
        grid=(BATCH, nb),
        in_specs=[pl.BlockSpec((SEQ, CONV_TC), lambda b, c: (b, COL_XBC // CONV_TC + c)),
                  pl.BlockSpec((SSM_CONV_W, CONV_TC), lambda b, c: (0, c)),
                  pl.BlockSpec((1, CONV_TC), lambda b, c: (0, c))],
        out_specs=pl.BlockSpec((SEQ, CONV_TC), lambda b, c: (b, c)),
        out_shape=jax.ShapeDtypeStruct((NP_TOK, D_XBC), F32),
        compiler_params=_cparams(2, 48),
        name="conv_m_prompt",
    )(proj, conv_w, conv_b)


def _conv_m_sample(proj, conv_w, conv_b, prev_pad):
    nb = D_XBC // CONV_TC
    rb = NP_TOK // NS_TOK
    own = pl.BlockSpec((NS_TOK, CONV_TC), lambda c: (0, c))
    return pl.pallas_call(
        functools.partial(_conv_m_body, DEC_SEQ, True),
        grid=(nb,),
        in_specs=[pl.BlockSpec((NS_TOK, CONV_TC), lambda c: (rb, COL_XBC // CONV_TC + c)),
                  pl.BlockSpec((SSM_CONV_W, CONV_TC), lambda c: (0, c)),
                  pl.BlockSpec((1, CONV_TC), lambda c: (0, c)), own],
        out_specs=own,
        out_shape=jax.ShapeDtypeStruct((NS_TOK, D_XBC), F32),
        compiler_params=_cparams(1, 48),
        name="conv_m_sample",
    )(proj, conv_w, conv_b, prev_pad)


def _ssd_body(xbc_ref, zlo_ref, zhi_ref, dtr_ref, dtb_ref, dtbc_ref, alog_ref, alogc_ref, dsk_ref, ng_ref,
              ym_ref, hl_ref, ht_ref, y_ref):
    L = SSM_CHUNK
    H = SSM_HEADS
    c = pl.program_id(1)

    @pl.when(c == 0)
    def _():
        ht_ref[...] = jnp.zeros_like(ht_ref)

    dtr = dtr_ref[...]
    dt = _softplus(dtr + dtb_ref[...])
    eye = (lax.broadcasted_iota(I32, (H, H), 0) == lax.broadcasted_iota(I32, (H, H), 1)).astype(F32)
    dtr_t = lax.dot_general(eye, dtr, (((1,), (1,)), ((), ())), precision=HI,
                            preferred_element_type=F32)
    dt_t = _softplus(dtr_t + dtbc_ref[...])
    adt = dt * (-jnp.exp(alog_ref[...]))
    adt_t = dt_t * (-jnp.exp(alogc_ref[...]))
    li = lax.broadcasted_iota(I32, (L, L), 0)
    si = lax.broadcasted_iota(I32, (L, L), 1)
    causal = li >= si
    a_cum = jnp.dot(causal.astype(F32), adt, precision=HI, preferred_element_type=F32)
    a_cum_t = jnp.dot(adt_t, (li <= si).astype(F32), precision=HI, preferred_element_type=F32)
    ea = jnp.exp(a_cum)
    a_tot_t = a_cum_t[:, L - 1:L]
    dte_t = jnp.exp(a_tot_t - a_cum_t)
    ea_tot_t = jnp.exp(a_tot_t)

    lane = lax.broadcasted_iota(I32, (L, LANES), 1)
    lo = lane < SSM_HEAD_DIM
    lane1 = lax.broadcasted_iota(I32, (1, LANES), 1) < SSM_HEAD_DIM
    gn = SSM_GROUPS * SSM_STATE
    for g in range(SSM_GROUPS):
        bg = xbc_ref[:, D_INNER + g * SSM_STATE:D_INNER + (g + 1) * SSM_STATE]
        cg = xbc_ref[:, D_INNER + gn + g * SSM_STATE:D_INNER + gn + (g + 1) * SSM_STATE]
        bg_t = bg.T
        cb = jnp.dot(cg.astype(BF16), bg_t.astype(BF16), preferred_element_type=F32)
        for jj in range(SSM_HEADS // SSM_GROUPS // 2):
            j = g * (SSM_HEADS // SSM_GROUPS // 2) + jj
            cols = slice(j * LANES, (j + 1) * LANES)
            lhs1, lhs2 = [], []
            for h in (2 * j, 2 * j + 1):
                diff = a_cum[:, h:h + 1] - a_cum_t[h:h + 1, :]
                decay = jnp.exp(jnp.where(causal, diff, -1e30))
                lhs1.append((cb * decay).astype(BF16))
            for h in (2 * j, 2 * j + 1):
                lhs1.append((cg * ea[:, h:h + 1]).astype(BF16))
                lhs2.append((bg_t * dte_t[h:h + 1, :]).astype(BF16))
            dt_pair = jnp.where(lo, dt[:, 2 * j:2 * j + 1], dt[:, 2 * j + 1:2 * j + 2])
            xdt = xbc_ref[:, cols] * dt_pair
            zero = jnp.zeros_like(xdt)
            x_top = jnp.where(lo, xdt, zero).astype(BF16)
            x_bot = jnp.where(lo, zero, xdt).astype(BF16)
            ht = ht_ref[:, cols]
            h_top = jnp.where(lo, ht, zero).astype(BF16)
            h_bot = jnp.where(lo, zero, ht).astype(BF16)
            y_ref[:, cols] = jnp.dot(jnp.concatenate(lhs1, axis=1),
                                     jnp.concatenate([x_top, x_bot, h_top, h_bot], axis=0),
                                     preferred_element_type=F32)
            upd = jnp.dot(jnp.concatenate(lhs2, axis=1), jnp.concatenate([x_top, x_bot], axis=0),
                          preferred_element_type=F32)
            keep = jnp.where(lane1, ea_tot_t[2 * j:2 * j + 1, :], ea_tot_t[2 * j + 1:2 * j + 2, :])
            ht_ref[:, cols] = ht * keep + upd

    gw = D_INNER // SSM_GROUPS
    for g in range(SSM_GROUPS):
        cols = slice(g * gw, (g + 1) * gw)
        zr = zlo_ref if g < SSM_GROUPS // 2 else zhi_ref
        zc = slice((g % (SSM_GROUPS // 2)) * gw, (g % (SSM_GROUPS // 2) + 1) * gw)
        z = zr[:, zc]
        y = y_ref[:, cols] + dsk_ref[:, cols] * xbc_ref[:, cols]
        y = y * (z * _sigmoid(z))
        y = y * lax.rsqrt(jnp.mean(y * y, axis=-1, keepdims=True) + EPS)
        ym_ref[:, cols] = (y * ng_ref[:, cols]).astype(ym_ref.dtype)

    @pl.when(c == pl.num_programs(1) - 1)
    def _():
        hl_ref[...] = ht_ref[...].T


def _ssd_call(xbc, proj, dt_raw, dtb, dtbc, alog, alogc, dsk, ng):
    L = SSM_CHUNK
    n_chunk = SEQ // L
    hp = SSM_HEADS * SSM_HEAD_DIM
    zb = D_INNER // 2
    vec = lambda shape: pl.BlockSpec(shape, lambda b, c: (0, 0))
    return pl.pallas_call(
        _ssd_body,
        grid=(BATCH, n_chunk),
        in_specs=[pl.BlockSpec((L, D_XBC), lambda b, c: (b * n_chunk + c, 0)),
                  pl.BlockSpec((L, zb), lambda b, c: (b * n_chunk + c, COL_Z // zb)),
                  pl.BlockSpec((L, zb), lambda b, c: (b * n_chunk + c, COL_Z // zb + 1)),
                  pl.BlockSpec((L, SSM_HEADS), lambda b, c: (b * n_chunk + c, 0)),
                  vec((1, SSM_HEADS)), vec((SSM_HEADS, 1)), vec((1, SSM_HEADS)), vec((SSM_HEADS, 1)),
                  vec((1, D_INNER)), vec((1, D_INNER))],
        out_specs=[pl.BlockSpec((L, D_INNER), lambda b, c: (b * n_chunk + c, 0)),
                   pl.BlockSpec((None, hp, SSM_STATE), lambda b, c: (b, 0, 0))],
        out_shape=[jax.ShapeDtypeStruct((NP_TOK, D_INNER), BF16),
                   jax.ShapeDtypeStruct((BATCH, hp, SSM_STATE), F32)],
        scratch_shapes=[pltpu.VMEM((SSM_STATE, hp), F32), pltpu.VMEM((L, hp), F32)],
        compiler_params=_cparams(2, 48),
        name="ssd_prompt",
    )(xbc, proj, proj, dt_raw, dtb, dtbc, alog, alogc, dsk, ng)


SSD_STEP_SEQS = 2


def _ssd_step_body(xbc_in, zlo_ref, zhi_ref, dtr_in, dtb_ref, dtbc_ref, alog_ref, alogc_ref, dsk_ref, ng_ref,
                   h0_ref, ym_ref, hl_ref, xp_ref, dp_ref):
    T = DEC_SEQ

    @pl.when(pl.program_id(0) == 0)
    def _():
        xp_ref[...] = jnp.zeros_like(xp_ref)
        dp_ref[...] = jnp.zeros_like(dp_ref)

    for s in range(SSD_STEP_SEQS):
        rows = pl.ds(s * T, T)
        _ssd_step_one(xbc_in.at[rows], zlo_ref.at[rows], zhi_ref.at[rows], dtr_in.at[rows], dtb_ref, dtbc_ref,
                      alog_ref, alogc_ref, dsk_ref, ng_ref, h0_ref.at[s], ym_ref.at[rows], hl_ref.at[s],
                      xp_ref.at[s], dp_ref.at[s])


def _ssd_step_one(xbc_in, zlo_ref, zhi_ref, dtr_in, dtb_ref, dtbc_ref, alog_ref, alogc_ref, dsk_ref, ng_ref,
                  h0_ref, ym_ref, hl_ref, xp_ref, dp_ref):
    T = DEC_SEQ
    L = SSM_CHUNK
    H = SSM_HEADS
    P = SSM_HEAD_DIM

    xp_ref[0:T, :] = xbc_in[...]
    dp_ref[0:T, :] = dtr_in[...]

    dt8 = _softplus(dtr_in[...] + dtb_ref[...])
    a_cum8 = dt8 * (-jnp.exp(alog_ref[...]))
    row8 = lax.broadcasted_iota(I32, (T, H), 0)
    k = 1
    while k < T:
        a_cum8 = a_cum8 + jnp.where(row8 >= k, pltpu.roll(a_cum8, k, 0), 0.0)
        k *= 2
    ea8 = jnp.exp(a_cum8)
    dte8 = jnp.exp(a_cum8[T - 1:T, :] - a_cum8)
    eye = (lax.broadcasted_iota(I32, (H, H), 0) == lax.broadcasted_iota(I32, (H, H), 1)).astype(F32)
    dtr_t = lax.dot_general(eye, dp_ref[...], (((1,), (1,)), ((), ())), precision=HI,
                            preferred_element_type=F32)
    dt_t = jnp.where(lax.broadcasted_iota(I32, (H, L), 1) < T, _softplus(dtr_t + dtbc_ref[...]), 0.0)
    adt_t = dt_t * (-jnp.exp(alogc_ref[...]))
    upper = (lax.broadcasted_iota(I32, (L, L), 0) <= lax.broadcasted_iota(I32, (L, L), 1)).astype(F32)
    a_cum_t = jnp.dot(adt_t, upper, precision=HI, preferred_element_type=F32)
    ea_tot_t = jnp.exp(a_cum_t[:, L - 1:L])

    causal8 = lax.broadcasted_iota(I32, (T, L), 0) >= lax.broadcasted_iota(I32, (T, L), 1)
    zeros_t = jnp.zeros((T, LANES), F32)
    gn = SSM_GROUPS * SSM_STATE
    gw = D_INNER // SSM_GROUPS
    hg = SSM_HEADS // SSM_GROUPS
    zeros_pad = jnp.zeros((L - T, gw), F32)
    lane_head = lax.broadcasted_iota(I32, (T, gw), 1) // P

    def per_head(table, g):
        out = jnp.zeros((T, gw), F32)
        for hh in range(hg):
            h = g * hg + hh
            out = jnp.where(lane_head == hh, table[:, h:h + 1], out)
        return out

    for g in range(SSM_GROUPS):
        heads = range(g * hg, (g + 1) * hg)
        cols = slice(g * gw, (g + 1) * gw)
        bgp = xp_ref[:, D_INNER + g * SSM_STATE:D_INNER + (g + 1) * SSM_STATE].astype(BF16)
        cg8 = xbc_in[:, D_INNER + gn + g * SSM_STATE:D_INNER + gn + (g + 1) * SSM_STATE]
        cg16 = jnp.concatenate([cg8, zeros_t], axis=0).astype(BF16)
        cb = lax.dot_general(cg16, bgp, (((1,), (1,)), ((), ())), preferred_element_type=F32)[0:T]

        def decay(h):
            diff = a_cum8[:, h:h + 1] - a_cum_t[h:h + 1, :]
            return jnp.exp(jnp.where(causal8, diff, -1e30))

        m = jnp.concatenate([cb * decay(h) for h in heads], axis=0).astype(BF16)
        cd = jnp.concatenate([cg8 * ea8[:, h:h + 1] for h in heads], axis=0).astype(BF16)
        xdt8 = xbc_in[:, cols] * per_head(dt8, g)
        xd8 = xdt8 * per_head(dte8, g)
        xdt_pad = jnp.concatenate([xdt8, zeros_pad], axis=0).astype(BF16)
        xd_pad = jnp.concatenate([xd8, zeros_pad], axis=0).astype(BF16)
        hp = h0_ref[g * hg:(g + 1) * hg].reshape(hg * P, SSM_STATE)
        out = (jnp.dot(m, xdt_pad, preferred_element_type=F32)
               + lax.dot_general(cd, hp.astype(BF16), (((1,), (1,)), ((), ())),
                                 preferred_element_type=F32))
        y_ssd = jnp.zeros((T, gw), F32)
        for hh in range(hg):
            y_ssd = jnp.where(lane_head == hh, out[hh * T:(hh + 1) * T], y_ssd)
        upd = lax.dot_general(xd_pad, bgp, (((0,), (0,)), ((), ())), preferred_element_type=F32)
        keep = jnp.concatenate([jnp.broadcast_to(ea_tot_t[h:h + 1, :], (P, SSM_STATE)) for h in heads], axis=0)
        hl_ref[g * hg:(g + 1) * hg] = (hp * keep + upd).reshape(hg, P, SSM_STATE)

        zr = zlo_ref if g < SSM_GROUPS // 2 else zhi_ref
        zc = slice((g % (SSM_GROUPS // 2)) * gw, (g % (SSM_GROUPS // 2) + 1) * gw)
        z = zr[:, zc]
        y = y_ssd + dsk_ref[:, cols] * xbc_in[:, cols]
        y = y * (z * _sigmoid(z))
        y = y * lax.rsqrt(jnp.mean(y * y, axis=-1, keepdims=True) + EPS)
        ym_ref[:, cols] = y * ng_ref[:, cols]


def _ssd_step_call(xbc, proj, dt_raw, dtb, dtbc, alog, alogc, dsk, ng, state):
    ns = SSD_STEP_SEQS
    T = ns * DEC_SEQ
    rb0 = NP_TOK // T
    zb = D_INNER // 2
    st = pl.BlockSpec((None, ns, SSM_HEADS, SSM_HEAD_DIM, SSM_STATE), lambda b: (0, b, 0, 0, 0))
    vec = lambda shape: pl.BlockSpec(shape, lambda b: (0, 0))
    return pl.pallas_call(
        _ssd_step_body,
        grid=(DEC_BATCH // ns,),
        in_specs=[pl.BlockSpec((T, D_XBC), lambda b: (b, 0)),
                  pl.BlockSpec((T, zb), lambda b: (rb0 + b, COL_Z // zb)),
                  pl.BlockSpec((T, zb), lambda b: (rb0 + b, COL_Z // zb + 1)),
                  pl.BlockSpec((T, SSM_HEADS), lambda b: (rb0 + b, 0)),
                  vec((1, SSM_HEADS)), vec((SSM_HEADS, 1)), vec((1, SSM_HEADS)), vec((SSM_HEADS, 1)),
                  vec((1, D_INNER)), vec((1, D_INNER)), st],
        out_specs=[pl.BlockSpec((T, D_INNER), lambda b: (b, 0)), st],
        out_shape=[jax.ShapeDtypeStruct((NS_TOK, D_INNER), F32),
                   jax.ShapeDtypeStruct(state.shape, F32)],
        scratch_shapes=[pltpu.VMEM((ns, SSM_CHUNK, D_XBC), F32), pltpu.VMEM((ns, SSM_CHUNK, SSM_HEADS), F32)],
        compiler_params=_cparams(1, 40),
        name="ssd_sample",
    )(xbc, proj, proj, dt_raw, dtb, dtbc, alog, alogc, dsk, ng, state)


def _merge_body(ya_ref, ym_ref, wa_ref, wb_ref, ga_ref, gb_ref, o_ref):
    a = jnp.dot(ya_ref[...], wa_ref[...].astype(BF16), preferred_element_type=F32)
    b = jnp.dot(ym_ref[...], wb_ref[...].astype(BF16), preferred_element_type=F32)
    o_ref[...] = (_sigmoid(ga_ref[...]) * a + _sigmoid(gb_ref[...]) * b).astype(BF16)


def _merge(ya, ym, wa, wb, proj):
    tm, tn = 1024, 256
    g0 = N_MAIN // tn
    return pl.pallas_call(
        _merge_body,
        grid=(N_TOK // tm, D_MODEL // tn),
        in_specs=[pl.BlockSpec((tm, D_CONV), lambda i, j: (i, 0)),
                  pl.BlockSpec((tm, D_INNER), lambda i, j: (i, 0)),
                  pl.BlockSpec((D_CONV, tn), lambda i, j: (0, j)),
                  pl.BlockSpec((D_INNER, tn), lambda i, j: (0, j)),
                  pl.BlockSpec((tm, tn), lambda i, j: (i, g0 + j)),
                  pl.BlockSpec((tm, tn), lambda i, j: (i, g0 + D_MODEL // tn + j))],
        out_specs=pl.BlockSpec((tm, tn), lambda i, j: (i, j)),
        out_shape=jax.ShapeDtypeStruct((N_TOK, D_MODEL), BF16),
        compiler_params=_cparams(2, 52),
        name="merge",
    )(ya, ym, wa, wb, proj, proj)


def _resid_body(npt, m_ref, w_ref, xp_ref, xs_ref, o_ref):
    x = jnp.where(pl.program_id(0) < npt, xp_ref[...], xs_ref[...])
    o_ref[...] = x + jnp.dot(m_ref[...], w_ref[...].astype(BF16), preferred_element_type=F32)


def _resid_matmul(m, w, xp, xs):
    tm, tn = 1024, 512
    npt = NP_TOK // tm
    return pl.pallas_call(
        functools.partial(_resid_body, npt),
        grid=(N_TOK // tm, D_MODEL // tn),
        in_specs=[pl.BlockSpec((tm, D_MODEL), lambda i, j: (i, 0)),
                  pl.BlockSpec((D_MODEL, tn), lambda i, j: (0, j)),
                  pl.BlockSpec((tm, tn), lambda i, j: (jnp.minimum(i, npt - 1), j)),
                  pl.BlockSpec((tm, tn), lambda i, j: (jnp.maximum(i - npt, 0), j))],
        out_specs=pl.BlockSpec((tm, tn), lambda i, j: (i, j)),
        out_shape=jax.ShapeDtypeStruct((N_TOK, D_MODEL), F32),
        compiler_params=_cparams(2, 48),
        name="out_proj",
    )(m, w, xp, xs)


ROUTER_TM = 512


def _router_body(h_ref, g_ref, wr_ref, br_ref, ei_ref, gt_ref, rk_ref, cnt_ref, carry_ref):
    tm = ROUTER_TM

    @pl.when(pl.program_id(0) == 0)
    def _():
        carry_ref[...] = jnp.zeros_like(carry_ref)

    h = h_ref[...]
    n2 = h * lax.rsqrt(jnp.mean(h * h, axis=-1, keepdims=True) + EPS) * g_ref[...]
    logits = jnp.dot(n2, wr_ref[...], precision=HI, preferred_element_type=F32) + br_ref[...]

    lane = lax.broadcasted_iota(I32, (tm, LANES), 1).astype(F32)
    work = logits
    vals, idxs = [], []
    for _ in range(TOP_K):
        m = jnp.max(work, axis=-1, keepdims=True)
        idx = jnp.min(jnp.where(work == m, lane, float(LANES)), axis=-1, keepdims=True)
        vals.append(m)
        idxs.append(idx)
        work = jnp.where(lane == idx, -jnp.inf, work)
    ex = [jnp.exp(v - vals[0]) for v in vals]
    den = ex[0] + ex[1] + ex[2] + ex[3]
    sel = jnp.zeros((tm, LANES), F32)
    for idx in idxs:
        sel = sel + jnp.where(lane == idx, 1.0, 0.0)
    before = (lax.broadcasted_iota(I32, (tm, tm), 0) > lax.broadcasted_iota(I32, (tm, tm), 1))
    rank_e = jnp.dot(before.astype(BF16), sel.astype(BF16), preferred_element_type=F32) + carry_ref[0:1, :]
    carry_ref[...] = carry_ref[...] + jnp.sum(sel, axis=0, keepdims=True)
    ei = jnp.zeros((tm, LANES), F32)
    gt = jnp.zeros((tm, LANES), F32)
    rk = jnp.zeros((tm, LANES), F32)
    for k in range(TOP_K):
        rank_k = jnp.sum(jnp.where(lane == idxs[k], rank_e, 0.0), axis=-1, keepdims=True)
        here = lane == float(k)
        ei = jnp.where(here, idxs[k], ei)
        gt = jnp.where(here, ex[k] / den, gt)
        rk = jnp.where(here, rank_k, rk)
    ei_ref[...] = ei.astype(I32)
    gt_ref[...] = gt
    rk_ref[...] = rk.astype(I32)
    cnt_ref[...] = carry_ref[...]


def _router(h, g, wr_pad, br_pad):
    tm = ROUTER_TM
    tok = pl.BlockSpec((tm, LANES), lambda i: (i, 0))
    return pl.pallas_call(
        _router_body,
        grid=(N_TOK // tm,),
        in_specs=[pl.BlockSpec((tm, D_MODEL), lambda i: (i, 0)),
                  pl.BlockSpec((1, D_MODEL), lambda i: (0, 0)),
                  pl.BlockSpec((D_MODEL, LANES), lambda i: (0, 0)),
                  pl.BlockSpec((1, LANES), lambda i: (0, 0))],
        out_specs=[tok, tok, tok, pl.BlockSpec((SUBLANES, LANES), lambda i: (0, 0))],
        out_shape=[jax.ShapeDtypeStruct((N_TOK, LANES), I32),
                   jax.ShapeDtypeStruct((N_TOK, LANES), F32),
                   jax.ShapeDtypeStruct((N_TOK, LANES), I32),
                   jax.ShapeDtypeStruct((SUBLANES, LANES), F32)],
        scratch_shapes=[pltpu.VMEM((SUBLANES, LANES), F32)],
        compiler_params=_cparams(1, 48),
        name="router",
    )(h, g, wr_pad, br_pad)


DISPATCH_TOK = 512


def _row_slab(ref, row):
    start = row * ROW_TILE
    if not isinstance(start, int):
        start = pl.multiple_of(start, ROW_TILE)
    return ref.at[pl.ds(start, ROW_TILE)]


def _row_copy(src, src_row, dst, dst_row, sem):
    return pltpu.make_async_copy(_row_slab(src, src_row), _row_slab(dst, dst_row), sem)


def _dispatch_body(dest_ref, zmask_ref, h_ref, g_ref, xs_hbm, rows_ref, zero_ref, sem, zsem):
    tm = DISPATCH_TOK
    base = pl.program_id(0) * tm
    sub_rows = MOE_SUB * ROW_TILE

    @pl.when(pl.program_id(0) == 0)
    def _():
        zero_ref[...] = jnp.zeros_like(zero_ref)

        def zero_copy(b):
            return pltpu.make_async_copy(
                zero_ref, xs_hbm.at[pl.ds(pl.multiple_of(b * sub_rows, sub_rows), sub_rows)], zsem)

        def start(b, carry):
            @pl.when(zmask_ref[b] != 0)
            def _():
                zero_copy(b).start()
            return carry

        def wait(b, carry):
            @pl.when(zmask_ref[b] != 0)
            def _():
                zero_copy(b).wait()
            return carry

        lax.fori_loop(0, MOE_ROWS // MOE_SUB, start, 0)
        lax.fori_loop(0, MOE_ROWS // MOE_SUB, wait, 0)

    h = h_ref[...]
    n2 = h * lax.rsqrt(jnp.mean(h * h, axis=-1, keepdims=True) + EPS) * g_ref[...]
    for cidx in range(ROW_TILE):
        rows_ref[pl.ds(cidx, tm, stride=ROW_TILE), :] = n2[:, cidx * LANES:(cidx + 1) * LANES]

    def issue(i, carry):
        for k in range(TOP_K):
            _row_copy(rows_ref, i, xs_hbm, dest_ref[(base + i) * TOP_K + k], sem).start()
        return carry

    lax.fori_loop(0, tm, issue, 0)

    for k in range(TOP_K):
        pltpu.make_async_copy(rows_ref, xs_hbm.at[pl.ds(0, tm * ROW_TILE)], sem).wait()


def _dispatch(dest_flat, zmask, h, g):
    tm = DISPATCH_TOK
    return pl.pallas_call(
        _dispatch_body,
        grid_spec=pltpu.PrefetchScalarGridSpec(
            num_scalar_prefetch=2,
            grid=(N_TOK // tm,),
            in_specs=[pl.BlockSpec((tm, D_MODEL), lambda i, d, z: (i, 0)),
                      pl.BlockSpec((1, D_MODEL), lambda i, d, z: (0, 0))],
            out_specs=pl.BlockSpec(memory_space=pl.ANY),
            scratch_shapes=[pltpu.VMEM((tm * ROW_TILE, LANES), F32),
                            pltpu.VMEM((MOE_SUB * ROW_TILE, LANES), F32),
                            pltpu.SemaphoreType.DMA(()), pltpu.SemaphoreType.DMA(())]),
        out_shape=jax.ShapeDtypeStruct((MOE_ROWS * ROW_TILE, LANES), F32),
        compiler_params=_cparams(1, 32),
        name="moe_dispatch",
    )(dest_flat, zmask, h, g)


def _expert_body(e_ref, r_ref, n_ref, xs_hbm, wg_ref, wu_ref, bg_ref, bu_ref, wd_ref, bd_ref, yb_hbm,
                 stage_ref, xb_ref, act_ref, sem_in, sem_out):
    del e_ref
    s = pl.program_id(0)
    j = pl.program_id(1)
    nsub = n_ref[s]
    row0 = r_ref[s]
    sub_rows = MOE_SUB * ROW_TILE

    def in_copy(k):
        return pltpu.make_async_copy(
            xs_hbm.at[pl.ds(pl.multiple_of((row0 + k * MOE_SUB) * ROW_TILE, sub_rows), sub_rows)],
            stage_ref.at[pl.ds(k * sub_rows, sub_rows)], sem_in)

    def out_copy(k):
        return pltpu.make_async_copy(
            stage_ref.at[pl.ds(k * sub_rows, sub_rows)],
            yb_hbm.at[pl.ds(pl.multiple_of((row0 + k * MOE_SUB) * ROW_TILE, sub_rows), sub_rows)], sem_out)

    @pl.when(j == 0)
    def _():
        for k in range(MOE_NSB):
            @pl.when(k < nsub)
            def _():
                in_copy(k).start()
        for k in range(MOE_NSB):
            @pl.when(k < nsub)
            def _():
                in_copy(k).wait()
        for k in range(MOE_NSB):
            @pl.when(k < nsub)
            def _():
                for cidx in range(ROW_TILE):
                    xb_ref[k * MOE_SUB:(k + 1) * MOE_SUB, cidx * LANES:(cidx + 1) * LANES] = (
                        stage_ref[pl.ds(k * sub_rows + cidx, MOE_SUB, stride=ROW_TILE), :].astype(BF16))

    full = nsub == MOE_NSB

    @pl.when(j < MOE_J1)
    def _():
        wg = wg_ref[...].astype(BF16)
        wu = wu_ref[...].astype(BF16)

        def up_rows(r0, nrows):
            x = xb_ref[r0:r0 + nrows, :]
            gate = jnp.dot(x, wg, preferred_element_type=F32) + bg_ref[...]
            up = jnp.dot(x, wu, preferred_element_type=F32) + bu_ref[...]
            gate = jnp.minimum(gate, SWIGLU_LIMIT)
            up = jnp.clip(up, -SWIGLU_LIMIT, SWIGLU_LIMIT)
            a = gate * _sigmoid(SWIGLU_ALPHA * gate) * (up + 1.0)
            act_ref[jnp.minimum(j, MOE_J1 - 1), r0:r0 + nrows, :] = a.astype(BF16)

        @pl.when(full)
        def _():
            for r0 in range(0, MOE_TS, MOE_CHUNK):
                up_rows(r0, MOE_CHUNK)

        for k in range(MOE_NSB - 1):
            @pl.when(jnp.logical_and(jnp.logical_not(full), k < nsub))
            def _():
                up_rows(k * MOE_SUB, MOE_SUB)

    per = MOE_TN // LANES
    for jd in range(MOE_J2):
        @pl.when(j == MOE_J1 + jd)
        def _():
            wd = wd_ref[...].astype(BF16)

            def down_rows(r0, nrows):
                y = bd_ref[...] + jnp.dot(act_ref[0, r0:r0 + nrows, :], wd[0:MOE_TN, :],
                                          preferred_element_type=F32)
                for q in range(1, MOE_J1):
                    y = y + jnp.dot(act_ref[q, r0:r0 + nrows, :], wd[q * MOE_TN:(q + 1) * MOE_TN, :],
                                    preferred_element_type=F32)
                for cc in range(per):
                    stage_ref[pl.ds(r0 * ROW_TILE + jd * per + cc, nrows, stride=ROW_TILE), :] = (
                        y[:, cc * LANES:(cc + 1) * LANES])

            @pl.when(full)
            def _():
                for r0 in range(0, MOE_TS, MOE_CHUNK):
                    down_rows(r0, MOE_CHUNK)

            for k in range(MOE_NSB - 1):
                @pl.when(jnp.logical_and(jnp.logical_not(full), k < nsub))
                def _():
                    down_rows(k * MOE_SUB, MOE_SUB)

    @pl.when(j == MOE_J1 + MOE_J2 - 1)
    def _():
        for k in range(MOE_NSB):
            @pl.when(k < nsub)
            def _():
                out_copy(k).start()
        for k in range(MOE_NSB):
            @pl.when(k < nsub)
            def _():
                out_copy(k).wait()


def _experts(sup_e, sup_row, sup_n, xs_rows, w_up, b_up, w_down, b_down):
    def up_j(s, j, e, r, n):
        return jnp.where(n[s] > 0, jnp.minimum(j, MOE_J1 - 1), MOE_J1 - 1)

    def dn_j(s, j, e, r, n):
        return jnp.where(n[s] > 0, jnp.maximum(j - MOE_J1, 0), MOE_J2 - 1)

    b_up3 = b_up.reshape(N_EXPERTS, 1, 2 * D_FF)
    b_down3 = b_down.reshape(N_EXPERTS, 1, D_MODEL)
    return pl.pallas_call(
        _expert_body,
        grid_spec=pltpu.PrefetchScalarGridSpec(
            num_scalar_prefetch=3,
            grid=(MOE_SMAX, MOE_J1 + MOE_J2),
            in_specs=[
                pl.BlockSpec(memory_space=pl.ANY),
                pl.BlockSpec((None, D_MODEL, MOE_TN), lambda s, j, e, r, n: (e[s], 0, up_j(s, j, e, r, n))),
                pl.BlockSpec((None, D_MODEL, MOE_TN),
                             lambda s, j, e, r, n: (e[s], 0, MOE_J1 + up_j(s, j, e, r, n))),
                pl.BlockSpec((None, 1, MOE_TN), lambda s, j, e, r, n: (e[s], 0, up_j(s, j, e, r, n))),
                pl.BlockSpec((None, 1, MOE_TN), lambda s, j, e, r, n: (e[s], 0, MOE_J1 + up_j(s, j, e, r, n))),
                pl.BlockSpec((None, D_FF, MOE_TN), lambda s, j, e, r, n: (e[s], 0, dn_j(s, j, e, r, n))),
                pl.BlockSpec((None, 1, MOE_TN), lambda s, j, e, r, n: (e[s], 0, dn_j(s, j, e, r, n))),
            ],
            out_specs=pl.BlockSpec(memory_space=pl.ANY),
            scratch_shapes=[pltpu.VMEM((MOE_TS * ROW_TILE, LANES), F32),
                            pltpu.VMEM((MOE_TS, D_MODEL), BF16),
                            pltpu.VMEM((MOE_J1, MOE_TS, MOE_TN), BF16),
                            pltpu.SemaphoreType.DMA(()),
                            pltpu.SemaphoreType.DMA(())]),
        out_shape=jax.ShapeDtypeStruct((MOE_ROWS * ROW_TILE, LANES), F32),
        input_output_aliases={3: 0},
        compiler_params=_cparams(2, 58),
        name="moe_experts",
    )(sup_e, sup_row, sup_n, xs_rows, w_up, w_up, b_up3, b_up3, w_down, b_down3)


COMB_TM = 512
COMB_TN = 512
COMB_J = D_MODEL // COMB_TN


def _combine_body(tile0, dest_ref, yb_hbm, h_ref, gt_ref, gp_ref, wg_ref, p_ref, wu_ref, gf_ref, o_ref,
                  gath_ref, n3_ref, sem):
    tm = COMB_TM
    i = pl.program_id(0)
    j = pl.program_id(1)

    @pl.when(j == 0)
    def _():
        base = (tile0 + i) * tm

        def issue(r, carry):
            for k in range(TOP_K):
                _row_copy(yb_hbm, dest_ref[(base + r) * TOP_K + k], gath_ref, k * tm + r, sem).start()
            return carry

        lax.fori_loop(0, tm, issue, 0)

        pltpu.make_async_copy(yb_hbm.at[pl.ds(0, TOP_K * tm * ROW_TILE)], gath_ref, sem).wait()

        gt = gt_ref[...]
        ssq = jnp.zeros((tm, 1), F32)
        for cidx in range(ROW_TILE):
            cols = slice(cidx * LANES, (cidx + 1) * LANES)
            acc = h_ref[:, cols]
            for k in range(TOP_K):
                acc = acc + gt[:, k:k + 1] * gath_ref[pl.ds(k * tm * ROW_TILE + cidx, tm, stride=ROW_TILE), :]
            o_ref[:, cols] = acc
            ssq = ssq + jnp.sum(acc * acc, axis=-1, keepdims=True)
        scale = lax.rsqrt(ssq / D_MODEL + EPS)
        for q in range(COMB_J):
            cols = slice(q * COMB_TN, (q + 1) * COMB_TN)
            n3_ref[:, cols] = (o_ref[:, cols] * scale * gp_ref[:, cols]).astype(BF16)

    for q in range(COMB_J):
        @pl.when(j == q)
        def _():
            cols = slice(q * COMB_TN, (q + 1) * COMB_TN)
            gate = _sigmoid(jnp.dot(n3_ref[...], wg_ref[...].astype(BF16), preferred_element_type=F32))
            up = jnp.dot(p_ref[...].astype(BF16), wu_ref[...].astype(BF16), preferred_element_type=F32)
            o_ref[:, cols] = o_ref[:, cols] + gate * up

    @pl.when(j == COMB_J - 1)
    def _():
        ssq = jnp.zeros((tm, 1), F32)
        for q in range(COMB_J):
            v = o_ref[:, q * COMB_TN:(q + 1) * COMB_TN]
            ssq = ssq + jnp.sum(v * v, axis=-1, keepdims=True)
        scale = lax.rsqrt(ssq / D_MODEL + EPS)
        for q in range(COMB_J):
            cols = slice(q * COMB_TN, (q + 1) * COMB_TN)
            o_ref[:, cols] = o_ref[:, cols] * scale * gf_ref[:, cols]


def _combine(dest_flat, yb_rows, h, gates, g_ple, w_ple_gate, p, w_ple_up, g_final, row0, name):
    tm = COMB_TM
    tile0 = row0 // tm
    n_rows = p.shape[0]
    return pl.pallas_call(
        functools.partial(_combine_body, tile0),
        grid_spec=pltpu.PrefetchScalarGridSpec(
            num_scalar_prefetch=1,
            grid=(n_rows // tm, COMB_J),
            in_specs=[
                pl.BlockSpec(memory_space=pl.ANY),
                pl.BlockSpec((tm, D_MODEL), lambda i, j, d: (tile0 + i, 0)),
                pl.BlockSpec((tm, LANES), lambda i, j, d: (tile0 + i, 0)),
                pl.BlockSpec((1, D_MODEL), lambda i, j, d: (0, 0)),
                pl.BlockSpec((D_MODEL, COMB_TN), lambda i, j, d: (0, j)),
                pl.BlockSpec((tm, D_PLE), lambda i, j, d: (i, 0)),
                pl.BlockSpec((D_PLE, COMB_TN), lambda i, j, d: (0, j)),
                pl.BlockSpec((1, D_MODEL), lambda i, j, d: (0, 0)),
            ],
            out_specs=pl.BlockSpec((tm, D_MODEL), lambda i, j, d: (i, 0)),
            scratch_shapes=[pltpu.VMEM((TOP_K * tm * ROW_TILE, LANES), F32),
                            pltpu.VMEM((tm, D_MODEL), BF16),
                            pltpu.SemaphoreType.DMA(())]),
        out_shape=jax.ShapeDtypeStruct((n_rows, D_MODEL), F32),
        compiler_params=_cparams(2, 56),
        name=name,
    )(dest_flat, yb_rows, h, gates, g_ple, w_ple_gate, p, w_ple_up, g_final)


def _routing_tables(counts, eidx, rank):
    nsub_e = (counts + MOE_SUB - 1) // MOE_SUB
    prow_e = (jnp.cumsum(nsub_e) - nsub_e) * MOE_SUB
    e_t = eidx.T
    first_row = jnp.zeros_like(e_t)
    for e in range(N_EXPERTS):
        first_row = jnp.where(e_t == e, prow_e[e], first_row)
    dest = (first_row + rank.T).T.reshape(-1).astype(I32)
    csub = jnp.cumsum(nsub_e)
    blk = jnp.arange(MOE_ROWS // MOE_SUB, dtype=I32)
    is_last = jnp.any((blk[:, None] == csub[None, :] - 1) & (nsub_e[None, :] > 0), axis=1)
    zmask = (is_last | (blk >= csub[-1])).astype(I32)
    nsup_e = (nsub_e + MOE_NSB - 1) // MOE_NSB
    csup = jnp.cumsum(nsup_e)
    total = csup[-1]
    s = jnp.arange(MOE_SMAX, dtype=I32)
    e_raw = jnp.minimum(jnp.searchsorted(csup, s, side="right").astype(I32), N_EXPERTS - 1)
    valid = s < total
    e_last = e_raw[jnp.maximum(total - 1, 0)]
    e_s = jnp.where(valid, e_raw, e_last)
    local = s - (csup[e_s] - nsup_e[e_s])
    row_s = jnp.where(valid, prow_e[e_s] + local * MOE_TS, 0)
    n_s = jnp.where(valid, jnp.clip(nsub_e[e_s] - local * MOE_NSB, 0, MOE_NSB), 0)
    return dest, zmask, e_s.astype(I32), row_s.astype(I32), n_s.astype(I32)


def _pad_prev(state, width):
    b, _, c = state.shape
    return jnp.pad(state, ((0, 0), (SUBLANES - (width - 1), 0), (0, 0))).reshape(b * SUBLANES, c)


def kernel(x_prompt, x_sample, p_prompt, p_sample, state_conv_a, state_conv_ssm, state_ssm, g_mix, w_in, conv_a_w, ssm_conv_w, ssm_conv_b, dt_bias, a_log, d_skip, ssm_norm_g, w_branch_a, w_branch_b, w_out, g_ffn, w_router, b_router, w_up, b_up, w_down, b_down, g_ple, w_ple_gate, w_ple_up, g_final):
    xp = x_prompt.reshape(NP_TOK, D_MODEL)
    xs = x_sample.reshape(NS_TOK, D_MODEL)
    g_mix2 = g_mix[0].reshape(1, D_MODEL)

    w_in_t = jnp.swapaxes(w_in, 1, 2).reshape(-1, D_MODEL)
    tn = 512
    proj = _rms_matmul(xp, xs, g_mix2, w_in_t,
                       lambda j: j * tn + (j // (N_MAIN // tn)) * (COL_GATE - N_MAIN),
                       N_PROJ, 1024, tn, "in_proj")
    dt_raw = _rms_matmul(xp, xs, g_mix2, w_in_t, lambda j: COL_DT + j * SSM_HEADS, SSM_HEADS, 1024, SSM_HEADS,
                         "in_proj_dt")

    ya_p, ua_tail = _conv_a_prompt(proj, conv_a_w[0])
    ya_s, ua_s = _conv_a_sample(proj, conv_a_w[0], _pad_prev(state_conv_a[0], CONV_A_W))
    conv_a_prompt = ua_tail[:, SUBLANES - (CONV_A_W - 1):, :]
    conv_a_sample = ua_s.reshape(DEC_BATCH, DEC_SEQ, D_CONV)[:, DEC_SEQ - (CONV_A_W - 1):, :]

    conv_b2 = ssm_conv_b[0].reshape(1, D_XBC)
    xbc_p = _conv_m_prompt(proj, ssm_conv_w[0], conv_b2)
    xbc_s = _conv_m_sample(proj, ssm_conv_w[0], conv_b2, _pad_prev(state_conv_ssm[0], SSM_CONV_W))
    proj3 = proj.reshape(N_TOK // DEC_SEQ, DEC_SEQ, N_PROJ)
    t0 = DEC_SEQ - (SSM_CONV_W - 1)
    conv_m_prompt = proj3[SEQ // DEC_SEQ - 1:NP_TOK // DEC_SEQ:SEQ // DEC_SEQ, t0:, COL_XBC:COL_XBC + D_XBC]
    conv_m_sample = proj3[NP_TOK // DEC_SEQ:, t0:, COL_XBC:COL_XBC + D_XBC]

    ssd_w = (dt_bias[0].reshape(1, SSM_HEADS), dt_bias[0].reshape(SSM_HEADS, 1),
             a_log[0].reshape(1, SSM_HEADS), a_log[0].reshape(SSM_HEADS, 1),
             jnp.repeat(d_skip[0], SSM_HEAD_DIM).reshape(1, D_INNER), ssm_norm_g[0].reshape(1, D_INNER))
    ym_p, ssm_p = _ssd_call(xbc_p, proj, dt_raw, *ssd_w)
    ym_s, ssm_s = _ssd_step_call(xbc_s, proj, dt_raw, *ssd_w, state_ssm)
    ssm_shape = (SSM_HEADS, SSM_HEAD_DIM, SSM_STATE)

    ya = jnp.concatenate([ya_p, ya_s], axis=0)
    ym = jnp.concatenate([ym_p, ym_s.astype(BF16)], axis=0)
    merged = _merge(ya, ym, w_branch_a[0], w_branch_b[0], proj)
    h = _resid_matmul(merged, w_out[0], xp, xs)

    wr_pad = jnp.pad(w_router[0], ((0, 0), (0, LANES - N_EXPERTS)))
    br_pad = jnp.pad(b_router[0].reshape(1, N_EXPERTS), ((0, 0), (0, LANES - N_EXPERTS)), constant_values=-1e30)
    g_ffn2 = g_ffn[0].reshape(1, D_MODEL)
    eidx, gate_w, rank, cnt = _router(h, g_ffn2, wr_pad, br_pad)
    counts = cnt[0, :N_EXPERTS].astype(I32)
    dest, zmask, sup_e, sup_row, sup_n = _routing_tables(counts, eidx[:, :TOP_K], rank[:, :TOP_K])
    xs_rows = _dispatch(dest, zmask, h, g_ffn2)
    yb_rows = _experts(sup_e, sup_row, sup_n, xs_rows, w_up[0], b_up[0], w_down[0], b_down[0])

    tail_w = (g_ple[0].reshape(1, D_MODEL), w_ple_gate[0])
    y_prompt = _combine(dest, yb_rows, h, gate_w, *tail_w, p_prompt.reshape(NP_TOK, D_PLE), w_ple_up[0],
                        g_final.reshape(1, D_MODEL), 0, "combine_prompt").reshape(BATCH, SEQ, D_MODEL)
    y_sample = _combine(dest, yb_rows, h, gate_w, *tail_w, p_sample.reshape(NS_TOK, D_PLE), w_ple_up[0],
                        g_final.reshape(1, D_MODEL), NP_TOK, "combine_sample").reshape(DEC_BATCH, DEC_SEQ, D_MODEL)
    return (y_prompt, y_sample,
            conv_a_prompt[None], conv_m_prompt[None], ssm_p.reshape((1, BATCH) + ssm_shape),
            conv_a_sample[None], conv_m_sample[None], ssm_s)
```

```python
import functools

import jax
import jax.numpy as jnp
from jax import lax
from jax.experimental import pallas as pl
from jax.experimental.pallas import tpu as pltpu

F32 = jnp.float32
BF16 = jnp.bfloat16
I32 = jnp.int32
HI = lax.Precision.HIGHEST

D_MODEL = 2048
BATCH = 4
SEQ = 2048
DEC_BATCH = 128
DEC_SEQ = 8
D_PLE = 256
D_CONV = D_MODEL
CONV_A_W = 3
D_INNER = 2 * D_MODEL
SSM_HEAD_DIM = 64
SSM_HEADS = D_INNER // SSM_HEAD_DIM
SSM_GROUPS = 8
SSM_STATE = 128
SSM_CONV_W = 4
SSM_CHUNK = 128
D_XBC = D_INNER + 2 * SSM_GROUPS * SSM_STATE
N_EXPERTS = 32
TOP_K = 4
D_FF = D_MODEL
SWIGLU_LIMIT = 7.0
SWIGLU_ALPHA = 1.702
EPS = 1e-6

NP_TOK = BATCH * SEQ
NS_TOK = DEC_BATCH * DEC_SEQ
N_TOK = NP_TOK + NS_TOK
N_ASSIGN = N_TOK * TOP_K

LANES = 128
SUBLANES = 8
ROW_TILE = D_MODEL // LANES

COL_AX = 0
COL_AB = D_CONV
COL_AC = 2 * D_CONV
COL_Z = 3 * D_CONV
COL_XBC = COL_Z + D_INNER
COL_DT = COL_XBC + D_XBC
COL_GATE = COL_DT + SSM_HEADS
N_MAIN = COL_DT
N_PROJ = N_MAIN + 2 * D_MODEL

MOE_SUB = 256
MOE_TS = 1280
MOE_NSB = MOE_TS // MOE_SUB
MOE_CHUNK = MOE_TS // 2
MOE_ROWS = N_ASSIGN + N_EXPERTS * MOE_SUB
MOE_SMAX = N_ASSIGN // MOE_TS + N_EXPERTS
MOE_TN = 512
MOE_J1 = D_FF // MOE_TN
MOE_J2 = D_MODEL // MOE_TN

MIB = 1024 * 1024


def _cparams(n_grid, vmem_mib):
    return pltpu.CompilerParams(dimension_semantics=("arbitrary",) * n_grid,
                                vmem_limit_bytes=vmem_mib * MIB)


def _sigmoid(x):
    return 1.0 / (1.0 + jnp.exp(-x))


def _softplus(x):
    return jnp.maximum(x, 0.0) + jnp.log1p(jnp.exp(-jnp.abs(x)))


def _group_specs(tm, width, n_grid):
    npt = NP_TOK // tm
    if n_grid == 1:
        return (pl.BlockSpec((tm, width), lambda i: (jnp.minimum(i, npt - 1), 0)),
                pl.BlockSpec((tm, width), lambda i: (jnp.maximum(i - npt, 0), 0)))
    return (pl.BlockSpec((tm, width), lambda i, j, *_: (jnp.minimum(i, npt - 1), 0)),
            pl.BlockSpec((tm, width), lambda i, j, *_: (jnp.maximum(i - npt, 0), 0)))


def _rms_mm_body(npt, xp_ref, xs_ref, g_ref, w_ref, o_ref, n_ref):
    i = pl.program_id(0)
    first = pl.program_id(1) == 0

    def norm(x_ref):
        x = x_ref[...]
        ms = jnp.mean(x * x, axis=-1, keepdims=True)
        n_ref[...] = (x * lax.rsqrt(ms + EPS) * g_ref[...]).astype(BF16)

    @pl.when(jnp.logical_and(first, i < npt))
    def _():
        norm(xp_ref)

    @pl.when(jnp.logical_and(first, i >= npt))
    def _():
        norm(xs_ref)

    o_ref[...] = lax.dot_general(n_ref[...], w_ref[...].astype(BF16), (((1,), (1,)), ((), ())),
                                 preferred_element_type=F32)


def _rms_matmul(xp, xs, g, w_t, feature_offset, ncols, tm, tn, name):
    k = xp.shape[1]
    xp_spec, xs_spec = _group_specs(tm, k, 2)
    return pl.pallas_call(
        functools.partial(_rms_mm_body, NP_TOK // tm),
        grid=(N_TOK // tm, ncols // tn),
        in_specs=[xp_spec, xs_spec,
                  pl.BlockSpec((1, k), lambda i, j: (0, 0)),
                  pl.BlockSpec((pl.Element(tn), pl.Element(k)),
                               lambda i, j: (pl.multiple_of(feature_offset(j), SUBLANES), 0))],
        out_specs=pl.BlockSpec((tm, tn), lambda i, j: (i, j)),
        out_shape=jax.ShapeDtypeStruct((N_TOK, ncols), F32),
        scratch_shapes=[pltpu.VMEM((tm, k), BF16)],
        compiler_params=_cparams(2, 56),
        name=name,
    )(xp, xs, g, w_t)


def _causal_taps(u, w_ref, width, rowmod, prev):
    rows = u.shape[0]
    acc = u * w_ref[width - 1:width, :]
    for s in range(1, width):
        shifted = pltpu.roll(u, s, 0)
        if prev is None:
            fill = jnp.zeros_like(u)
        else:
            fill = pltpu.roll(prev, rows - SUBLANES + s, 0)
        acc = acc + jnp.where(rowmod >= s, shifted, fill) * w_ref[width - 1 - s:width - s, :]
    return acc


def _row_in_seq(shape, period):
    r = lax.broadcasted_iota(I32, shape, 0)
    return r if period >= shape[0] else r & (period - 1)


def _conv_a_body(period, has_prev, *refs):
    if has_prev:
        ax_ref, ab_ref, ac_ref, w_ref, prev_ref, ya_ref, u_ref = refs
        prev = prev_ref[...]
    else:
        ax_ref, ab_ref, ac_ref, w_ref, ya_ref, u_ref = refs
        prev = None
    u = ac_ref[...] * ax_ref[...]
    v = _causal_taps(u, w_ref, CONV_A_W, _row_in_seq(u.shape, period), prev)
    ya_ref[...] = (ab_ref[...] * v).astype(BF16)
    if has_prev:
        u_ref[...] = u
    else:
        u_ref[...] = u[u.shape[0] - SUBLANES:, :]


def _conv_m_body(period, has_prev, *refs):
    if has_prev:
        x_ref, w_ref, b_ref, prev_ref, o_ref = refs
        prev = prev_ref[...]
    else:
        x_ref, w_ref, b_ref, o_ref = refs
        prev = None
    u = x_ref[...]
    v = _causal_taps(u, w_ref, SSM_CONV_W, _row_in_seq(u.shape, period), prev) + b_ref[...]
    o_ref[...] = v * _sigmoid(v)


CONV_TC = 256


def _conv_a_prompt(proj, conv_w):
    nb = D_CONV // CONV_TC
    blk = lambda off: pl.BlockSpec((SEQ, CONV_TC), lambda b, c, off=off: (b, off // CONV_TC + c))
    return pl.pallas_call(
        functools.partial(_conv_a_body, SEQ, False),
        grid=(BATCH, nb),
        in_specs=[blk(COL_AX), blk(COL_AB), blk(COL_AC),
                  pl.BlockSpec((CONV_A_W, CONV_TC), lambda b, c: (0, c))],
        out_specs=[pl.BlockSpec((SEQ, CONV_TC), lambda b, c: (b, c)),
                   pl.BlockSpec((None, SUBLANES, CONV_TC), lambda b, c: (b, 0, c))],
        out_shape=[jax.ShapeDtypeStruct((NP_TOK, D_CONV), BF16),
                   jax.ShapeDtypeStruct((BATCH, SUBLANES, D_CONV), F32)],
        compiler_params=_cparams(2, 48),
        name="conv_a_prompt",
    )(proj, proj, proj, conv_w)


def _conv_a_sample(proj, conv_w, prev_pad):
    nb = D_CONV // CONV_TC
    rb = NP_TOK // NS_TOK
    blk = lambda off: pl.BlockSpec((NS_TOK, CONV_TC), lambda c, off=off: (rb, off // CONV_TC + c))
    own = pl.BlockSpec((NS_TOK, CONV_TC), lambda c: (0, c))
    return pl.pallas_call(
        functools.partial(_conv_a_body, DEC_SEQ, True),
        grid=(nb,),
        in_specs=[blk(COL_AX), blk(COL_AB), blk(COL_AC),
                  pl.BlockSpec((CONV_A_W, CONV_TC), lambda c: (0, c)), own],
        out_specs=[own, own],
        out_shape=[jax.ShapeDtypeStruct((NS_TOK, D_CONV), BF16),
                   jax.ShapeDtypeStruct((NS_TOK, D_CONV), F32)],
        compiler_params=_cparams(1, 48),
        name="conv_a_sample",
    )(proj, proj, proj, conv_w, prev_pad)


def _conv_m_prompt(proj, conv_w, conv_b):
    nb = D_XBC // CONV_TC
    return pl.pallas_call(
        functools.partial(_conv_m_body, SEQ, False),
        grid=(BATCH, nb),
        in_specs=[pl.BlockSpec((SEQ, CONV_TC), lambda b, c: (b, COL_XBC // CONV_TC + c)),
                  pl.BlockSpec((SSM_CONV_W, CONV_TC), lambda b, c: (0, c)),
                  pl.BlockSpec((1, CONV_TC), lambda b, c: (0, c))],
        out_specs=pl.BlockSpec((SEQ, CONV_TC), lambda b, c: (b, c)),
        out_shape=jax.ShapeDtypeStruct((NP_TOK, D_XBC), F32),
        compiler_params=_cparams(2, 48),
        name="conv_m_prompt",
    )(proj, conv_w, conv_b)


def _conv_m_sample(proj, conv_w, conv_b, prev_pad):
    nb = D_XBC // CONV_TC
    rb = NP_TOK // NS_TOK
    own = pl.BlockSpec((NS_TOK, CONV_TC), lambda c: (0, c))
    return pl.pallas_call(
        functools.partial(_conv_m_body, DEC_SEQ, True),
        grid=(nb,),
        in_specs=[pl.BlockSpec((NS_TOK, CONV_TC), lambda c: (rb, COL_XBC // CONV_TC + c)),
                  pl.BlockSpec((SSM_CONV_W, CONV_TC), lambda c: (0, c)),
                  pl.BlockSpec((1, CONV_TC), lambda c: (0, c)), own],
        out_specs=own,
        out_shape=jax.ShapeDtypeStruct((NS_TOK, D_XBC), F32),
        compiler_params=_cparams(1, 48),
        name="conv_m_sample",
    )(proj, conv_w, conv_b, prev_pad)


def _ssd_body(xbc_ref, zlo_ref, zhi_ref, dtr_ref, dtb_ref, dtbc_ref, alog_ref, alogc_ref, dsk_ref, ng_ref,
              ym_ref, hl_ref, ht_ref, y_ref):
    L = SSM_CHUNK
    H = SSM_HEADS
    c = pl.program_id(1)

    @pl.when(c == 0)
    def _():
        ht_ref[...] = jnp.zeros_like(ht_ref)

    dtr = dtr_ref[...]
    dt = _softplus(dtr + dtb_ref[...])
    eye = (lax.broadcasted_iota(I32, (H, H), 0) == lax.broadcasted_iota(I32, (H, H), 1)).astype(F32)
    dtr_t = lax.dot_general(eye, dtr, (((1,), (1,)), ((), ())), precision=HI,
                            preferred_element_type=F32)
    dt_t = _softplus(dtr_t + dtbc_ref[...])
    adt = dt * (-jnp.exp(alog_ref[...]))
    adt_t = dt_t * (-jnp.exp(alogc_ref[...]))
    li = lax.broadcasted_iota(I32, (L, L), 0)
    si = lax.broadcasted_iota(I32, (L, L), 1)
    causal = li >= si
    a_cum = jnp.dot(causal.astype(F32), adt, precision=HI, preferred_element_type=F32)
    a_cum_t = jnp.dot(adt_t, (li <= si).astype(F32), precision=HI, preferred_element_type=F32)
    ea = jnp.exp(a_cum)
    a_tot_t = a_cum_t[:, L - 1:L]
    dte_t = jnp.exp(a_tot_t - a_cum_t)
    ea_tot_t = jnp.exp(a_tot_t)

    lane = lax.broadcasted_iota(I32, (L, LANES), 1)
    lo = lane < SSM_HEAD_DIM
    lane1 = lax.broadcasted_iota(I32, (1, LANES), 1) < SSM_HEAD_DIM
    gn = SSM_GROUPS * SSM_STATE
    for g in range(SSM_GROUPS):
        bg = xbc_ref[:, D_INNER + g * SSM_STATE:D_INNER + (g + 1) * SSM_STATE]
        cg = xbc_ref[:, D_INNER + gn + g * SSM_STATE:D_INNER + gn + (g + 1) * SSM_STATE]
        bg_t = bg.T
        cb = jnp.dot(cg.astype(BF16), bg_t.astype(BF16), preferred_element_type=F32)
        for jj in range(SSM_HEADS // SSM_GROUPS // 2):
            j = g * (SSM_HEADS // SSM_GROUPS // 2) + jj
            cols = slice(j * LANES, (j + 1) * LANES)
            lhs1, lhs2 = [], []
            for h in (2 * j, 2 * j + 1):
                diff = a_cum[:, h:h + 1] - a_cum_t[h:h + 1, :]
                decay = jnp.exp(jnp.where(causal, diff, -1e30))
                lhs1.append((cb * decay).astype(BF16))
            for h in (2 * j, 2 * j + 1):
                lhs1.append((cg * ea[:, h:h + 1]).astype(BF16))
                lhs2.append((bg_t * dte_t[h:h + 1, :]).astype(BF16))
            dt_pair = jnp.where(lo, dt[:, 2 * j:2 * j + 1], dt[:, 2 * j + 1:2 * j + 2])
            xdt = xbc_ref[:, cols] * dt_pair
            zero = jnp.zeros_like(xdt)
            x_top = jnp.where(lo, xdt, zero).astype(BF16)
            x_bot = jnp.where(lo, zero, xdt).astype(BF16)
            ht = ht_ref[:, cols]
            h_top = jnp.where(lo, ht, zero).astype(BF16)
            h_bot = jnp.where(lo, zero, ht).astype(BF16)
            y_ref[:, cols] = jnp.dot(jnp.concatenate(lhs1, axis=1),
                                     jnp.concatenate([x_top, x_bot, h_top, h_bot], axis=0),
                                     preferred_element_type=F32)
            upd = jnp.dot(jnp.concatenate(lhs2, axis=1), jnp.concatenate([x_top, x_bot], axis=0),
                          preferred_element_type=F32)
            keep = jnp.where(lane1, ea_tot_t[2 * j:2 * j + 1, :], ea_tot_t[2 * j + 1:2 * j + 2, :])
            ht_ref[:, cols] = ht * keep + upd

    gw = D_INNER // SSM_GROUPS
    for g in range(SSM_GROUPS):
        cols = slice(g * gw, (g + 1) * gw)
        zr = zlo_ref if g < SSM_GROUPS // 2 else zhi_ref
        zc = slice((g % (SSM_GROUPS // 2)) * gw, (g % (SSM_GROUPS // 2) + 1) * gw)
        z = zr[:, zc]
        y = y_ref[:, cols] + dsk_ref[:, cols] * xbc_ref[:, cols]
        y = y * (z * _sigmoid(z))
        y = y * lax.rsqrt(jnp.mean(y * y, axis=-1, keepdims=True) + EPS)
        ym_ref[:, cols] = (y * ng_ref[:, cols]).astype(ym_ref.dtype)

    @pl.when(c == pl.num_programs(1) - 1)
    def _():
        hl_ref[...] = ht_ref[...].T


def _ssd_call(xbc, proj, dt_raw, dtb, dtbc, alog, alogc, dsk, ng):
    L = SSM_CHUNK
    n_chunk = SEQ // L
    hp = SSM_HEADS * SSM_HEAD_DIM
    zb = D_INNER // 2
    vec = lambda shape: pl.BlockSpec(shape, lambda b, c: (0, 0))
    return pl.pallas_call(
        _ssd_body,
        grid=(BATCH, n_chunk),
        in_specs=[pl.BlockSpec((L, D_XBC), lambda b, c: (b * n_chunk + c, 0)),
                  pl.BlockSpec((L, zb), lambda b, c: (b * n_chunk + c, COL_Z // zb)),
                  pl.BlockSpec((L, zb), lambda b, c: (b * n_chunk + c, COL_Z // zb + 1)),
                  pl.BlockSpec((L, SSM_HEADS), lambda b, c: (b * n_chunk + c, 0)),
                  vec((1, SSM_HEADS)), vec((SSM_HEADS, 1)), vec((1, SSM_HEADS)), vec((SSM_HEADS, 1)),
                  vec((1, D_INNER)), vec((1, D_INNER))],
        out_specs=[pl.BlockSpec((L, D_INNER), lambda b, c: (b * n_chunk + c, 0)),
                   pl.BlockSpec((None, hp, SSM_STATE), lambda b, c: (b, 0, 0))],
        out_shape=[jax.ShapeDtypeStruct((NP_TOK, D_INNER), BF16),
                   jax.ShapeDtypeStruct((BATCH, hp, SSM_STATE), F32)],
        scratch_shapes=[pltpu.VMEM((SSM_STATE, hp), F32), pltpu.VMEM((L, hp), F32)],
        compiler_params=_cparams(2, 48),
        name="ssd_prompt",
    )(xbc, proj, proj, dt_raw, dtb, dtbc, alog, alogc, dsk, ng)


SSD_STEP_SEQS = 2


def _ssd_step_body(xbc_in, zlo_ref, zhi_ref, dtr_in, dtb_ref, dtbc_ref, alog_ref, alogc_ref, dsk_ref, ng_ref,
                   h0_ref, ym_ref, hl_ref, xp_ref, dp_ref):
    T = DEC_SEQ

    @pl.when(pl.program_id(0) == 0)
    def _():
        xp_ref[...] = jnp.zeros_like(xp_ref)
        dp_ref[...] = jnp.zeros_like(dp_ref)

    for s in range(SSD_STEP_SEQS):
        rows = pl.ds(s * T, T)
        _ssd_step_one(xbc_in.at[rows], zlo_ref.at[rows], zhi_ref.at[rows], dtr_in.at[rows], dtb_ref, dtbc_ref,
                      alog_ref, alogc_ref, dsk_ref, ng_ref, h0_ref.at[s], ym_ref.at[rows], hl_ref.at[s],
                      xp_ref.at[s], dp_ref.at[s])


def _ssd_step_one(xbc_in, zlo_ref, zhi_ref, dtr_in, dtb_ref, dtbc_ref, alog_ref, alogc_ref, dsk_ref, ng_ref,
                  h0_ref, ym_ref, hl_ref, xp_ref, dp_ref):
    T = DEC_SEQ
    L = SSM_CHUNK
    H = SSM_HEADS
    P = SSM_HEAD_DIM

    xp_ref[0:T, :] = xbc_in[...]
    dp_ref[0:T, :] = dtr_in[...]

    dt8 = _softplus(dtr_in[...] + dtb_ref[...])
    a_cum8 = dt8 * (-jnp.exp(alog_ref[...]))
    row8 = lax.broadcasted_iota(I32, (T, H), 0)
    k = 1
    while k < T:
        a_cum8 = a_cum8 + jnp.where(row8 >= k, pltpu.roll(a_cum8, k, 0), 0.0)
        k *= 2
    ea8 = jnp.exp(a_cum8)
    dte8 = jnp.exp(a_cum8[T - 1:T, :] - a_cum8)
    eye = (lax.broadcasted_iota(I32, (H, H), 0) == lax.broadcasted_iota(I32, (H, H), 1)).astype(F32)
    dtr_t = lax.dot_general(eye, dp_ref[...], (((1,), (1,)), ((), ())), precision=HI,
                            preferred_element_type=F32)
    dt_t = jnp.where(lax.broadcasted_iota(I32, (H, L), 1) < T, _softplus(dtr_t + dtbc_ref[...]), 0.0)
    adt_t = dt_t * (-jnp.exp(alogc_ref[...]))
    upper = (lax.broadcasted_iota(I32, (L, L), 0) <= lax.broadcasted_iota(I32, (L, L), 1)).astype(F32)
    a_cum_t = jnp.dot(adt_t, upper, precision=HI, preferred_element_type=F32)
    ea_tot_t = jnp.exp(a_cum_t[:, L - 1:L])

    causal8 = lax.broadcasted_iota(I32, (T, L), 0) >= lax.broadcasted_iota(I32, (T, L), 1)
    zeros_t = jnp.zeros((T, LANES), F32)
    gn = SSM_GROUPS * SSM_STATE
    gw = D_INNER // SSM_GROUPS
    hg = SSM_HEADS // SSM_GROUPS
    zeros_pad = jnp.zeros((L - T, gw), F32)
    lane_head = lax.broadcasted_iota(I32, (T, gw), 1) // P

    def per_head(table, g):
        out = jnp.zeros((T, gw), F32)
        for hh in range(hg):
            h = g * hg + hh
            out = jnp.where(lane_head == hh, table[:, h:h + 1], out)
        return out

    for g in range(SSM_GROUPS):
        heads = range(g * hg, (g + 1) * hg)
        cols = slice(g * gw, (g + 1) * gw)
        bgp = xp_ref[:, D_INNER + g * SSM_STATE:D_INNER + (g + 1) * SSM_STATE].astype(BF16)
        cg8 = xbc_in[:, D_INNER + gn + g * SSM_STATE:D_INNER + gn + (g + 1) * SSM_STATE]
        cg16 = jnp.concatenate([cg8, zeros_t], axis=0).astype(BF16)
        cb = lax.dot_general(cg16, bgp, (((1,), (1,)), ((), ())), preferred_element_type=F32)[0:T]

        def decay(h):
            diff = a_cum8[:, h:h + 1] - a_cum_t[h:h + 1, :]
            return jnp.exp(jnp.where(causal8, diff, -1e30))

        m = jnp.concatenate([cb * decay(h) for h in heads], axis=0).astype(BF16)
        cd = jnp.concatenate([cg8 * ea8[:, h:h + 1] for h in heads], axis=0).astype(BF16)
        xdt8 = xbc_in[:, cols] * per_head(dt8, g)
        xd8 = xdt8 * per_head(dte8, g)
        xdt_pad = jnp.concatenate([xdt8, zeros_pad], axis=0).astype(BF16)
        xd_pad = jnp.concatenate([xd8, zeros_pad], axis=0).astype(BF16)
        hp = h0_ref[g * hg:(g + 1) * hg].reshape(hg * P, SSM_STATE)
        out = (jnp.dot(m, xdt_pad, preferred_element_type=F32)
               + lax.dot_general(cd, hp.astype(BF16), (((1,), (1,)), ((), ())),
                                 preferred_element_type=F32))
        y_ssd = jnp.zeros((T, gw), F32)
        for hh in range(hg):
            y_ssd = jnp.where(lane_head == hh, out[hh * T:(hh + 1) * T], y_ssd)
        upd = lax.dot_general(xd_pad, bgp, (((0,), (0,)), ((), ())), preferred_element_type=F32)
        keep = jnp.concatenate([jnp.broadcast_to(ea_tot_t[h:h + 1, :], (P, SSM_STATE)) for h in heads], axis=0)
        hl_ref[g * hg:(g + 1) * hg] = (hp * keep + upd).reshape(hg, P, SSM_STATE)

        zr = zlo_ref if g < SSM_GROUPS // 2 else zhi_ref
        zc = slice((g % (SSM_GROUPS // 2)) * gw, (g % (SSM_GROUPS // 2) + 1) * gw)
        z = zr[:, zc]
        y = y_ssd + dsk_ref[:, cols] * xbc_in[:, cols]
        y = y * (z * _sigmoid(z))
        y = y * lax.rsqrt(jnp.mean(y * y, axis=-1, keepdims=True) + EPS)
        ym_ref[:, cols] = y * ng_ref[:, cols]


def _ssd_step_call(xbc, proj, dt_raw, dtb, dtbc, alog, alogc, dsk, ng, state):
    ns = SSD_STEP_SEQS
    T = ns * DEC_SEQ
    rb0 = NP_TOK // T
    zb = D_INNER // 2
    st = pl.BlockSpec((None, ns, SSM_HEADS, SSM_HEAD_DIM, SSM_STATE), lambda b: (0, b, 0, 0, 0))
    vec = lambda shape: pl.BlockSpec(shape, lambda b: (0, 0))
    return pl.pallas_call(
        _ssd_step_body,
        grid=(DEC_BATCH // ns,),
        in_specs=[pl.BlockSpec((T, D_XBC), lambda b: (b, 0)),
                  pl.BlockSpec((T, zb), lambda b: (rb0 + b, COL_Z // zb)),
                  pl.BlockSpec((T, zb), lambda b: (rb0 + b, COL_Z // zb + 1)),
                  pl.BlockSpec((T, SSM_HEADS), lambda b: (rb0 + b, 0)),
                  vec((1, SSM_HEADS)), vec((SSM_HEADS, 1)), vec((1, SSM_HEADS)), vec((SSM_HEADS, 1)),
                  vec((1, D_INNER)), vec((1, D_INNER)), st],
        out_specs=[pl.BlockSpec((T, D_INNER), lambda b: (b, 0)), st],
        out_shape=[jax.ShapeDtypeStruct((NS_TOK, D_INNER), F32),
                   jax.ShapeDtypeStruct(state.shape, F32)],
        scratch_shapes=[pltpu.VMEM((ns, SSM_CHUNK, D_XBC), F32), pltpu.VMEM((ns, SSM_CHUNK, SSM_HEADS), F32)],
        compiler_params=_cparams(1, 40),
        name="ssd_sample",
    )(xbc, proj, proj, dt_raw, dtb, dtbc, alog, alogc, dsk, ng, state)


def _merge_body(ya_ref, ym_ref, wa_ref, wb_ref, ga_ref, gb_ref, o_ref):
    a = jnp.dot(ya_ref[...], wa_ref[...].astype(BF16), preferred_element_type=F32)
    b = jnp.dot(ym_ref[...], wb_ref[...].astype(BF16), preferred_element_type=F32)
    o_ref[...] = (_sigmoid(ga_ref[...]) * a + _sigmoid(gb_ref[...]) * b).astype(BF16)


def _merge(ya, ym, wa, wb, proj, row0, name):
    tm, tn = 1024, 256
    g0 = N_MAIN // tn
    tile0 = row0 // tm
    n_rows = ya.shape[0]
    return pl.pallas_call(
        _merge_body,
        grid=(n_rows // tm, D_MODEL // tn),
        in_specs=[pl.BlockSpec((tm, D_CONV), lambda i, j: (i, 0)),
                  pl.BlockSpec((tm, D_INNER), lambda i, j: (i, 0)),
                  pl.BlockSpec((D_CONV, tn), lambda i, j: (0, j)),
                  pl.BlockSpec((D_INNER, tn), lambda i, j: (0, j)),
                  pl.BlockSpec((tm, tn), lambda i, j: (tile0 + i, g0 + j)),
                  pl.BlockSpec((tm, tn), lambda i, j: (tile0 + i, g0 + D_MODEL // tn + j))],
        out_specs=pl.BlockSpec((tm, tn), lambda i, j: (i, j)),
        out_shape=jax.ShapeDtypeStruct((n_rows, D_MODEL), BF16),
        compiler_params=_cparams(2, 52),
        name=name,
    )(ya, ym, wa, wb, proj, proj)


def _resid_body(m_ref, w_ref, x_ref, o_ref):
    o_ref[...] = x_ref[...] + jnp.dot(m_ref[...], w_ref[...].astype(BF16), preferred_element_type=F32)


def _resid_matmul(m, w, x, name):
    tm, tn = 1024, 512
    n_rows = m.shape[0]
    return pl.pallas_call(
        _resid_body,
        grid=(n_rows // tm, D_MODEL // tn),
        in_specs=[pl.BlockSpec((tm, D_MODEL), lambda i, j: (i, 0)),
                  pl.BlockSpec((D_MODEL, tn), lambda i, j: (0, j)),
                  pl.BlockSpec((tm, tn), lambda i, j: (i, j))],
        out_specs=pl.BlockSpec((tm, tn), lambda i, j: (i, j)),
        out_shape=jax.ShapeDtypeStruct((n_rows, D_MODEL), F32),
        compiler_params=_cparams(2, 48),
        name=name,
    )(m, w, x)


ROUTER_TM = 512


def _router_body(hp_ref, hs_ref, g_ref, wr_ref, br_ref, ei_ref, gt_ref, rk_ref, cnt_ref, carry_ref):
    tm = ROUTER_TM

    @pl.when(pl.program_id(0) == 0)
    def _():
        carry_ref[...] = jnp.zeros_like(carry_ref)

    h = jnp.where(pl.program_id(0) < NP_TOK // tm, hp_ref[...], hs_ref[...])
    n2 = h * lax.rsqrt(jnp.mean(h * h, axis=-1, keepdims=True) + EPS) * g_ref[...]
    logits = jnp.dot(n2, wr_ref[...], precision=HI, preferred_element_type=F32) + br_ref[...]

    lane = lax.broadcasted_iota(I32, (tm, LANES), 1).astype(F32)
    work = logits
    vals, idxs = [], []
    for _ in range(TOP_K):
        m = jnp.max(work, axis=-1, keepdims=True)
        idx = jnp.min(jnp.where(work == m, lane, float(LANES)), axis=-1, keepdims=True)
        vals.append(m)
        idxs.append(idx)
        work = jnp.where(lane == idx, -jnp.inf, work)
    ex = [jnp.exp(v - vals[0]) for v in vals]
    den = ex[0] + ex[1] + ex[2] + ex[3]
    sel = jnp.zeros((tm, LANES), F32)
    for idx in idxs:
        sel = sel + jnp.where(lane == idx, 1.0, 0.0)
    before = (lax.broadcasted_iota(I32, (tm, tm), 0) > lax.broadcasted_iota(I32, (tm, tm), 1))
    rank_e = jnp.dot(before.astype(BF16), sel.astype(BF16), preferred_element_type=F32) + carry_ref[0:1, :]
    carry_ref[...] = carry_ref[...] + jnp.sum(sel, axis=0, keepdims=True)
    ei = jnp.zeros((tm, LANES), F32)
    gt = jnp.zeros((tm, LANES), F32)
    rk = jnp.zeros((tm, LANES), F32)
    for k in range(TOP_K):
        rank_k = jnp.sum(jnp.where(lane == idxs[k], rank_e, 0.0), axis=-1, keepdims=True)
        here = lane == float(k)
        ei = jnp.where(here, idxs[k], ei)
        gt = jnp.where(here, ex[k] / den, gt)
        rk = jnp.where(here, rank_k, rk)
    ei_ref[...] = ei.astype(I32)
    gt_ref[...] = gt
    rk_ref[...] = rk.astype(I32)
    cnt_ref[...] = carry_ref[...]


def _router(hp, hs, g, wr_pad, br_pad):
    tm = ROUTER_TM
    tok = pl.BlockSpec((tm, LANES), lambda i: (i, 0))
    return pl.pallas_call(
        _router_body,
        grid=(N_TOK // tm,),
        in_specs=[*_group_specs(tm, D_MODEL, 1),
                  pl.BlockSpec((1, D_MODEL), lambda i: (0, 0)),
                  pl.BlockSpec((D_MODEL, LANES), lambda i: (0, 0)),
                  pl.BlockSpec((1, LANES), lambda i: (0, 0))],
        out_specs=[tok, tok, tok, pl.BlockSpec((SUBLANES, LANES), lambda i: (0, 0))],
        out_shape=[jax.ShapeDtypeStruct((N_TOK, LANES), I32),
                   jax.ShapeDtypeStruct((N_TOK, LANES), F32),
                   jax.ShapeDtypeStruct((N_TOK, LANES), I32),
                   jax.ShapeDtypeStruct((SUBLANES, LANES), F32)],
        scratch_shapes=[pltpu.VMEM((SUBLANES, LANES), F32)],
        compiler_params=_cparams(1, 48),
        name="router",
    )(hp, hs, g, wr_pad, br_pad)


DISPATCH_TOK = 512


def _row_slab(ref, row):
    start = row * ROW_TILE
    if not isinstance(start, int):
        start = pl.multiple_of(start, ROW_TILE)
    return ref.at[pl.ds(start, ROW_TILE)]


def _row_copy(src, src_row, dst, dst_row, sem):
    return pltpu.make_async_copy(_row_slab(src, src_row), _row_slab(dst, dst_row), sem)


def _dispatch_body(dest_ref, zmask_ref, hp_ref, hs_ref, g_ref, xs_hbm, rows_ref, zero_ref, sem, zsem):
    tm = DISPATCH_TOK
    base = pl.program_id(0) * tm
    sub_rows = MOE_SUB * ROW_TILE

    @pl.when(pl.program_id(0) == 0)
    def _():
        zero_ref[...] = jnp.zeros_like(zero_ref)

        def zero_copy(b):
            return pltpu.make_async_copy(
                zero_ref, xs_hbm.at[pl.ds(pl.multiple_of(b * sub_rows, sub_rows), sub_rows)], zsem)

        def start(b, carry):
            @pl.when(zmask_ref[b] != 0)
            def _():
                zero_copy(b).start()
            return carry

        def wait(b, carry):
            @pl.when(zmask_ref[b] != 0)
            def _():
                zero_copy(b).wait()
            return carry

        lax.fori_loop(0, MOE_ROWS // MOE_SUB, start, 0)
        lax.fori_loop(0, MOE_ROWS // MOE_SUB, wait, 0)

    h = jnp.where(pl.program_id(0) < NP_TOK // tm, hp_ref[...], hs_ref[...])
    n2 = h * lax.rsqrt(jnp.mean(h * h, axis=-1, keepdims=True) + EPS) * g_ref[...]
    for cidx in range(ROW_TILE):
        rows_ref[pl.ds(cidx, tm, stride=ROW_TILE), :] = n2[:, cidx * LANES:(cidx + 1) * LANES]

    def issue(i, carry):
        for k in range(TOP_K):
            _row_copy(rows_ref, i, xs_hbm, dest_ref[(base + i) * TOP_K + k], sem).start()
        return carry

    lax.fori_loop(0, tm, issue, 0)

    for k in range(TOP_K):
        pltpu.make_async_copy(rows_ref, xs_hbm.at[pl.ds(0, tm * ROW_TILE)], sem).wait()


def _dispatch(dest_flat, zmask, hp, hs, g):
    tm = DISPATCH_TOK
    npt = NP_TOK // tm
    return pl.pallas_call(
        _dispatch_body,
        grid_spec=pltpu.PrefetchScalarGridSpec(
            num_scalar_prefetch=2,
            grid=(N_TOK // tm,),
            in_specs=[pl.BlockSpec((tm, D_MODEL), lambda i, d, z: (jnp.minimum(i, npt - 1), 0)),
                      pl.BlockSpec((tm, D_MODEL), lambda i, d, z: (jnp.maximum(i - npt, 0), 0)),
                      pl.BlockSpec((1, D_MODEL), lambda i, d, z: (0, 0))],
            out_specs=pl.BlockSpec(memory_space=pl.ANY),
            scratch_shapes=[pltpu.VMEM((tm * ROW_TILE, LANES), F32),
                            pltpu.VMEM((MOE_SUB * ROW_TILE, LANES), F32),
                            pltpu.SemaphoreType.DMA(()), pltpu.SemaphoreType.DMA(())]),
        out_shape=jax.ShapeDtypeStruct((MOE_ROWS * ROW_TILE, LANES), F32),
        compiler_params=_cparams(1, 32),
        name="moe_dispatch",
    )(dest_flat, zmask, hp, hs, g)


def _expert_body(e_ref, r_ref, n_ref, xs_hbm, wg_ref, wu_ref, bg_ref, bu_ref, wd_ref, bd_ref, yb_hbm,
                 stage_ref, xb_ref, act_ref, sem_in, sem_out):
    del e_ref
    s = pl.program_id(0)
    j = pl.program_id(1)
    nsub = n_ref[s]
    row0 = r_ref[s]
    sub_rows = MOE_SUB * ROW_TILE

    def in_copy(k):
        return pltpu.make_async_copy(
            xs_hbm.at[pl.ds(pl.multiple_of((row0 + k * MOE_SUB) * ROW_TILE, sub_rows), sub_rows)],
            stage_ref.at[pl.ds(k * sub_rows, sub_rows)], sem_in)

    def out_copy(k):
        return pltpu.make_async_copy(
            stage_ref.at[pl.ds(k * sub_rows, sub_rows)],
            yb_hbm.at[pl.ds(pl.multiple_of((row0 + k * MOE_SUB) * ROW_TILE, sub_rows), sub_rows)], sem_out)

    @pl.when(j == 0)
    def _():
        for k in range(MOE_NSB):
            @pl.when(k < nsub)
            def _():
                in_copy(k).start()
        for k in range(MOE_NSB):
            @pl.when(k < nsub)
            def _():
                in_copy(k).wait()
        for k in range(MOE_NSB):
            @pl.when(k < nsub)
            def _():
                for cidx in range(ROW_TILE):
                    xb_ref[k * MOE_SUB:(k + 1) * MOE_SUB, cidx * LANES:(cidx + 1) * LANES] = (
                        stage_ref[pl.ds(k * sub_rows + cidx, MOE_SUB, stride=ROW_TILE), :].astype(BF16))

    full = nsub == MOE_NSB

    @pl.when(j < MOE_J1)
    def _():
        wg = wg_ref[...].astype(BF16)
        wu = wu_ref[...].astype(BF16)

        def up_rows(r0, nrows):
            x = xb_ref[r0:r0 + nrows, :]
            gate = jnp.dot(x, wg, preferred_element_type=F32) + bg_ref[...]
            up = jnp.dot(x, wu, preferred_element_type=F32) + bu_ref[...]
            gate = jnp.minimum(gate, SWIGLU_LIMIT)
            up = jnp.clip(up, -SWIGLU_LIMIT, SWIGLU_LIMIT)
            a = gate * _sigmoid(SWIGLU_ALPHA * gate) * (up + 1.0)
            act_ref[jnp.minimum(j, MOE_J1 - 1), r0:r0 + nrows, :] = a.astype(BF16)

        @pl.when(full)
        def _():
            for r0 in range(0, MOE_TS, MOE_CHUNK):
                up_rows(r0, MOE_CHUNK)

        for k in range(MOE_NSB - 1):
            @pl.when(jnp.logical_and(jnp.logical_not(full), k < nsub))
            def _():
                up_rows(k * MOE_SUB, MOE_SUB)

    per = MOE_TN // LANES
    for jd in range(MOE_J2):
        @pl.when(j == MOE_J1 + jd)
        def _():
            wd = wd_ref[...].astype(BF16)

            def down_rows(r0, nrows):
                y = bd_ref[...] + jnp.dot(act_ref[0, r0:r0 + nrows, :], wd[0:MOE_TN, :],
                                          preferred_element_type=F32)
                for q in range(1, MOE_J1):
                    y = y + jnp.dot(act_ref[q, r0:r0 + nrows, :], wd[q * MOE_TN:(q + 1) * MOE_TN, :],
                                    preferred_element_type=F32)
                for cc in range(per):
                    stage_ref[pl.ds(r0 * ROW_TILE + jd * per + cc, nrows, stride=ROW_TILE), :] = (
                        y[:, cc * LANES:(cc + 1) * LANES])

            @pl.when(full)
            def _():
                for r0 in range(0, MOE_TS, MOE_CHUNK):
                    down_rows(r0, MOE_CHUNK)

            for k in range(MOE_NSB - 1):
                @pl.when(jnp.logical_and(jnp.logical_not(full), k < nsub))
                def _():
                    down_rows(k * MOE_SUB, MOE_SUB)

    @pl.when(j == MOE_J1 + MOE_J2 - 1)
    def _():
        for k in range(MOE_NSB):
            @pl.when(k < nsub)
            def _():
                out_copy(k).start()
        for k in range(MOE_NSB):
            @pl.when(k < nsub)
            def _():
                out_copy(k).wait()


def _experts(sup_e, sup_row, sup_n, xs_rows, w_up, b_up, w_down, b_down):
    def up_j(s, j, e, r, n):
        return jnp.where(n[s] > 0, jnp.minimum(j, MOE_J1 - 1), MOE_J1 - 1)

    def dn_j(s, j, e, r, n):
        return jnp.where(n[s] > 0, jnp.maximum(j - MOE_J1, 0), MOE_J2 - 1)

    b_up3 = b_up.reshape(N_EXPERTS, 1, 2 * D_FF)
    b_down3 = b_down.reshape(N_EXPERTS, 1, D_MODEL)
    return pl.pallas_call(
        _expert_body,
        grid_spec=pltpu.PrefetchScalarGridSpec(
            num_scalar_prefetch=3,
            grid=(MOE_SMAX, MOE_J1 + MOE_J2),
            in_specs=[
                pl.BlockSpec(memory_space=pl.ANY),
                pl.BlockSpec((None, D_MODEL, MOE_TN), lambda s, j, e, r, n: (e[s], 0, up_j(s, j, e, r, n))),
                pl.BlockSpec((None, D_MODEL, MOE_TN),
                             lambda s, j, e, r, n: (e[s], 0, MOE_J1 + up_j(s, j, e, r, n))),
                pl.BlockSpec((None, 1, MOE_TN), lambda s, j, e, r, n: (e[s], 0, up_j(s, j, e, r, n))),
                pl.BlockSpec((None, 1, MOE_TN), lambda s, j, e, r, n: (e[s], 0, MOE_J1 + up_j(s, j, e, r, n))),
                pl.BlockSpec((None, D_FF, MOE_TN), lambda s, j, e, r, n: (e[s], 0, dn_j(s, j, e, r, n))),
                pl.BlockSpec((None, 1, MOE_TN), lambda s, j, e, r, n: (e[s], 0, dn_j(s, j, e, r, n))),
            ],
            out_specs=pl.BlockSpec(memory_space=pl.ANY),
            scratch_shapes=[pltpu.VMEM((MOE_TS * ROW_TILE, LANES), F32),
                            pltpu.VMEM((MOE_TS, D_MODEL), BF16),
                            pltpu.VMEM((MOE_J1, MOE_TS, MOE_TN), BF16),
                            pltpu.SemaphoreType.DMA(()),
                            pltpu.SemaphoreType.DMA(())]),
        out_shape=jax.ShapeDtypeStruct((MOE_ROWS * ROW_TILE, LANES), F32),
        input_output_aliases={3: 0},
        compiler_params=_cparams(2, 58),
        name="moe_experts",
    )(sup_e, sup_row, sup_n, xs_rows, w_up, w_up, b_up3, b_up3, w_down, b_down3)


COMB_TM = 512
COMB_TN = 512
COMB_J = D_MODEL // COMB_TN


def _combine_body(tile0, dest_ref, yb_hbm, h_ref, gt_ref, gp_ref, wg_ref, p_ref, wu_ref, gf_ref, o_ref,
                  gath_ref, n3_ref, sem):
    tm = COMB_TM
    i = pl.program_id(0)
    j = pl.program_id(1)

    @pl.when(j == 0)
    def _():
        base = (tile0 + i) * tm

        def issue(r, carry):
            for k in range(TOP_K):
                _row_copy(yb_hbm, dest_ref[(base + r) * TOP_K + k], gath_ref, k * tm + r, sem).start()
            return carry

        lax.fori_loop(0, tm, issue, 0)

        pltpu.make_async_copy(yb_hbm.at[pl.ds(0, TOP_K * tm * ROW_TILE)], gath_ref, sem).wait()

        gt = gt_ref[...]
        ssq = jnp.zeros((tm, 1), F32)
        for cidx in range(ROW_TILE):
            cols = slice(cidx * LANES, (cidx + 1) * LANES)
            acc = h_ref[:, cols]
            for k in range(TOP_K):
                acc = acc + gt[:, k:k + 1] * gath_ref[pl.ds(k * tm * ROW_TILE + cidx, tm, stride=ROW_TILE), :]
            o_ref[:, cols] = acc
            ssq = ssq + jnp.sum(acc * acc, axis=-1, keepdims=True)
        scale = lax.rsqrt(ssq / D_MODEL + EPS)
        for q in range(COMB_J):
            cols = slice(q * COMB_TN, (q + 1) * COMB_TN)
            n3_ref[:, cols] = (o_ref[:, cols] * scale * gp_ref[:, cols]).astype(BF16)

    for q in range(COMB_J):
        @pl.when(j == q)
        def _():
            cols = slice(q * COMB_TN, (q + 1) * COMB_TN)
            gate = _sigmoid(jnp.dot(n3_ref[...], wg_ref[...].astype(BF16), preferred_element_type=F32))
            up = jnp.dot(p_ref[...].astype(BF16), wu_ref[...].astype(BF16), preferred_element_type=F32)
            o_ref[:, cols] = o_ref[:, cols] + gate * up

    @pl.when(j == COMB_J - 1)
    def _():
        ssq = jnp.zeros((tm, 1), F32)
        for q in range(COMB_J):
            v = o_ref[:, q * COMB_TN:(q + 1) * COMB_TN]
            ssq = ssq + jnp.sum(v * v, axis=-1, keepdims=True)
        scale = lax.rsqrt(ssq / D_MODEL + EPS)
        for q in range(COMB_J):
            cols = slice(q * COMB_TN, (q + 1) * COMB_TN)
            o_ref[:, cols] = o_ref[:, cols] * scale * gf_ref[:, cols]


def _combine(dest_flat, yb_rows, h, gates, g_ple, w_ple_gate, p, w_ple_up, g_final, row0, name):
    tm = COMB_TM
    tile0 = row0 // tm
    n_rows = p.shape[0]
    return pl.pallas_call(
        functools.partial(_combine_body, tile0),
        grid_spec=pltpu.PrefetchScalarGridSpec(
            num_scalar_prefetch=1,
            grid=(n_rows // tm, COMB_J),
            in_specs=[
                pl.BlockSpec(memory_space=pl.ANY),
                pl.BlockSpec((tm, D_MODEL), lambda i, j, d: (i, 0)),
                pl.BlockSpec((tm, LANES), lambda i, j, d: (tile0 + i, 0)),
                pl.BlockSpec((1, D_MODEL), lambda i, j, d: (0, 0)),
                pl.BlockSpec((D_MODEL, COMB_TN), lambda i, j, d: (0, j)),
                pl.BlockSpec((tm, D_PLE), lambda i, j, d: (i, 0)),
                pl.BlockSpec((D_PLE, COMB_TN), lambda i, j, d: (0, j)),
                pl.BlockSpec((1, D_MODEL), lambda i, j, d: (0, 0)),
            ],
            out_specs=pl.BlockSpec((tm, D_MODEL), lambda i, j, d: (i, 0)),
            scratch_shapes=[pltpu.VMEM((TOP_K * tm * ROW_TILE, LANES), F32),
                            pltpu.VMEM((tm, D_MODEL), BF16),
                            pltpu.SemaphoreType.DMA(())]),
        out_shape=jax.ShapeDtypeStruct((n_rows, D_MODEL), F32),
        compiler_params=_cparams(2, 56),
        name=name,
    )(dest_flat, yb_rows, h, gates, g_ple, w_ple_gate, p, w_ple_up, g_final)


def _routing_tables(counts, eidx, rank):
    nsub_e = (counts + MOE_SUB - 1) // MOE_SUB
    prow_e = (jnp.cumsum(nsub_e) - nsub_e) * MOE_SUB
    e_t = eidx.T
    first_row = jnp.zeros_like(e_t)
    for e in range(N_EXPERTS):
        first_row = jnp.where(e_t == e, prow_e[e], first_row)
    dest = (first_row + rank.T).T.reshape(-1).astype(I32)
    csub = jnp.cumsum(nsub_e)
    blk = jnp.arange(MOE_ROWS // MOE_SUB, dtype=I32)
    is_last = jnp.any((blk[:, None] == csub[None, :] - 1) & (nsub_e[None, :] > 0), axis=1)
    zmask = (is_last | (blk >= csub[-1])).astype(I32)
    nsup_e = (nsub_e + MOE_NSB - 1) // MOE_NSB
    csup = jnp.cumsum(nsup_e)
    total = csup[-1]
    s = jnp.arange(MOE_SMAX, dtype=I32)
    e_raw = jnp.minimum(jnp.searchsorted(csup, s, side="right").astype(I32), N_EXPERTS - 1)
    valid = s < total
    e_last = e_raw[jnp.maximum(total - 1, 0)]
    e_s = jnp.where(valid, e_raw, e_last)
    local = s - (csup[e_s] - nsup_e[e_s])
    row_s = jnp.where(valid, prow_e[e_s] + local * MOE_TS, 0)
    n_s = jnp.where(valid, jnp.clip(nsub_e[e_s] - local * MOE_NSB, 0, MOE_NSB), 0)
    return dest, zmask, e_s.astype(I32), row_s.astype(I32), n_s.astype(I32)


def _pad_prev(state, width):
    b, _, c = state.shape
    return jnp.pad(state, ((0, 0), (SUBLANES - (width - 1), 0), (0, 0))).reshape(b * SUBLANES, c)


def kernel(x_prompt, x_sample, p_prompt, p_sample, state_conv_a, state_conv_ssm, state_ssm, g_mix, w_in, conv_a_w, ssm_conv_w, ssm_conv_b, dt_bias, a_log, d_skip, ssm_norm_g, w_branch_a, w_branch_b, w_out, g_ffn, w_router, b_router, w_up, b_up, w_down, b_down, g_ple, w_ple_gate, w_ple_up, g_final):
    xp = x_prompt.reshape(NP_TOK, D_MODEL)
    xs = x_sample.reshape(NS_TOK, D_MODEL)
    g_mix2 = g_mix[0].reshape(1, D_MODEL)

    w_in_t = jnp.swapaxes(w_in, 1, 2).reshape(-1, D_MODEL)
    tn = 512
    proj = _rms_matmul(xp, xs, g_mix2, w_in_t,
                       lambda j: j * tn + (j // (N_MAIN // tn)) * (COL_GATE - N_MAIN),
                       N_PROJ, 1024, tn, "in_proj")
    dt_raw = _rms_matmul(xp, xs, g_mix2, w_in_t, lambda j: COL_DT + j * SSM_HEADS, SSM_HEADS, 1024, SSM_HEADS,
                         "in_proj_dt")

    ya_p, ua_tail = _conv_a_prompt(proj, conv_a_w[0])
    ya_s, ua_s = _conv_a_sample(proj, conv_a_w[0], _pad_prev(state_conv_a[0], CONV_A_W))
    conv_a_prompt = ua_tail[:, SUBLANES - (CONV_A_W - 1):, :]
    conv_a_sample = ua_s.reshape(DEC_BATCH, DEC_SEQ, D_CONV)[:, DEC_SEQ - (CONV_A_W - 1):, :]

    conv_b2 = ssm_conv_b[0].reshape(1, D_XBC)
    xbc_p = _conv_m_prompt(proj, ssm_conv_w[0], conv_b2)
    xbc_s = _conv_m_sample(proj, ssm_conv_w[0], conv_b2, _pad_prev(state_conv_ssm[0], SSM_CONV_W))
    proj3 = proj.reshape(N_TOK // DEC_SEQ, DEC_SEQ, N_PROJ)
    t0 = DEC_SEQ - (SSM_CONV_W - 1)
    conv_m_prompt = proj3[SEQ // DEC_SEQ - 1:NP_TOK // DEC_SEQ:SEQ // DEC_SEQ, t0:, COL_XBC:COL_XBC + D_XBC]
    conv_m_sample = proj3[NP_TOK // DEC_SEQ:, t0:, COL_XBC:COL_XBC + D_XBC]

    ssd_w = (dt_bias[0].reshape(1, SSM_HEADS), dt_bias[0].reshape(SSM_HEADS, 1),
             a_log[0].reshape(1, SSM_HEADS), a_log[0].reshape(SSM_HEADS, 1),
             jnp.repeat(d_skip[0], SSM_HEAD_DIM).reshape(1, D_INNER), ssm_norm_g[0].reshape(1, D_INNER))
    ym_p, ssm_p = _ssd_call(xbc_p, proj, dt_raw, *ssd_w)
    ym_s, ssm_s = _ssd_step_call(xbc_s, proj, dt_raw, *ssd_w, state_ssm)
    ssm_shape = (SSM_HEADS, SSM_HEAD_DIM, SSM_STATE)

    merged_p = _merge(ya_p, ym_p, w_branch_a[0], w_branch_b[0], proj, 0, "merge_prompt")
    merged_s = _merge(ya_s, ym_s.astype(BF16), w_branch_a[0], w_branch_b[0], proj, NP_TOK, "merge_sample")
    h_p = _resid_matmul(merged_p, w_out[0], xp, "out_proj_prompt")
    h_s = _resid_matmul(merged_s, w_out[0], xs, "out_proj_sample")

    wr_pad = jnp.pad(w_router[0], ((0, 0), (0, LANES - N_EXPERTS)))
    br_pad = jnp.pad(b_router[0].reshape(1, N_EXPERTS), ((0, 0), (0, LANES - N_EXPERTS)), constant_values=-1e30)
    g_ffn2 = g_ffn[0].reshape(1, D_MODEL)
    eidx, gate_w, rank, cnt = _router(h_p, h_s, g_ffn2, wr_pad, br_pad)
    counts = cnt[0, :N_EXPERTS].astype(I32)
    dest, zmask, sup_e, sup_row, sup_n = _routing_tables(counts, eidx[:, :TOP_K], rank[:, :TOP_K])
    xs_rows = _dispatch(dest, zmask, h_p, h_s, g_ffn2)
    yb_rows = _experts(sup_e, sup_row, sup_n, xs_rows, w_up[0], b_up[0], w_down[0], b_down[0])

    tail_w = (g_ple[0].reshape(1, D_MODEL), w_ple_gate[0])
    y_prompt = _combine(dest, yb_rows, h_p, gate_w, *tail_w, p_prompt.reshape(NP_TOK, D_PLE), w_ple_up[0],
                        g_final.reshape(1, D_MODEL), 0, "combine_prompt").reshape(BATCH, SEQ, D_MODEL)
    y_sample = _combine(dest, yb_rows, h_s, gate_w, *tail_w, p_sample.reshape(NS_TOK, D_PLE), w_ple_up[0],
                        g_final.reshape(1, D_MODEL), NP_TOK, "combine_sample").reshape(DEC_BATCH, DEC_SEQ, D_MODEL)
    return (y_prompt, y_sample,
            conv_a_prompt[None], conv_m_prompt[None], ssm_p.reshape((1, BATCH) + ssm_shape),
            conv_a_sample[None], conv_m_sample[None], ssm_s)
```
